```python
import math
import jax, jax.numpy as jnp
from jax import lax
import numpy as np

D_MODEL = 4096
BATCH = 4
SEQ = 2048
DEPTH = 1
DEC_BATCH = 128
DEC_SEQ = 4
PAST_LEN = 16384
PAGE_SIZE = 128

N_HEADS = 32
HEAD_K = D_MODEL // N_HEADS
HEAD_V = D_MODEL // N_HEADS
D_HGRN = N_HEADS * HEAD_K
D_CONV = D_MODEL // 2
CONV_WIDTH = 31
N_GROUPS = 4
EXPERTS_PER_GROUP = 8
N_EXPERTS = N_GROUPS * EXPERTS_PER_GROUP
TOP_K = 2
D_EXPERT = D_MODEL // 4
CHUNK = 16
MOE_BLOCK = 128
EPS = 1e-6
SPLITS = [D_HGRN, 2 * D_HGRN, 3 * D_HGRN, 4 * D_HGRN, 4 * D_HGRN + D_CONV,
          4 * D_HGRN + 2 * D_CONV, 4 * D_HGRN + 2 * D_CONV + D_MODEL]
D_IN = 4 * D_HGRN + 2 * D_CONV + 2 * D_MODEL

kernel_name = "hgrn2_conformer_hmoe_adaln_step"


def rmsnorm(x, g):
    xf = x.astype(jnp.float32)
    y = xf * lax.rsqrt(jnp.mean(xf * xf, axis=-1, keepdims=True) + EPS)
    return (y * g.astype(jnp.float32)).astype(x.dtype)


def layernorm(x, g, b):
    xf = x.astype(jnp.float32)
    mu = jnp.mean(xf, axis=-1, keepdims=True)
    var = jnp.mean(jnp.square(xf - mu), axis=-1, keepdims=True)
    y = (xf - mu) * lax.rsqrt(var + EPS) * g.astype(jnp.float32) + b.astype(jnp.float32)
    return y.astype(x.dtype)


def hgrn2_recurrence(q, log_f, k, v, s0):
    B, T, H, K = q.shape
    V = v.shape[-1]
    C = math.gcd(T, CHUNK)
    n = T // C

    def chunks(t):
        return t.astype(jnp.float32).reshape(B, n, C, H, t.shape[-1]).transpose(1, 0, 2, 3, 4)

    causal = (jnp.arange(C)[:, None] >= jnp.arange(C)[None, :])[None, :, :, None, None]

    def step(S, inp):
        qc, lfc, kc, vc = inp
        b = jnp.cumsum(lfc, axis=1)
        inter = jnp.einsum('bthk,bhkv->bthv', qc * jnp.exp(b), S)
        diff = b[:, :, None] - b[:, None, :]
        decay = jnp.exp(jnp.where(causal, diff, -jnp.inf))
        scores = jnp.einsum('bthk,btshk,bshk->btsh', qc, decay, kc)
        intra = jnp.einsum('btsh,bshv->bthv', scores, vc)
        b_last = b[:, -1]
        S_new = jnp.exp(b_last)[..., None] * S + jnp.einsum(
            'bshk,bshv->bhkv', kc * jnp.exp(b_last[:, None] - b), vc)
        return S_new, inter + intra

    s_final, o = lax.scan(step, s0.astype(jnp.float32),
                          (chunks(q), chunks(log_f), chunks(k), chunks(v)))
    return o.transpose(1, 0, 2, 3, 4).reshape(B, T, H, V), s_final


def causal_depthwise_conv(u, buf, w_dw, b_dw):
    full = jnp.concatenate([buf.astype(u.dtype), u], axis=1)
    out = lax.conv_general_dilated(full, w_dw[:, None, :].astype(u.dtype), window_strides=(1,),
                                   padding='VALID', dimension_numbers=('NWC', 'WIO', 'NWC'),
                                   feature_group_count=u.shape[-1])
    new_buf = full[:, full.shape[1] - (CONV_WIDTH - 1):]
    return out + b_dw.astype(u.dtype), new_buf


def hier_moe(h, w_r1, b_r1, w_r2, b_r2, moe_w1, moe_w3, moe_w2):
    N, D = h.shape
    hf = h.astype(jnp.float32)
    l1 = hf @ w_r1.astype(jnp.float32) + b_r1.astype(jnp.float32)
    p1 = jax.nn.softmax(l1, axis=-1)
    grp = jnp.argmax(l1, axis=-1)
    p_grp = jnp.max(p1, axis=-1, keepdims=True)
    l2 = (hf @ w_r2.astype(jnp.float32) + b_r2.astype(jnp.float32)).reshape(N, N_GROUPS, EXPERTS_PER_GROUP)
    l2g = l2[jnp.arange(N), grp]
    top_v, top_i = lax.top_k(l2g, TOP_K)
    weights = p_grp * jax.nn.softmax(top_v, axis=-1)
    expert = grp[:, None] * EXPERTS_PER_GROUP + top_i

    A = N * TOP_K
    e = expert.reshape(-1)
    tok = jnp.repeat(jnp.arange(N, dtype=jnp.int32), TOP_K)
    wt = weights.reshape(-1)
    counts = jnp.bincount(e, length=N_EXPERTS)
    padded = (counts + MOE_BLOCK - 1) // MOE_BLOCK * MOE_BLOCK
    ends = jnp.cumsum(padded)
    starts = ends - padded
    seg_start = jnp.cumsum(counts) - counts
    order = jnp.argsort(e)
    e_s = e[order]
    dest = starts[e_s] + jnp.arange(A) - seg_start[e_s]
    n_blocks = -(-A // MOE_BLOCK) + N_EXPERTS
    L = n_blocks * MOE_BLOCK
    slot_tok = jnp.full((L,), N, jnp.int32).at[dest].set(tok[order])
    slot_w = jnp.zeros((L,), jnp.float32).at[dest].set(wt[order])
    block_e = jnp.minimum(jnp.searchsorted(ends, jnp.arange(n_blocks) * MOE_BLOCK, side='right'),
                          N_EXPERTS - 1)
    h_pad = jnp.concatenate([h, jnp.zeros((1, D), h.dtype)], axis=0)

    def run_block(args):
        idx, eb = args
        xb = h_pad[idx]
        a = xb @ moe_w1[eb]
        b = xb @ moe_w3[eb]
        return (jax.nn.silu(a) * b) @ moe_w2[eb]

    yb = lax.map(run_block, (slot_tok.reshape(n_blocks, MOE_BLOCK), block_e))
    y = yb.reshape(L, D) * slot_w[:, None].astype(h.dtype)
    return jnp.zeros((N + 1, D), h.dtype).at[slot_tok].add(y)[:N]


def decoder_layer(x, c, s0, buf0, g_mix, w_ada, b_ada, w_in, lb, g_onorm, w_dw, b_dw, ln_g, ln_b,
                  w_pw2, b_pw2, w_out, g_ffn, w_r1, b_r1, w_r2, b_r2, moe_w1, moe_w3, moe_w2):
    B, T, D = x.shape
    ada = jax.nn.silu(c) @ w_ada + b_ada
    sh_m, sc_m, gt_m, sh_f, sc_f, gt_f = [t[:, None, :] for t in jnp.split(ada, 6, axis=-1)]

    h = rmsnorm(x, g_mix) * (1 + sc_m) + sh_m
    proj = h @ w_in
    q, f_logit, i_in, o_gate, glu_a, glu_b, gate_a, gate_b = jnp.split(proj, SPLITS, axis=-1)

    f = lb + (1 - lb) * jax.nn.sigmoid(f_logit.astype(jnp.float32))
    log_f = jnp.log(f)
    k = 1 - f
    heads = lambda t: t.reshape(B, T, N_HEADS, t.shape[-1] // N_HEADS)
    o, s_new = hgrn2_recurrence(heads(q), heads(log_f), heads(k), heads(i_in), s0)
    o = rmsnorm(o, g_onorm.reshape(N_HEADS, HEAD_V)).reshape(B, T, D_HGRN).astype(x.dtype)
    y_a = o * jax.nn.silu(o_gate)

    u = glu_a * jax.nn.sigmoid(glu_b)
    conv, new_buf = causal_depthwise_conv(u, buf0, w_dw, b_dw)
    y_b = jax.nn.silu(layernorm(conv, ln_g, ln_b)) @ w_pw2 + b_pw2

    merged = jax.nn.sigmoid(gate_a) * y_a + jax.nn.sigmoid(gate_b) * y_b
    x = x + gt_m * (merged @ w_out)

    h2 = rmsnorm(x, g_ffn) * (1 + sc_f) + sh_f
    moe = hier_moe(h2.reshape(B * T, D), w_r1, b_r1, w_r2, b_r2, moe_w1, moe_w3, moe_w2)
    x = x + gt_f * moe.reshape(B, T, D)
    return x, s_new, new_buf


def setup_inputs(seed: int = 0) -> dict:
    key = jax.random.key(seed)
    ks = jax.random.split(key, 32)
    f32 = jnp.float32
    D = D_MODEL
    nrm = lambda k, shape, s: jax.random.normal(k, shape, f32) * s
    return {
        "x_prompt": nrm(ks[0], (BATCH, SEQ, D), 1.0),
        "x_sample": nrm(ks[1], (DEC_BATCH, DEC_SEQ, D), 1.0),
        "state_hgrn": nrm(ks[2], (DEPTH, DEC_BATCH, N_HEADS, HEAD_K, HEAD_V), 0.5),
        "cache_conv": nrm(ks[3], (DEPTH, DEC_BATCH, CONV_WIDTH - 1, D_CONV), 0.5),
        "c_prompt": nrm(ks[4], (BATCH, D), 1.0),
        "c_sample": nrm(ks[5], (DEC_BATCH, D), 1.0),
        "g_mix": 1.0 + nrm(ks[6], (DEPTH, D), 0.02),
        "w_ada": nrm(ks[7], (DEPTH, D, 6 * D), 0.5 * D ** -0.5),
        "b_ada": nrm(ks[8], (DEPTH, 6 * D), 0.02),
        "w_in": nrm(ks[9], (DEPTH, D, D_IN), D ** -0.5),
        "lb_param": nrm(ks[10], (DEPTH + 1, D_HGRN), 0.1),
        "g_onorm": 1.0 + nrm(ks[11], (DEPTH, D_HGRN), 0.02),
        "w_dw": nrm(ks[12], (DEPTH, CONV_WIDTH, D_CONV), CONV_WIDTH ** -0.5),
        "b_dw": nrm(ks[13], (DEPTH, D_CONV), 0.02),
        "ln_g": 1.0 + nrm(ks[14], (DEPTH, D_CONV), 0.02),
        "ln_b": nrm(ks[15], (DEPTH, D_CONV), 0.02),
        "w_pw2": nrm(ks[16], (DEPTH, D_CONV, D), D_CONV ** -0.5),
        "b_pw2": nrm(ks[17], (DEPTH, D), 0.02),
        "w_out": nrm(ks[18], (DEPTH, D, D), D ** -0.5),
        "g_ffn": 1.0 + nrm(ks[19], (DEPTH, D), 0.02),
        "w_r1": nrm(ks[20], (DEPTH, D, N_GROUPS), D ** -0.5),
        "b_r1": nrm(ks[21], (DEPTH, N_GROUPS), 0.01),
        "w_r2": nrm(ks[22], (DEPTH, D, N_EXPERTS), D ** -0.5),
        "b_r2": nrm(ks[23], (DEPTH, N_EXPERTS), 0.01),
        "moe_w1": nrm(ks[24], (DEPTH, N_EXPERTS, D, D_EXPERT), D ** -0.5),
        "moe_w3": nrm(ks[25], (DEPTH, N_EXPERTS, D, D_EXPERT), D ** -0.5),
        "moe_w2": nrm(ks[26], (DEPTH, N_EXPERTS, D_EXPERT, D), D_EXPERT ** -0.5),
        "g_final": 1.0 + nrm(ks[27], (D,), 0.02),
    }


def reference(x_prompt, x_sample, state_hgrn, cache_conv, c_prompt, c_sample, g_mix, w_ada, b_ada,
              w_in, lb_param, g_onorm, w_dw, b_dw, ln_g, ln_b, w_pw2, b_pw2, w_out, g_ffn,
              w_r1, b_r1, w_r2, b_r2, moe_w1, moe_w3, moe_w2, g_final):
    lb_all = jnp.cumsum(jax.nn.softmax(lb_param.astype(jnp.float32), axis=0), axis=0)
    xp, xs = x_prompt, x_sample
    sp_list, bp_list, ss_list, bs_list = [], [], [], []
    for l in range(DEPTH):
        params = (g_mix[l], w_ada[l], b_ada[l], w_in[l], lb_all[l], g_onorm[l], w_dw[l], b_dw[l],
                  ln_g[l], ln_b[l], w_pw2[l], b_pw2[l], w_out[l], g_ffn[l], w_r1[l], b_r1[l],
                  w_r2[l], b_r2[l], moe_w1[l], moe_w3[l], moe_w2[l])
        s0_p = jnp.zeros((BATCH, N_HEADS, HEAD_K, HEAD_V), jnp.float32)
        buf0_p = jnp.zeros((BATCH, CONV_WIDTH - 1, D_CONV), x_prompt.dtype)
        xp, sp, bp = decoder_layer(xp, c_prompt, s0_p, buf0_p, *params)
        xs, ss, bs = decoder_layer(xs, c_sample, state_hgrn[l], cache_conv[l], *params)
        sp_list.append(sp)
        bp_list.append(bp)
        ss_list.append(ss)
        bs_list.append(bs)
    y_prompt = rmsnorm(xp, g_final)
    y_sample = rmsnorm(xs, g_final)
    return (y_prompt, y_sample, jnp.stack(sp_list), jnp.stack(bp_list), jnp.stack(ss_list), jnp.stack(bs_list))
```

```python
import functools

import jax
import jax.numpy as jnp
from jax import lax
from jax.experimental import pallas as pl
from jax.experimental.pallas import tpu as pltpu

F32 = jnp.float32
BF16 = jnp.bfloat16
EPS = 1e-6
TOP_K = 2
LANES = 128
SUBLANES = 8
VMEM_LIMIT = 56 * 1024 * 1024
FACTORED_DECAY_MAX_SPREAD = 150.0


def _params(*sem):
    return pltpu.CompilerParams(dimension_semantics=sem, vmem_limit_bytes=VMEM_LIMIT)


def _tile(n, target, mult):
    best = None
    for t in range(mult, min(n, target) + 1, mult):
        if n % t == 0:
            best = t
    assert best is not None, (n, target, mult)
    return best


def _sigmoid(x):
    return 1.0 / (1.0 + jnp.exp(-x))


def _silu(x):
    return x * _sigmoid(x)


def _ada_body(c_ref, w_ref, b_ref, o_ref):
    s = _silu(c_ref[...]).astype(BF16)
    o_ref[...] = jnp.dot(s, w_ref[...].astype(BF16), preferred_element_type=F32) + b_ref[...]


def _ada_call(c_all, w_ada, b_ada):
    bp, d = c_all.shape
    n6 = w_ada.shape[1]
    tn = _tile(n6, 512, LANES)
    return pl.pallas_call(
        _ada_body,
        grid=(n6 // tn,),
        in_specs=[pl.BlockSpec((bp, d), lambda j: (0, 0)),
                  pl.BlockSpec((d, tn), lambda j: (0, j)),
                  pl.BlockSpec((1, tn), lambda j: (0, j))],
        out_specs=pl.BlockSpec((bp, tn), lambda j: (0, j)),
        out_shape=jax.ShapeDtypeStruct((bp, n6), F32),
        compiler_params=_params("arbitrary"),
        name="ada",
    )(c_all, w_ada, b_ada.reshape(1, n6))


def _rms(x, g):
    return x * lax.rsqrt(jnp.mean(x * x, axis=-1, keepdims=True) + EPS) * g


def _norm1_body(x_ref, g_ref, sc_ref, sh_ref, *rest):
    o_ref = rest[-1]
    y = _rms(x_ref[...], g_ref[...])
    o_ref[...] = (y * (1.0 + sc_ref[...]) + sh_ref[...]).astype(o_ref.dtype)


def _norm2_body(x_ref, a_ref, gt_ref, g_ref, sc_ref, sh_ref, wr_ref, br_ref, *rest):
    x1_ref, h2_ref, lg_ref = rest[-3:]
    x1 = x_ref[...] + gt_ref[...] * a_ref[...]
    x1_ref[...] = x1
    h2 = _rms(x1, g_ref[...]) * (1.0 + sc_ref[...]) + sh_ref[...]
    h2_ref[...] = h2.astype(h2_ref.dtype)
    lg_ref[...] = jnp.dot(h2, wr_ref[...], preferred_element_type=F32,
                          precision=lax.Precision.HIGHEST) + br_ref[...]


def _final_body(x1_ref, m_ref, gt_ref, g_ref, o_ref):
    x2 = x1_ref[...] + gt_ref[...] * m_ref[...]
    o_ref[...] = _rms(x2, g_ref[...])


class _Rows:
    def __init__(self, np_, t_p, ns, t_s, d):
        self.np_, self.t_p, self.ns, self.t_s, self.d = np_, t_p, ns, t_s, d
        self.tt_p = _tile(t_p, 256, 16)
        self.tt_s = _tile(ns, 128, 16)
        assert np_ % self.tt_s == 0
        self.n = np_ + ns

    def grid(self, sample):
        return (self.ns // self.tt_s,) if sample else (self.np_ // self.tt_p,)

    def rows(self, sample, width, offset=True):
        if sample:
            base = self.np_ // self.tt_s if offset else 0
            return pl.BlockSpec((self.tt_s, width), lambda i: (base + i, 0))
        return pl.BlockSpec((self.tt_p, width), lambda i: (i, 0))

    def mod(self, sample, which):
        if sample:
            return pl.BlockSpec((self.tt_s, self.d), lambda i: (i, which))
        per = self.t_p // self.tt_p
        return pl.BlockSpec((None, None, 1, self.d), lambda i: (i // per, which, 0, 0))

    def const(self, shape):
        return pl.BlockSpec(shape, lambda i: tuple(0 for _ in shape))


def _rowwise(body, rows, name, prompt_in, sample_in, specs, out_widths, out_dtypes):
    outs = None
    for sample, ops in ((False, prompt_in), (True, sample_in)):
        in_specs = [s(sample) for s in specs]
        ops = list(ops)
        aliases = {}
        if outs is not None:
            for k, o in enumerate(outs):
                aliases[len(ops)] = k
                ops.append(o)
                in_specs.append(pl.BlockSpec(memory_space=pl.ANY))
        outs = pl.pallas_call(
            body,
            grid=rows.grid(sample),
            in_specs=in_specs,
            out_specs=[rows.rows(sample, w) for w in out_widths],
            out_shape=[jax.ShapeDtypeStruct((rows.n, w), dt) for w, dt in zip(out_widths, out_dtypes)],
            input_output_aliases=aliases,
            compiler_params=_params("arbitrary"),
            name=name + ("_sample" if sample else "_prompt"),
        )(*ops)
    return outs


def _mm_body(x_ref, w_ref, o_ref, wb_ref):
    @pl.when(pl.program_id(1) == 0)
    def _():
        wb_ref[...] = w_ref[...].astype(BF16)
    o_ref[...] = jnp.dot(x_ref[...], wb_ref[...], preferred_element_type=F32).astype(o_ref.dtype)


def _mm_pw2_body(x_ref, w_ref, b_ref, ma_ref, gb_ref, o_ref, wb_ref):
    @pl.when(pl.program_id(1) == 0)
    def _():
        wb_ref[...] = w_ref[...].astype(BF16)
    yb = jnp.dot(x_ref[...], wb_ref[...], preferred_element_type=F32) + b_ref[...]
    o_ref[...] = (ma_ref[...].astype(F32) + _sigmoid(gb_ref[...]) * yb).astype(o_ref.dtype)


def _mm_call(x, w, out_dtype, tm_target, tn_target, name):
    m, k = x.shape
    n = w.shape[1]
    tm = _tile(m, tm_target, 16)
    tn = _tile(n, tn_target, LANES)
    return pl.pallas_call(
        _mm_body,
        grid=(n // tn, m // tm),
        in_specs=[pl.BlockSpec((tm, k), lambda j, i: (i, 0)),
                  pl.BlockSpec((k, tn), lambda j, i: (0, j))],
        out_specs=pl.BlockSpec((tm, tn), lambda j, i: (i, j)),
        out_shape=jax.ShapeDtypeStruct((m, n), out_dtype),
        scratch_shapes=[pltpu.VMEM((k, tn), BF16)],
        compiler_params=_params("arbitrary", "arbitrary"),
        name=name,
    )(x, w)


def _mm_pw2_call(x, w, bias, ma, proj, gate_b_col):
    m, k = x.shape
    n = w.shape[1]
    tm = _tile(m, 1088, 16)
    tn = _tile(n, 512, LANES)
    assert gate_b_col % tn == 0
    gb0 = gate_b_col // tn
    return pl.pallas_call(
        _mm_pw2_body,
        grid=(n // tn, m // tm),
        in_specs=[pl.BlockSpec((tm, k), lambda j, i: (i, 0)),
                  pl.BlockSpec((k, tn), lambda j, i: (0, j)),
                  pl.BlockSpec((1, tn), lambda j, i: (0, j)),
                  pl.BlockSpec((tm, tn), lambda j, i: (i, j)),
                  pl.BlockSpec((tm, tn), lambda j, i: (i, gb0 + j))],
        out_specs=pl.BlockSpec((tm, tn), lambda j, i: (i, j)),
        out_shape=jax.ShapeDtypeStruct((m, n), BF16),
        scratch_shapes=[pltpu.VMEM((k, tn), BF16)],
        compiler_params=_params("arbitrary", "arbitrary"),
        name="pw2",
    )(x, w, bias.reshape(1, n), ma, proj)


def _gate_epilogue(o, gon, og, ga):
    ya = _rms(o, gon) * _silu(og)
    return (_sigmoid(ga) * ya).astype(BF16)


def _hgrn_prompt_body(q_ref, f_ref, i_ref, og_ref, ga_ref, lb_ref, gon_ref, ma_ref, so_ref,
                      st_ref, b_scr, k_scr, *, chunk, n_sub, hb):
    c_idx = pl.program_id(2)
    C = chunk

    @pl.when(c_idx == 0)
    def _():
        st_ref[...] = jnp.zeros_like(st_ref)

    row_i = lax.broadcasted_iota(jnp.int32, (C, C), 0)
    col_i = lax.broadcasted_iota(jnp.int32, (C, C), 1)
    causal = row_i >= col_i
    tri = causal.astype(F32)

    lb = lb_ref[...]
    f = lb + (1.0 - lb) * _sigmoid(f_ref[...])
    log_f = jnp.log(f)
    k_scr[...] = 1.0 - f
    spread = jnp.zeros((1, hb * LANES), F32)
    for c in range(n_sub):
        rows = slice(c * C, (c + 1) * C)
        b = jnp.dot(tri, log_f[rows], preferred_element_type=F32, precision=lax.Precision.HIGHEST)
        b_scr[rows, :] = b
        spread = jnp.maximum(spread, b[0:1, :] - b[C - 1:C, :])
    factored_ok = jnp.max(spread) < FACTORED_DECAY_MAX_SPREAD

    def pairwise_scores(q, c, sl):
        rows = slice(c * C, (c + 1) * C)
        b = b_scr[rows, sl]
        t_i = lax.broadcasted_iota(jnp.int32, (C, 1), 0)
        s_i = lax.broadcasted_iota(jnp.int32, (1, C), 1)

        def column_group(g, acc):
            grp = pl.ds(pl.multiple_of(c * C + g * SUBLANES, SUBLANES), SUBLANES)
            b_g = b_scr[grp, sl]
            k_g = k_scr[grp, sl]
            for r in range(SUBLANES):
                s = g * SUBLANES + r
                decay = jnp.exp(jnp.where(t_i >= s, b - b_g[r:r + 1], -jnp.inf))
                col = jnp.sum(q * k_g[r:r + 1] * decay, axis=-1, keepdims=True)
                acc = jnp.where(s_i == s, col, acc)
            return acc

        return lax.fori_loop(0, C // SUBLANES, column_group, jnp.zeros((C, C), F32))

    def run(factored):
        for hh in range(hb):
            sl = slice(hh * LANES, (hh + 1) * LANES)
            st = st_ref[hh]
            for c in range(n_sub):
                rows = slice(c * C, (c + 1) * C)
                q = q_ref[rows, sl]
                v = i_ref[rows, sl]
                b = b_scr[rows, sl]
                k = k_scr[rows, sl]
                if factored:
                    b_mid = b[C // 2 - 1:C // 2, :]
                    qd = (q * jnp.exp(b - b_mid)).astype(BF16)
                    kd = (k * jnp.exp(b_mid - b)).astype(BF16)
                    scores = lax.dot_general(qd, kd, (((1,), (1,)), ((), ())),
                                             preferred_element_type=F32)
                    scores = jnp.where(causal, scores, 0.0)
                else:
                    scores = pairwise_scores(q, c, sl)
                v16 = v.astype(BF16)
                intra = jnp.dot(scores.astype(BF16), v16, preferred_element_type=F32)
                qe = (q * jnp.exp(b)).astype(BF16)
                inter = lax.dot_general(qe, st.astype(BF16), (((1,), (1,)), ((), ())),
                                        preferred_element_type=F32)
                b_last = b[C - 1:C, :]
                kl = (k * jnp.exp(b_last - b)).astype(BF16)
                st = st * jnp.exp(b_last) + jnp.dot(v.T.astype(BF16), kl, preferred_element_type=F32)
                ma_ref[rows, sl] = _gate_epilogue(inter + intra, gon_ref[:, sl],
                                                  og_ref[rows, sl], ga_ref[rows, sl])
            st_ref[hh] = st

    @pl.when(factored_ok)
    def _():
        run(True)

    @pl.when(jnp.logical_not(factored_ok))
    def _():
        run(False)

    @pl.when(c_idx == pl.num_programs(2) - 1)
    def _():
        for hh in range(hb):
            so_ref[hh] = st_ref[hh].T


def _hgrn_prompt_call(proj, lb, g_onorm, bsz, t, n_heads, d_conv, n_rows):
    d = n_heads * LANES
    chunk = _tile(t, 128, LANES)
    n_sub = 2 if t % (2 * chunk) == 0 else 1
    tc = chunk * n_sub
    hb = 2 if n_heads % 2 == 0 else 1
    w = hb * LANES
    hcols = n_heads // hb
    ga0 = (4 * d + 2 * d_conv) // w
    per_b = t // tc

    def col(base):
        return pl.BlockSpec((tc, w), lambda b, h, c: (b * per_b + c, base + h))

    body = functools.partial(_hgrn_prompt_body, chunk=chunk, n_sub=n_sub, hb=hb)
    return pl.pallas_call(
        body,
        grid=(bsz, hcols, per_b),
        in_specs=[col(0), col(hcols), col(2 * hcols), col(3 * hcols), col(ga0),
                  pl.BlockSpec((1, w), lambda b, h, c: (0, h)),
                  pl.BlockSpec((1, w), lambda b, h, c: (0, h))],
        out_specs=[pl.BlockSpec((tc, w), lambda b, h, c: (b * per_b + c, h)),
                   pl.BlockSpec((None, hb, LANES, LANES), lambda b, h, c: (b, h, 0, 0))],
        out_shape=[jax.ShapeDtypeStruct((n_rows, d), BF16),
                   jax.ShapeDtypeStruct((bsz, n_heads, LANES, LANES), F32)],
        scratch_shapes=[pltpu.VMEM((hb, LANES, LANES), F32),
                        pltpu.VMEM((tc, w), F32),
                        pltpu.VMEM((tc, w), F32)],
        compiler_params=_params("arbitrary", "arbitrary", "arbitrary"),
        name="hgrn_prompt",
    )(proj, proj, proj, proj, proj, lb, g_onorm)


def _hgrn_sample_body(q_ref, f_ref, i_ref, og_ref, ga_ref, lb_ref, gon_ref, s_ref, ma_in_ref,
                      ma_ref, so_ref, *, t_s, n_heads):
    del ma_in_ref
    nb = SUBLANES // t_s
    row = lax.broadcasted_iota(jnp.int32, (SUBLANES, 1), 0)
    tok = row % t_s
    contract0 = (((0,), (0,)), ((), ()))

    def head(h, carry):
        sl = pl.ds(pl.multiple_of(h * LANES, LANES), LANES)
        q = q_ref[:, sl]
        v = i_ref[:, sl]
        lb = lb_ref[:, sl]
        f = lb + (1.0 - lb) * _sigmoid(f_ref[:, sl])
        log_f = jnp.log(f)
        k = 1.0 - f
        b = log_f
        for d in range(1, t_s):
            b = b + jnp.where(tok >= d, pltpu.roll(log_f, d, 0), 0.0)
        o = jnp.zeros((SUBLANES, LANES), F32)
        for d in range(t_s):
            kd, bd, vd = (k, b, v) if d == 0 else (pltpu.roll(k, d, 0), pltpu.roll(b, d, 0),
                                                  pltpu.roll(v, d, 0))
            decay = jnp.exp(jnp.where(tok >= d, b - bd, -jnp.inf))
            score = jnp.sum(q * kd * decay, axis=-1, keepdims=True)
            o = o + score * jnp.where(tok >= d, vd, 0.0)
        qe = (q * jnp.exp(b)).astype(BF16)
        for bi in range(nb):
            in_b = (row >= bi * t_s) & (row < (bi + 1) * t_s)
            s0 = s_ref[bi, h]
            inter = jnp.dot(qe, s0.astype(BF16), preferred_element_type=F32)
            o = o + jnp.where(in_b, inter, 0.0)
            b_last = b[(bi + 1) * t_s - 1:(bi + 1) * t_s, :]
            kk = jnp.where(in_b, k * jnp.exp(jnp.where(in_b, b_last - b, 0.0)), 0.0)
            vv = jnp.where(in_b, v, 0.0)
            upd = lax.dot_general(kk.astype(BF16), vv.astype(BF16), contract0,
                                  preferred_element_type=F32)
            d_row = jnp.where(row == 0, jnp.exp(b_last), 0.0)
            one_row = jnp.where(row == 0, jnp.ones((SUBLANES, LANES), F32), 0.0)
            decay_kv = lax.dot_general(d_row, one_row, contract0, preferred_element_type=F32,
                                       precision=lax.Precision.HIGHEST)
            so_ref[bi, h] = s0 * decay_kv + upd
        ma_ref[:, sl] = _gate_epilogue(o, gon_ref[:, sl], og_ref[:, sl], ga_ref[:, sl])
        return carry

    lax.fori_loop(0, n_heads, head, 0)


def _hgrn_sample_call(proj, lb, g_onorm, state, ma, row0, t_s, d_conv):
    bs, n_heads = state.shape[0], state.shape[1]
    d = n_heads * LANES
    assert SUBLANES % t_s == 0 and row0 % SUBLANES == 0
    nb = SUBLANES // t_s
    r0 = row0 // SUBLANES
    ga0 = (4 * d + 2 * d_conv) // d

    def col(base):
        return pl.BlockSpec((SUBLANES, d), lambda i: (r0 + i, base))

    body = functools.partial(_hgrn_sample_body, t_s=t_s, n_heads=n_heads)
    return pl.pallas_call(
        body,
        grid=(bs // nb,),
        in_specs=[col(0), col(1), col(2), col(3), col(ga0),
                  pl.BlockSpec((1, d), lambda i: (0, 0)),
                  pl.BlockSpec((1, d), lambda i: (0, 0)),
                  pl.BlockSpec((nb, n_heads, LANES, LANES), lambda i: (i, 0, 0, 0)),
                  pl.BlockSpec(memory_space=pl.ANY)],
        out_specs=[pl.BlockSpec((SUBLANES, d), lambda i: (r0 + i, 0)),
                   pl.BlockSpec((nb, n_heads, LANES, LANES), lambda i: (i, 0, 0, 0))],
        out_shape=[jax.ShapeDtypeStruct(ma.shape, ma.dtype),
                   jax.ShapeDtypeStruct(state.shape, F32)],
        input_output_aliases={8: 0},
        compiler_params=_params("arbitrary"),
        name="hgrn_sample",
    )(proj, proj, proj, proj, proj, lb, g_onorm, state, ma)


def _ln_swish(x, g, b):
    mu = jnp.mean(x, axis=-1, keepdims=True)
    xc = x - mu
    var = jnp.mean(xc * xc, axis=-1, keepdims=True)
    return _silu(xc * lax.rsqrt(var + EPS) * g + b)


def _conv_prompt_body(ga_ref, gb_ref, w_ref, bdw_ref, lg_ref, lbias_ref, o_ref, cache_ref,
                      ext_ref, conv_ref, *, width, hist, rc):
    t_idx = pl.program_id(1)
    tt = ga_ref.shape[0]
    dc = ga_ref.shape[1]

    @pl.when(t_idx == 0)
    def _():
        ext_ref[0:hist, :] = jnp.zeros((hist, dc), F32)

    @pl.when(t_idx > 0)
    def _():
        ext_ref[0:hist, :] = ext_ref[tt:tt + hist, :]

    ext_ref[hist:hist + tt, :] = ga_ref[...] * _sigmoid(gb_ref[...])
    off = hist - (width - 1)

    def strip(l, carry):
        lanes = pl.ds(pl.multiple_of(l * LANES, LANES), LANES)
        for r0 in range(0, tt, rc):
            acc = jnp.zeros((rc, LANES), F32) + bdw_ref[:, lanes]
            for j in range(width):
                acc = acc + w_ref[j:j + 1, lanes] * ext_ref[r0 + off + j:r0 + off + j + rc, lanes]
            conv_ref[r0:r0 + rc, lanes] = acc
        return carry

    lax.fori_loop(0, dc // LANES, strip, 0)

    def norm(i, carry):
        rows = pl.ds(pl.multiple_of(i * rc, rc), rc)
        o_ref[rows, :] = _ln_swish(conv_ref[rows, :], lg_ref[...], lbias_ref[...]).astype(o_ref.dtype)
        return carry

    lax.fori_loop(0, tt // rc, norm, 0)

    @pl.when(t_idx == pl.num_programs(1) - 1)
    def _():
        cache_ref[...] = ext_ref[hist + tt - (width - 1):hist + tt, :]


def _conv_prompt_call(proj, w_dw, b_dw, ln_g, ln_b, bsz, t, d_hgrn, n_rows):
    width, dc = w_dw.shape
    tt = _tile(t, 256, 32)
    hist = 32
    assert width - 1 <= hist <= tt and (4 * d_hgrn) % dc == 0
    c0 = 4 * d_hgrn // dc
    per_b = t // tt
    vec = lambda a: a.reshape(1, dc)
    body = functools.partial(_conv_prompt_body, width=width, hist=hist, rc=32)
    return pl.pallas_call(
        body,
        grid=(bsz, per_b),
        in_specs=[pl.BlockSpec((tt, dc), lambda b, i: (b * per_b + i, c0)),
                  pl.BlockSpec((tt, dc), lambda b, i: (b * per_b + i, c0 + 1)),
                  pl.BlockSpec((width, dc), lambda b, i: (0, 0)),
                  pl.BlockSpec((1, dc), lambda b, i: (0, 0)),
                  pl.BlockSpec((1, dc), lambda b, i: (0, 0)),
                  pl.BlockSpec((1, dc), lambda b, i: (0, 0))],
        out_specs=[pl.BlockSpec((tt, dc), lambda b, i: (b * per_b + i, 0)),
                   pl.BlockSpec((None, width - 1, dc), lambda b, i: (b, 0, 0))],
        out_shape=[jax.ShapeDtypeStruct((n_rows, dc), BF16),
                   jax.ShapeDtypeStruct((bsz, width - 1, dc), F32)],
        scratch_shapes=[pltpu.VMEM((hist + tt, dc), F32), pltpu.VMEM((tt, dc), F32)],
        compiler_params=_params("arbitrary", "arbitrary"),
        name="conv_prompt",
    )(proj, proj, w_dw, vec(b_dw), vec(ln_g), vec(ln_b))


def _conv_sample_body(ga_ref, gb_ref, cache_ref, w_ref, bdw_ref, lg_ref, lbias_ref, vc_in_ref,
                      o_ref, ncache_ref, ext_ref, *, width, t_s, bb):
    del vc_in_ref
    u = ga_ref[...] * _sigmoid(gb_ref[...])
    for bi in range(bb):
        ext_ref[0:width - 1, :] = cache_ref[bi]
        ext_ref[width - 1:width - 1 + t_s, :] = u[bi * t_s:(bi + 1) * t_s]
        acc = jnp.zeros((t_s, u.shape[1]), F32) + bdw_ref[...]
        for j in range(width):
            acc = acc + w_ref[j:j + 1, :] * ext_ref[j:j + t_s, :]
        ncache_ref[bi] = ext_ref[t_s:t_s + width - 1, :]
        o_ref[bi * t_s:(bi + 1) * t_s, :] = _ln_swish(acc, lg_ref[...], lbias_ref[...]).astype(o_ref.dtype)


def _conv_sample_call(proj, cache, w_dw, b_dw, ln_g, ln_b, vconv, row0, t_s, d_hgrn):
    width, dc = w_dw.shape
    bs = cache.shape[0]
    bb = _tile(bs, 8, 1)
    rows = bb * t_s
    assert rows % 16 == 0 and row0 % rows == 0
    r0 = row0 // rows
    c0 = 4 * d_hgrn // dc
    vec = lambda a: a.reshape(1, dc)
    body = functools.partial(_conv_sample_body, width=width, t_s=t_s, bb=bb)
    return pl.pallas_call(
        body,
        grid=(bs // bb,),
        in_specs=[pl.BlockSpec((rows, dc), lambda i: (r0 + i, c0)),
                  pl.BlockSpec((rows, dc), lambda i: (r0 + i, c0 + 1)),
                  pl.BlockSpec((bb, width - 1, dc), lambda i: (i, 0, 0)),
                  pl.BlockSpec((width, dc), lambda i: (0, 0)),
                  pl.BlockSpec((1, dc), lambda i: (0, 0)),
                  pl.BlockSpec((1, dc), lambda i: (0, 0)),
                  pl.BlockSpec((1, dc), lambda i: (0, 0)),
                  pl.BlockSpec(memory_space=pl.ANY)],
        out_specs=[pl.BlockSpec((rows, dc), lambda i: (r0 + i, 0)),
                   pl.BlockSpec((bb, width - 1, dc), lambda i: (i, 0, 0))],
        out_shape=[jax.ShapeDtypeStruct(vconv.shape, vconv.dtype),
                   jax.ShapeDtypeStruct(cache.shape, F32)],
        scratch_shapes=[pltpu.VMEM((width - 1 + t_s + SUBLANES, dc), F32)],
        input_output_aliases={7: 0},
        compiler_params=_params("arbitrary"),
        name="conv_sample",
    )(proj, proj, cache, w_dw, vec(b_dw), vec(ln_g), vec(ln_b), vconv)


def _expert_changed(be_ref, r):
    return jnp.logical_or(r == 0, be_ref[r] != be_ref[jnp.maximum(r - 1, 0)])


def _moe_up_body(be_ref, nr_ref, x_ref, w1_ref, w3_ref, g_ref, w1b_ref, w3b_ref):
    r = pl.program_id(1)
    occupied = r < nr_ref[0]

    @pl.when(jnp.logical_and(occupied, _expert_changed(be_ref, r)))
    def _():
        w1b_ref[...] = w1_ref[...].astype(BF16)
        w3b_ref[...] = w3_ref[...].astype(BF16)

    @pl.when(occupied)
    def _():
        x = x_ref[...]
        a = jnp.dot(x, w1b_ref[...], preferred_element_type=F32)
        b = jnp.dot(x, w3b_ref[...], preferred_element_type=F32)
        g_ref[...] = (_silu(a) * b).astype(g_ref.dtype)


def _moe_down_body(be_ref, nr_ref, g_ref, w2_ref, y_ref, w2b_ref):
    r = pl.program_id(1)
    occupied = r < nr_ref[0]

    @pl.when(jnp.logical_and(occupied, _expert_changed(be_ref, r)))
    def _():
        w2b_ref[...] = w2_ref[...].astype(BF16)

    @pl.when(occupied)
    def _():
        y_ref[...] = jnp.dot(g_ref[...], w2b_ref[...], preferred_element_type=F32)


def _moe_experts(x_sorted, block_e, n_real, w1, w3, w2, tm):
    l, d = x_sorted.shape
    n_exp, _, dh = w1.shape
    nblk = l // tm
    th = _tile(dh, 512, LANES)
    tn = _tile(d, 2048, LANES)
    rr = lambda r, nr: jnp.minimum(r, nr[0] - 1)

    g = pl.pallas_call(
        _moe_up_body,
        grid_spec=pltpu.PrefetchScalarGridSpec(
            num_scalar_prefetch=2,
            grid=(dh // th, nblk),
            in_specs=[pl.BlockSpec((tm, d), lambda j, r, be, nr: (rr(r, nr), 0)),
                      pl.BlockSpec((None, d, th), lambda j, r, be, nr: (be[rr(r, nr)], 0, j)),
                      pl.BlockSpec((None, d, th), lambda j, r, be, nr: (be[rr(r, nr)], 0, j))],
            out_specs=pl.BlockSpec((tm, th), lambda j, r, be, nr: (rr(r, nr), j)),
            scratch_shapes=[pltpu.VMEM((d, th), BF16), pltpu.VMEM((d, th), BF16)]),
        out_shape=jax.ShapeDtypeStruct((l, dh), BF16),
        compiler_params=_params("arbitrary", "arbitrary"),
        name="moe_up",
    )(block_e, n_real, x_sorted, w1, w3)

    return pl.pallas_call(
        _moe_down_body,
        grid_spec=pltpu.PrefetchScalarGridSpec(
            num_scalar_prefetch=2,
            grid=(d // tn, nblk),
            in_specs=[pl.BlockSpec((tm, dh), lambda j, r, be, nr: (rr(r, nr), 0)),
                      pl.BlockSpec((None, dh, tn), lambda j, r, be, nr: (be[rr(r, nr)], 0, j))],
            out_specs=pl.BlockSpec((tm, tn), lambda j, r, be, nr: (rr(r, nr), j)),
            scratch_shapes=[pltpu.VMEM((dh, tn), BF16)]),
        out_shape=jax.ShapeDtypeStruct((l, d), F32),
        compiler_params=_params("arbitrary", "arbitrary"),
        name="moe_down",
    )(block_e, n_real, g, w2)


def _route(logits, n_groups, n_experts, tm):
    n = logits.shape[0]
    eg = n_experts // n_groups
    l1 = logits[:, :n_groups]
    p1 = jax.nn.softmax(l1, axis=-1)
    grp = jnp.argmax(l1, axis=-1)
    p_grp = jnp.max(p1, axis=-1, keepdims=True)
    l2 = logits[:, n_groups:n_groups + n_experts].reshape(n, n_groups, eg)
    l2g = jnp.take_along_axis(l2, grp[:, None, None], axis=1)[:, 0]
    top_v, top_i = lax.top_k(l2g, TOP_K)
    weights = p_grp * jax.nn.softmax(top_v, axis=-1)
    expert = (grp[:, None] * eg + top_i).astype(jnp.int32)

    a = n * TOP_K
    e = expert.reshape(-1)
    tok = jnp.repeat(jnp.arange(n, dtype=jnp.int32), TOP_K)
    counts = jnp.bincount(e, length=n_experts)
    padded = (counts + tm - 1) // tm * tm
    ends = jnp.cumsum(padded)
    starts = ends - padded
    seg_start = jnp.cumsum(counts) - counts
    order = jnp.argsort(e)
    e_s = e[order]
    dest = (starts[e_s] + jnp.arange(a) - seg_start[e_s]).astype(jnp.int32)
    nblk = -(-a // tm) + n_experts
    slot_tok = jnp.zeros((nblk * tm,), jnp.int32).at[dest].set(tok[order])
    pos = jnp.zeros((a,), jnp.int32).at[order].set(dest).reshape(n, TOP_K)
    block_e = jnp.minimum(jnp.searchsorted(ends, jnp.arange(nblk) * tm, side='right'),
                          n_experts - 1).astype(jnp.int32)
    n_real = (ends[-1:] // tm).astype(jnp.int32)
    return weights, slot_tok, pos, block_e, n_real


def kernel(x_prompt, x_sample, state_hgrn, cache_conv, c_prompt, c_sample, g_mix, w_ada, b_ada, w_in, lb_param, g_onorm, w_dw, b_dw, ln_g, ln_b, w_pw2, b_pw2, w_out, g_ffn, w_r1, b_r1, w_r2, b_r2, moe_w1, moe_w3, moe_w2, g_final):
    bp, t_p, d = x_prompt.shape
    bs, t_s, _ = x_sample.shape
    depth, _, n_heads, head_k, head_v = state_hgrn.shape
    assert depth == 1 and head_k == LANES and head_v == LANES
    d_hgrn = n_heads * head_k
    d_conv = w_dw.shape[-1]
    n_groups, n_experts = w_r1.shape[-1], w_r2.shape[-1]
    np_, ns = bp * t_p, bs * t_s
    n = np_ + ns
    rows = _Rows(np_, t_p, ns, t_s, d)
    vec = lambda a: a.reshape(1, -1)

    nb = bp + bs
    nb_pad = -(-nb // 16) * 16
    c_all = jnp.concatenate([c_prompt, c_sample, jnp.zeros((nb_pad - nb, d), F32)], axis=0)
    ada = _ada_call(c_all, w_ada[0], b_ada[0])
    ada_p = ada.reshape(nb_pad, 6, 1, d)
    ada_s = jnp.repeat(ada[bp:nb], t_s, axis=0)
    SH_M, SC_M, GT_M, SH_F, SC_F, GT_F = range(6)
    mod = lambda which: (lambda sample: rows.mod(sample, which))
    ada_of = lambda sample: ada_s if sample else ada_p
    xs = (x_prompt.reshape(np_, d), x_sample.reshape(ns, d))
    x_spec = lambda sample: rows.rows(sample, d, offset=False)
    flat = lambda width: (lambda sample: rows.rows(sample, width))
    const = lambda shape: (lambda sample: rows.const(shape))

    (h,) = _rowwise(
        _norm1_body, rows, "norm1",
        [xs[0], vec(g_mix[0]), ada_p, ada_p], [xs[1], vec(g_mix[0]), ada_s, ada_s],
        [x_spec, const((1, d)), mod(SC_M), mod(SH_M)], [d], [BF16])
    proj = _mm_call(h, w_in[0], F32, 1088, 512, "in_proj")

    lb = jax.nn.softmax(lb_param.astype(F32), axis=0)[0:1]
    ma, state_p = _hgrn_prompt_call(proj, lb, vec(g_onorm[0]), bp, t_p, n_heads, d_conv, n)
    ma, state_s = _hgrn_sample_call(proj, lb, vec(g_onorm[0]), state_hgrn[0], ma, np_, t_s, d_conv)

    vconv, cache_p = _conv_prompt_call(proj, w_dw[0], b_dw[0], ln_g[0], ln_b[0], bp, t_p, d_hgrn, n)
    vconv, cache_s = _conv_sample_call(proj, cache_conv[0], w_dw[0], b_dw[0], ln_g[0], ln_b[0],
                                       vconv, np_, t_s, d_hgrn)

    merged = _mm_pw2_call(vconv, w_pw2[0], b_pw2[0], ma, proj, 4 * d_hgrn + 2 * d_conv + d)
    attn = _mm_call(merged, w_out[0], F32, 1088, 512, "out_proj")

    n_r = n_groups + n_experts
    n_r_pad = -(-n_r // LANES) * LANES
    w_r = jnp.concatenate([w_r1[0], w_r2[0], jnp.zeros((d, n_r_pad - n_r), F32)], axis=1)
    b_r = jnp.concatenate([b_r1[0], b_r2[0], jnp.zeros((n_r_pad - n_r,), F32)]).reshape(1, n_r_pad)
    x1, h2, logits = _rowwise(
        _norm2_body, rows, "norm2",
        [xs[0], attn, ada_p, vec(g_ffn[0]), ada_p, ada_p, w_r, b_r],
        [xs[1], attn, ada_s, vec(g_ffn[0]), ada_s, ada_s, w_r, b_r],
        [x_spec, flat(d), mod(GT_M), const((1, d)), mod(SC_F), mod(SH_F),
         const((d, n_r_pad)), const((1, n_r_pad))],
        [d, d, n_r_pad], [F32, BF16, F32])

    tm = 256
    weights, slot_tok, pos, block_e, n_real = _route(logits, n_groups, n_experts, tm)
    x_sorted = h2[slot_tok]
    y_sorted = _moe_experts(x_sorted, block_e, n_real, moe_w1[0], moe_w3[0], moe_w2[0], tm)
    moe = jnp.sum(y_sorted[pos] * weights[..., None], axis=1)

    outs = []
    for sample in (False, True):
        outs.append(pl.pallas_call(
            _final_body,
            grid=rows.grid(sample),
            in_specs=[rows.rows(sample, d), rows.rows(sample, d), rows.mod(sample, GT_F),
                      rows.const((1, d))],
            out_specs=rows.rows(sample, d, offset=False),
            out_shape=jax.ShapeDtypeStruct((ns if sample else np_, d), F32),
            compiler_params=_params("arbitrary"),
            name="final_sample" if sample else "final_prompt",
        )(x1, moe, ada_of(sample), vec(g_final)))
    y_prompt = outs[0].reshape(bp, t_p, d)
    y_sample = outs[1].reshape(bs, t_s, d)
    return (y_prompt, y_sample, state_p[None], cache_p[None], state_s[None], cache_s[None])
```

```python
import functools

import jax
import jax.numpy as jnp
from jax import lax
from jax.experimental import pallas as pl
from jax.experimental.pallas import tpu as pltpu

F32 = jnp.float32
BF16 = jnp.bfloat16
EPS = 1e-6
TOP_K = 2
LANES = 128
SUBLANES = 8
VMEM_LIMIT = 56 * 1024 * 1024
FACTORED_DECAY_MAX_SPREAD = 150.0


def _params(*sem):
    return pltpu.CompilerParams(dimension_semantics=sem, vmem_limit_bytes=VMEM_LIMIT)


def _tile(n, target, mult):
    best = None
    for t in range(mult, min(n, target) + 1, mult):
        if n % t == 0:
            best = t
    assert best is not None, (n, target, mult)
    return best


def _sigmoid(x):
    return 1.0 / (1.0 + jnp.exp(-x))


def _silu(x):
    return x * _sigmoid(x)


def _ada_body(c_ref, w_ref, b_ref, o_ref):
    s = _silu(c_ref[...]).astype(BF16)
    o_ref[...] = jnp.dot(s, w_ref[...].astype(BF16), preferred_element_type=F32) + b_ref[...]


def _ada_call(c_all, w_ada, b_ada):
    bp, d = c_all.shape
    n6 = w_ada.shape[1]
    tn = _tile(n6, 512, LANES)
    return pl.pallas_call(
        _ada_body,
        grid=(n6 // tn,),
        in_specs=[pl.BlockSpec((bp, d), lambda j: (0, 0)),
                  pl.BlockSpec((d, tn), lambda j: (0, j)),
                  pl.BlockSpec((1, tn), lambda j: (0, j))],
        out_specs=pl.BlockSpec((bp, tn), lambda j: (0, j)),
        out_shape=jax.ShapeDtypeStruct((bp, n6), F32),
        compiler_params=_params("arbitrary"),
        name="ada",
    )(c_all, w_ada, b_ada.reshape(1, n6))


def _rms(x, g):
    return x * lax.rsqrt(jnp.mean(x * x, axis=-1, keepdims=True) + EPS) * g


def _mod(ref, n_rows):
    return ref[...] if ref.shape[0] == n_rows else ref[0:1, :]


def _norm1_body(x_ref, g_ref, sc_ref, sh_ref, *rest):
    o_ref = rest[-1]
    x = x_ref[...]
    tt = x.shape[0]
    y = _rms(x, g_ref[...])
    o_ref[...] = (y * (1.0 + _mod(sc_ref, tt)) + _mod(sh_ref, tt)).astype(o_ref.dtype)


def _norm2_body(x_ref, a_ref, gt_ref, g_ref, sc_ref, sh_ref, wr_ref, br_ref, *rest):
    x1_ref, h2_ref, lg_ref = rest[-3:]
    tt = x_ref.shape[0]
    x1 = x_ref[...] + _mod(gt_ref, tt) * a_ref[...]
    x1_ref[...] = x1
    h2 = _rms(x1, g_ref[...]) * (1.0 + _mod(sc_ref, tt)) + _mod(sh_ref, tt)
    h2_ref[...] = h2
    lg_ref[...] = jnp.dot(h2, wr_ref[...], preferred_element_type=F32,
                          precision=lax.Precision.HIGHEST) + br_ref[...]


def _row_gather_start(idx_ref, base, n_rows, src_hbm, dst, sem):
    def body(i, carry):
        pltpu.make_async_copy(src_hbm.at[pl.ds(idx_ref[base + i], 1)], dst.at[pl.ds(i, 1)], sem).start()
        return carry

    lax.fori_loop(0, n_rows, body, 0, unroll=8)


def _row_gather_wait(n_rows, src_hbm, dst, sem):
    pltpu.make_async_copy(src_hbm.at[pl.ds(0, n_rows)], dst, sem).wait()


def _final_body(pos_ref, x1_ref, w_ref, gt_ref, g_ref, y_hbm, o_ref, buf_ref, sem_ref, *, row0):
    i = pl.program_id(0)
    tt = x1_ref.shape[0]

    def start(step, slot):
        for k in range(TOP_K):
            _row_gather_start(pos_ref, k * pos_ref.shape[0] // TOP_K + row0 + step * tt, tt, y_hbm,
                              buf_ref.at[slot, k], sem_ref.at[slot])

    @pl.when(i == 0)
    def _():
        start(0, 0)

    @pl.when(i + 1 < pl.num_programs(0))
    def _():
        start(i + 1, (i + 1) % 2)

    slot = i % 2
    for k in range(TOP_K):
        _row_gather_wait(tt, y_hbm, buf_ref.at[slot, k], sem_ref.at[slot])
    per_row_gate = gt_ref.shape[0] == tt

    def row_group(c, carry):
        rs = pl.ds(pl.multiple_of(c * SUBLANES, SUBLANES), SUBLANES)
        moe = jnp.zeros((SUBLANES, x1_ref.shape[1]), F32)
        for k in range(TOP_K):
            moe = moe + w_ref[rs, k:k + 1] * buf_ref[slot, k, rs, :]
        gate = gt_ref[rs, :] if per_row_gate else gt_ref[0:1, :]
        o_ref[rs, :] = _rms(x1_ref[rs, :] + gate * moe, g_ref[...])
        return carry

    lax.fori_loop(0, tt // SUBLANES, row_group, 0, unroll=2)


class _Rows:
    def __init__(self, np_, t_p, ns, t_s, d):
        self.np_, self.t_p, self.ns, self.t_s, self.d = np_, t_p, ns, t_s, d
        self.tt_p = _tile(t_p, 256, 16)
        self.tt_s = _tile(ns, 128, 16)
        assert np_ % self.tt_s == 0
        self.n = np_ + ns

    def grid(self, sample):
        return (self.ns // self.tt_s,) if sample else (self.np_ // self.tt_p,)

    def rows(self, sample, width, offset=True):
        if sample:
            base = self.np_ // self.tt_s if offset else 0
            return pl.BlockSpec((self.tt_s, width), lambda i: (base + i, 0))
        return pl.BlockSpec((self.tt_p, width), lambda i: (i, 0))

    def mod(self, sample, which):
        if sample:
            return pl.BlockSpec((self.tt_s, self.d), lambda i: (i, which))
        per = self.t_p // self.tt_p
        return pl.BlockSpec((SUBLANES, self.d), lambda i: (i // per, which))

    def const(self, shape):
        return pl.BlockSpec(shape, lambda i: tuple(0 for _ in shape))


def _rowwise(body, rows, name, prompt_in, sample_in, specs, out_widths, out_dtypes):
    outs = None
    for sample, ops in ((False, prompt_in), (True, sample_in)):
        in_specs = [s(sample) for s in specs]
        ops = list(ops)
        aliases = {}
        if outs is not None:
            for k, o in enumerate(outs):
                aliases[len(ops)] = k
                ops.append(o)
                in_specs.append(pl.BlockSpec(memory_space=pl.ANY))
        outs = pl.pallas_call(
            body,
            grid=rows.grid(sample),
            in_specs=in_specs,
            out_specs=[rows.rows(sample, w) for w in out_widths],
            out_shape=[jax.ShapeDtypeStruct((rows.n, w), dt) for w, dt in zip(out_widths, out_dtypes)],
            input_output_aliases=aliases,
            compiler_params=_params("arbitrary"),
            name=name + ("_sample" if sample else "_prompt"),
        )(*ops)
    return outs


def _mm_body(x_ref, w_ref, o_ref, wb_ref):
    @pl.when(pl.program_id(1) == 0)
    def _():
        wb_ref[...] = w_ref[...].astype(BF16)
    o_ref[...] = jnp.dot(x_ref[...], wb_ref[...], preferred_element_type=F32).astype(o_ref.dtype)


def _mm_pw2_body(x_ref, w_ref, b_ref, ma_ref, gb_ref, o_ref, wb_ref):
    @pl.when(pl.program_id(1) == 0)
    def _():
        wb_ref[...] = w_ref[...].astype(BF16)
    yb = jnp.dot(x_ref[...], wb_ref[...], preferred_element_type=F32) + b_ref[...]
    o_ref[...] = (ma_ref[...].astype(F32) + _sigmoid(gb_ref[...]) * yb).astype(o_ref.dtype)


def _mm_call(x, w, out_dtype, tm_target, tn_target, name):
    m, k = x.shape
    n = w.shape[1]
    tm = _tile(m, tm_target, 16)
    tn = _tile(n, tn_target, LANES)
    return pl.pallas_call(
        _mm_body,
        grid=(n // tn, m // tm),
        in_specs=[pl.BlockSpec((tm, k), lambda j, i: (i, 0)),
                  pl.BlockSpec((k, tn), lambda j, i: (0, j))],
        out_specs=pl.BlockSpec((tm, tn), lambda j, i: (i, j)),
        out_shape=jax.ShapeDtypeStruct((m, n), out_dtype),
        scratch_shapes=[pltpu.VMEM((k, tn), BF16)],
        compiler_params=_params("arbitrary", "arbitrary"),
        name=name,
    )(x, w)


def _mm_pw2_call(x, w, bias, ma, proj, gate_b_col):
    m, k = x.shape
    n = w.shape[1]
    tm = _tile(m, 1088, 16)
    tn = _tile(n, 512, LANES)
    assert gate_b_col % tn == 0
    gb0 = gate_b_col // tn
    return pl.pallas_call(
        _mm_pw2_body,
        grid=(n // tn, m // tm),
        in_specs=[pl.BlockSpec((tm, k), lambda j, i: (i, 0)),
                  pl.BlockSpec((k, tn), lambda j, i: (0, j)),
                  pl.BlockSpec((1, tn), lambda j, i: (0, j)),
                  pl.BlockSpec((tm, tn), lambda j, i: (i, j)),
                  pl.BlockSpec((tm, tn), lambda j, i: (i, gb0 + j))],
        out_specs=pl.BlockSpec((tm, tn), lambda j, i: (i, j)),
        out_shape=jax.ShapeDtypeStruct((m, n), BF16),
        scratch_shapes=[pltpu.VMEM((k, tn), BF16)],
        compiler_params=_params("arbitrary", "arbitrary"),
        name="pw2",
    )(x, w, bias.reshape(1, n), ma, proj)


def _bf16_terms(x):
    hi = x.astype(BF16).astype(F32)
    r = x - hi
    mid = r.astype(BF16).astype(F32)
    return hi, mid, r - mid


def _gate_epilogue(o, gon, og, ga):
    ya = _rms(o, gon) * _silu(og)
    return (_sigmoid(ga) * ya).astype(BF16)


def _hgrn_prompt_body(q_ref, f_ref, i_ref, og_ref, ga_ref, lb_ref, gon_ref, ma_ref, so_ref,
                      st_ref, b_scr, k_scr, *, chunk, n_sub, hb):
    c_idx = pl.program_id(2)
    C = chunk

    @pl.when(c_idx == 0)
    def _():
        st_ref[...] = jnp.zeros_like(st_ref)

    row_i = lax.broadcasted_iota(jnp.int32, (C, C), 0)
    col_i = lax.broadcasted_iota(jnp.int32, (C, C), 1)
    causal = row_i >= col_i
    tri = causal.astype(BF16)

    lb = lb_ref[...]
    f = lb + (1.0 - lb) * _sigmoid(f_ref[...])
    log_f = jnp.log(f)
    k_scr[...] = 1.0 - f
    spread = jnp.zeros((1, hb * LANES), F32)
    for c in range(n_sub):
        rows = slice(c * C, (c + 1) * C)
        b = sum(jnp.dot(tri, term.astype(BF16), preferred_element_type=F32)
                for term in _bf16_terms(log_f[rows]))
        b_scr[rows, :] = b
        spread = jnp.maximum(spread, b[0:1, :] - b[C - 1:C, :])
    factored_ok = jnp.max(spread) < FACTORED_DECAY_MAX_SPREAD

    def pairwise_scores(q, c, sl):
        rows = slice(c * C, (c + 1) * C)
        b = b_scr[rows, sl]
        t_i = lax.broadcasted_iota(jnp.int32, (C, 1), 0)
        s_i = lax.broadcasted_iota(jnp.int32, (1, C), 1)

        def column_group(g, acc):
            grp = pl.ds(pl.multiple_of(c * C + g * SUBLANES, SUBLANES), SUBLANES)
            b_g = b_scr[grp, sl]
            k_g = k_scr[grp, sl]
            for r in range(SUBLANES):
                s = g * SUBLANES + r
                decay = jnp.exp(jnp.where(t_i >= s, b - b_g[r:r + 1], -jnp.inf))
                col = jnp.sum(q * k_g[r:r + 1] * decay, axis=-1, keepdims=True)
                acc = jnp.where(s_i == s, col, acc)
            return acc

        return lax.fori_loop(0, C // SUBLANES, column_group, jnp.zeros((C, C), F32))

    def run(factored):
        for hh in range(hb):
            sl = slice(hh * LANES, (hh + 1) * LANES)
            st = st_ref[hh]
            for c in range(n_sub):
                rows = slice(c * C, (c + 1) * C)
                q = q_ref[rows, sl]
                v = i_ref[rows, sl]
                b = b_scr[rows, sl]
                k = k_scr[rows, sl]
                if factored:
                    b_mid = b[C // 2 - 1:C // 2, :]
                    qd = (q * jnp.exp(b - b_mid)).astype(BF16)
                    kd = (k * jnp.exp(b_mid - b)).astype(BF16)
                    scores = lax.dot_general(qd, kd, (((1,), (1,)), ((), ())),
                                             preferred_element_type=F32)
                    scores = jnp.where(causal, scores, 0.0)
                else:
                    scores = pairwise_scores(q, c, sl)
                v16 = v.astype(BF16)
                intra = jnp.dot(scores.astype(BF16), v16, preferred_element_type=F32)
                qe = (q * jnp.exp(b)).astype(BF16)
                inter = lax.dot_general(qe, st.astype(BF16), (((1,), (1,)), ((), ())),
                                        preferred_element_type=F32)
                b_last = b[C - 1:C, :]
                kl = (k * jnp.exp(b_last - b)).astype(BF16)
                st = st * jnp.exp(b_last) + jnp.dot(v.T.astype(BF16), kl, preferred_element_type=F32)
                ma_ref[rows, sl] = _gate_epilogue(inter + intra, gon_ref[:, sl],
                                                  og_ref[rows, sl], ga_ref[rows, sl])
            st_ref[hh] = st

    @pl.when(factored_ok)
    def _():
        run(True)

    @pl.when(jnp.logical_not(factored_ok))
    def _():
        run(False)

    @pl.when(c_idx == pl.num_programs(2) - 1)
    def _():
        for hh in range(hb):
            so_ref[hh] = st_ref[hh].T


def _hgrn_prompt_call(proj, lb, g_onorm, bsz, t, n_heads, d_conv, n_rows):
    d = n_heads * LANES
    chunk = _tile(t, 128, LANES)
    n_sub = 2 if t % (2 * chunk) == 0 else 1
    tc = chunk * n_sub
    hb = 4 if n_heads % 4 == 0 else 1
    w = hb * LANES
    hcols = n_heads // hb
    ga0 = (4 * d + 2 * d_conv) // w
    per_b = t // tc

    def col(base):
        return pl.BlockSpec((tc, w), lambda b, h, c: (b * per_b + c, base + h))

    body = functools.partial(_hgrn_prompt_body, chunk=chunk, n_sub=n_sub, hb=hb)
    return pl.pallas_call(
        body,
        grid=(bsz, hcols, per_b),
        in_specs=[col(0), col(hcols), col(2 * hcols), col(3 * hcols), col(ga0),
                  pl.BlockSpec((1, w), lambda b, h, c: (0, h)),
                  pl.BlockSpec((1, w), lambda b, h, c: (0, h))],
        out_specs=[pl.BlockSpec((tc, w), lambda b, h, c: (b * per_b + c, h)),
                   pl.BlockSpec((None, hb, LANES, LANES), lambda b, h, c: (b, h, 0, 0))],
        out_shape=[jax.ShapeDtypeStruct((n_rows, d), BF16),
                   jax.ShapeDtypeStruct((bsz, n_heads, LANES, LANES), F32)],
        scratch_shapes=[pltpu.VMEM((hb, LANES, LANES), F32),
                        pltpu.VMEM((tc, w), F32),
                        pltpu.VMEM((tc, w), F32)],
        compiler_params=_params("arbitrary", "arbitrary", "arbitrary"),
        name="hgrn_prompt",
    )(proj, proj, proj, proj, proj, lb, g_onorm)


def _hgrn_sample_body(q_ref, f_ref, i_ref, og_ref, ga_ref, lb_ref, gon_ref, s_ref, ma_in_ref,
                      ma_ref, so_ref, *, t_s, n_heads):
    del ma_in_ref
    nb = SUBLANES // t_s
    row = lax.broadcasted_iota(jnp.int32, (SUBLANES, 1), 0)
    tok = row % t_s
    contract0 = (((0,), (0,)), ((), ()))

    def head(h, carry):
        sl = pl.ds(pl.multiple_of(h * LANES, LANES), LANES)
        q = q_ref[:, sl]
        v = i_ref[:, sl]
        lb = lb_ref[:, sl]
        f = lb + (1.0 - lb) * _sigmoid(f_ref[:, sl])
        log_f = jnp.log(f)
        k = 1.0 - f
        b = log_f
        for d in range(1, t_s):
            b = b + jnp.where(tok >= d, pltpu.roll(log_f, d, 0), 0.0)
        o = jnp.zeros((SUBLANES, LANES), F32)
        for d in range(t_s):
            kd, bd, vd = (k, b, v) if d == 0 else (pltpu.roll(k, d, 0), pltpu.roll(b, d, 0),
                                                  pltpu.roll(v, d, 0))
            decay = jnp.exp(jnp.where(tok >= d, b - bd, -jnp.inf))
            score = jnp.sum(q * kd * decay, axis=-1, keepdims=True)
            o = o + score * jnp.where(tok >= d, vd, 0.0)
        qe = (q * jnp.exp(b)).astype(BF16)
        for bi in range(nb):
            in_b = (row >= bi * t_s) & (row < (bi + 1) * t_s)
            s0 = s_ref[bi, h]
            inter = jnp.dot(qe, s0.astype(BF16), preferred_element_type=F32)
            o = o + jnp.where(in_b, inter, 0.0)
            b_last = b[(bi + 1) * t_s - 1:(bi + 1) * t_s, :]
            kk = jnp.where(in_b, k * jnp.exp(jnp.where(in_b, b_last - b, 0.0)), 0.0)
            vv = jnp.where(in_b, v, 0.0)
            upd = lax.dot_general(kk.astype(BF16), vv.astype(BF16), contract0,
                                  preferred_element_type=F32)
            hi, mid, lo = _bf16_terms(jnp.exp(b_last))
            d_rows = jnp.where(row == 0, hi, jnp.where(row == 1, mid, jnp.where(row == 2, lo, 0.0)))
            one_rows = jnp.where(row < 3, jnp.ones((SUBLANES, LANES), F32), 0.0)
            decay_kv = lax.dot_general(d_rows.astype(BF16), one_rows.astype(BF16), contract0,
                                       preferred_element_type=F32)
            so_ref[bi, h] = s0 * decay_kv + upd
        ma_ref[:, sl] = _gate_epilogue(o, gon_ref[:, sl], og_ref[:, sl], ga_ref[:, sl])
        return carry

    lax.fori_loop(0, n_heads, head, 0, unroll=4 if n_heads % 4 == 0 else 1)


def _hgrn_sample_call(proj, lb, g_onorm, state, ma, row0, t_s, d_conv):
    bs, n_heads = state.shape[0], state.shape[1]
    d = n_heads * LANES
    assert SUBLANES % t_s == 0 and row0 % SUBLANES == 0
    nb = SUBLANES // t_s
    r0 = row0 // SUBLANES
    ga0 = (4 * d + 2 * d_conv) // d

    def col(base):
        return pl.BlockSpec((SUBLANES, d), lambda i: (r0 + i, base))

    body = functools.partial(_hgrn_sample_body, t_s=t_s, n_heads=n_heads)
    return pl.pallas_call(
        body,
        grid=(bs // nb,),
        in_specs=[col(0), col(1), col(2), col(3), col(ga0),
                  pl.BlockSpec((1, d), lambda i: (0, 0)),
                  pl.BlockSpec((1, d), lambda i: (0, 0)),
                  pl.BlockSpec((nb, n_heads, LANES, LANES), lambda i: (i, 0, 0, 0)),
                  pl.BlockSpec(memory_space=pl.ANY)],
        out_specs=[pl.BlockSpec((SUBLANES, d), lambda i: (r0 + i, 0)),
                   pl.BlockSpec((nb, n_heads, LANES, LANES), lambda i: (i, 0, 0, 0))],
        out_shape=[jax.ShapeDtypeStruct(ma.shape, ma.dtype),
                   jax.ShapeDtypeStruct(state.shape, F32)],
        input_output_aliases={8: 0},
        compiler_params=_params("arbitrary"),
        name="hgrn_sample",
    )(proj, proj, proj, proj, proj, lb, g_onorm, state, ma)


def _ln_swish(x, g, b):
    mu = jnp.mean(x, axis=-1, keepdims=True)
    xc = x - mu
    var = jnp.mean(xc * xc, axis=-1, keepdims=True)
    return _silu(xc * lax.rsqrt(var + EPS) * g + b)


def _conv_prompt_body(ga_ref, gb_ref, w_ref, bdw_ref, lg_ref, lbias_ref, o_ref, cache_ref,
                      ext_ref, conv_ref, *, width, hist, rc):
    t_idx = pl.program_id(1)
    tt = ga_ref.shape[0]
    dc = ga_ref.shape[1]

    @pl.when(t_idx == 0)
    def _():
        ext_ref[0:hist, :] = jnp.zeros((hist, dc), F32)

    @pl.when(t_idx > 0)
    def _():
        ext_ref[0:hist, :] = ext_ref[tt:tt + hist, :]

    ext_ref[hist:hist + tt, :] = ga_ref[...] * _sigmoid(gb_ref[...])
    off = hist - (width - 1)

    def strip(l, carry):
        lanes = pl.ds(pl.multiple_of(l * LANES, LANES), LANES)
        for r0 in range(0, tt, rc):
            acc = jnp.zeros((rc, LANES), F32) + bdw_ref[:, lanes]
            for j in range(width):
                acc = acc + w_ref[j:j + 1, lanes] * ext_ref[r0 + off + j:r0 + off + j + rc, lanes]
            conv_ref[r0:r0 + rc, lanes] = acc
        return carry

    lax.fori_loop(0, dc // LANES, strip, 0)

    def norm(i, carry):
        rows = pl.ds(pl.multiple_of(i * rc, rc), rc)
        o_ref[rows, :] = _ln_swish(conv_ref[rows, :], lg_ref[...], lbias_ref[...]).astype(o_ref.dtype)
        return carry

    lax.fori_loop(0, tt // rc, norm, 0)

    @pl.when(t_idx == pl.num_programs(1) - 1)
    def _():
        cache_ref[...] = ext_ref[hist + tt - (width - 1):hist + tt, :]


def _conv_prompt_call(proj, w_dw, b_dw, ln_g, ln_b, bsz, t, d_hgrn, n_rows):
    width, dc = w_dw.shape
    tt = _tile(t, 256, 32)
    hist = 32
    assert width - 1 <= hist <= tt and (4 * d_hgrn) % dc == 0
    c0 = 4 * d_hgrn // dc
    per_b = t // tt
    vec = lambda a: a.reshape(1, dc)
    body = functools.partial(_conv_prompt_body, width=width, hist=hist, rc=32)
    return pl.pallas_call(
        body,
        grid=(bsz, per_b),
        in_specs=[pl.BlockSpec((tt, dc), lambda b, i: (b * per_b + i, c0)),
                  pl.BlockSpec((tt, dc), lambda b, i: (b * per_b + i, c0 + 1)),
                  pl.BlockSpec((width, dc), lambda b, i: (0, 0)),
                  pl.BlockSpec((1, dc), lambda b, i: (0, 0)),
                  pl.BlockSpec((1, dc), lambda b, i: (0, 0)),
                  pl.BlockSpec((1, dc), lambda b, i: (0, 0))],
        out_specs=[pl.BlockSpec((tt, dc), lambda b, i: (b * per_b + i, 0)),
                   pl.BlockSpec((None, width - 1, dc), lambda b, i: (b, 0, 0))],
        out_shape=[jax.ShapeDtypeStruct((n_rows, dc), BF16),
                   jax.ShapeDtypeStruct((bsz, width - 1, dc), F32)],
        scratch_shapes=[pltpu.VMEM((hist + tt, dc), F32), pltpu.VMEM((tt, dc), F32)],
        compiler_params=_params("arbitrary", "arbitrary"),
        name="conv_prompt",
    )(proj, proj, w_dw, vec(b_dw), vec(ln_g), vec(ln_b))


def _conv_sample_body(ga_ref, gb_ref, cache_ref, w_ref, bdw_ref, lg_ref, lbias_ref, vc_in_ref,
                      o_ref, ncache_ref, ext_ref, *, width, t_s, bb):
    del vc_in_ref
    u = ga_ref[...] * _sigmoid(gb_ref[...])
    for bi in range(bb):
        ext_ref[0:width - 1, :] = cache_ref[bi]
        ext_ref[width - 1:width - 1 + t_s, :] = u[bi * t_s:(bi + 1) * t_s]
        acc = jnp.zeros((t_s, u.shape[1]), F32) + bdw_ref[...]
        for j in range(width):
            acc = acc + w_ref[j:j + 1, :] * ext_ref[j:j + t_s, :]
        ncache_ref[bi] = ext_ref[t_s:t_s + width - 1, :]
        o_ref[bi * t_s:(bi + 1) * t_s, :] = _ln_swish(acc, lg_ref[...], lbias_ref[...]).astype(o_ref.dtype)


def _conv_sample_call(proj, cache, w_dw, b_dw, ln_g, ln_b, vconv, row0, t_s, d_hgrn):
    width, dc = w_dw.shape
    bs = cache.shape[0]
    bb = _tile(bs, 8, 1)
    rows = bb * t_s
    assert rows % 16 == 0 and row0 % rows == 0
    r0 = row0 // rows
    c0 = 4 * d_hgrn // dc
    vec = lambda a: a.reshape(1, dc)
    body = functools.partial(_conv_sample_body, width=width, t_s=t_s, bb=bb)
    return pl.pallas_call(
        body,
        grid=(bs // bb,),
        in_specs=[pl.BlockSpec((rows, dc), lambda i: (r0 + i, c0)),
                  pl.BlockSpec((rows, dc), lambda i: (r0 + i, c0 + 1)),
                  pl.BlockSpec((bb, width - 1, dc), lambda i: (i, 0, 0)),
                  pl.BlockSpec((width, dc), lambda i: (0, 0)),
                  pl.BlockSpec((1, dc), lambda i: (0, 0)),
                  pl.BlockSpec((1, dc), lambda i: (0, 0)),
                  pl.BlockSpec((1, dc), lambda i: (0, 0)),
                  pl.BlockSpec(memory_space=pl.ANY)],
        out_specs=[pl.BlockSpec((rows, dc), lambda i: (r0 + i, 0)),
                   pl.BlockSpec((bb, width - 1, dc), lambda i: (i, 0, 0))],
        out_shape=[jax.ShapeDtypeStruct(vconv.shape, vconv.dtype),
                   jax.ShapeDtypeStruct(cache.shape, F32)],
        scratch_shapes=[pltpu.VMEM((width - 1 + t_s + SUBLANES, dc), F32)],
        input_output_aliases={7: 0},
        compiler_params=_params("arbitrary"),
        name="conv_sample",
    )(proj, proj, cache, w_dw, vec(b_dw), vec(ln_g), vec(ln_b), vconv)


def _expert_changed(be_ref, r):
    return jnp.logical_or(r == 0, be_ref[r] != be_ref[jnp.maximum(r - 1, 0)])


def _moe_up_body(be_ref, nr_ref, x_ref, w1_ref, w3_ref, g_ref, w1b_ref, w3b_ref):
    r = pl.program_id(1)
    occupied = r < nr_ref[0]

    @pl.when(jnp.logical_and(occupied, _expert_changed(be_ref, r)))
    def _():
        w1b_ref[...] = w1_ref[...].astype(BF16)
        w3b_ref[...] = w3_ref[...].astype(BF16)

    @pl.when(occupied)
    def _():
        x = x_ref[...]
        a = jnp.dot(x, w1b_ref[...], preferred_element_type=F32)
        b = jnp.dot(x, w3b_ref[...], preferred_element_type=F32)
        g_ref[...] = (_silu(a) * b).astype(g_ref.dtype)


def _moe_down_body(be_ref, nr_ref, g_ref, w2_ref, y_ref, w2b_ref):
    r = pl.program_id(1)
    occupied = r < nr_ref[0]

    @pl.when(jnp.logical_and(occupied, _expert_changed(be_ref, r)))
    def _():
        w2b_ref[...] = w2_ref[...].astype(BF16)

    @pl.when(occupied)
    def _():
        y_ref[...] = jnp.dot(g_ref[...], w2b_ref[...], preferred_element_type=F32)


def _dispatch_body(tok_ref, nr_ref, h2_hbm, o_ref, buf_ref, sem_ref):
    r = pl.program_id(0)
    tm = o_ref.shape[0]

    def start(step, slot):
        _row_gather_start(tok_ref, step * tm, tm, h2_hbm, buf_ref.at[slot], sem_ref.at[slot])

    @pl.when(r == 0)
    def _():
        start(0, 0)

    @pl.when(r + 1 < nr_ref[0])
    def _():
        start(r + 1, (r + 1) % 2)

    @pl.when(r < nr_ref[0])
    def _():
        slot = r % 2
        _row_gather_wait(tm, h2_hbm, buf_ref.at[slot], sem_ref.at[slot])
        o_ref[...] = buf_ref[slot].astype(o_ref.dtype)


def _dispatch_call(h2, slot_tok, n_real, tm):
    n, d = h2.shape
    nblk = slot_tok.shape[0] // tm
    return pl.pallas_call(
        _dispatch_body,
        grid_spec=pltpu.PrefetchScalarGridSpec(
            num_scalar_prefetch=2,
            grid=(nblk,),
            in_specs=[pl.BlockSpec(memory_space=pl.ANY)],
            out_specs=pl.BlockSpec((tm, d), lambda r, tok, nr: (jnp.minimum(r, nr[0] - 1), 0)),
            scratch_shapes=[pltpu.VMEM((2, tm, d), F32), pltpu.SemaphoreType.DMA((2,))]),
        out_shape=jax.ShapeDtypeStruct((nblk * tm, d), BF16),
        compiler_params=_params("arbitrary"),
        name="moe_dispatch",
    )(slot_tok, n_real, h2)


def _moe_experts(x_sorted, block_e, n_real, w1, w3, w2, tm):
    l, d = x_sorted.shape
    n_exp, _, dh = w1.shape
    nblk = l // tm
    th = _tile(dh, 512, LANES)
    tn = _tile(d, 2048, LANES)
    rr = lambda r, nr: jnp.minimum(r, nr[0] - 1)

    g = pl.pallas_call(
        _moe_up_body,
        grid_spec=pltpu.PrefetchScalarGridSpec(
            num_scalar_prefetch=2,
            grid=(dh // th, nblk),
            in_specs=[pl.BlockSpec((tm, d), lambda j, r, be, nr: (rr(r, nr), 0)),
                      pl.BlockSpec((None, d, th), lambda j, r, be, nr: (be[rr(r, nr)], 0, j)),
                      pl.BlockSpec((None, d, th), lambda j, r, be, nr: (be[rr(r, nr)], 0, j))],
            out_specs=pl.BlockSpec((tm, th), lambda j, r, be, nr: (rr(r, nr), j)),
            scratch_shapes=[pltpu.VMEM((d, th), BF16), pltpu.VMEM((d, th), BF16)]),
        out_shape=jax.ShapeDtypeStruct((l, dh), BF16),
        compiler_params=_params("arbitrary", "arbitrary"),
        name="moe_up",
    )(block_e, n_real, x_sorted, w1, w3)

    return pl.pallas_call(
        _moe_down_body,
        grid_spec=pltpu.PrefetchScalarGridSpec(
            num_scalar_prefetch=2,
            grid=(d // tn, nblk),
            in_specs=[pl.BlockSpec((tm, dh), lambda j, r, be, nr: (rr(r, nr), 0)),
                      pl.BlockSpec((None, dh, tn), lambda j, r, be, nr: (be[rr(r, nr)], 0, j))],
            out_specs=pl.BlockSpec((tm, tn), lambda j, r, be, nr: (rr(r, nr), j)),
            scratch_shapes=[pltpu.VMEM((dh, tn), BF16)]),
        out_shape=jax.ShapeDtypeStruct((l, d), F32),
        compiler_params=_params("arbitrary", "arbitrary"),
        name="moe_down",
    )(block_e, n_real, g, w2)


def _route(logits, n_groups, n_experts, tm):
    n = logits.shape[0]
    eg = n_experts // n_groups
    l1 = logits[:, :n_groups]
    p1 = jax.nn.softmax(l1, axis=-1)
    grp = jnp.argmax(l1, axis=-1)
    p_grp = jnp.max(p1, axis=-1, keepdims=True)
    l2 = logits[:, n_groups:n_groups + n_experts].reshape(n, n_groups, eg)
    l2g = jnp.take_along_axis(l2, grp[:, None, None], axis=1)[:, 0]
    top_v, top_i = lax.top_k(l2g, TOP_K)
    weights = p_grp * jax.nn.softmax(top_v, axis=-1)
    expert = (grp[:, None] * eg + top_i).astype(jnp.int32)

    a = n * TOP_K
    e = expert.reshape(-1)
    tok = jnp.repeat(jnp.arange(n, dtype=jnp.int32), TOP_K)
    counts = jnp.bincount(e, length=n_experts)
    padded = (counts + tm - 1) // tm * tm
    ends = jnp.cumsum(padded)
    starts = ends - padded
    seg_start = jnp.cumsum(counts) - counts
    order = jnp.argsort(e)
    e_s = e[order]
    dest = (starts[e_s] + jnp.arange(a) - seg_start[e_s]).astype(jnp.int32)
    nblk = -(-a // tm) + n_experts
    slot_tok = jnp.zeros((nblk * tm,), jnp.int32).at[dest].set(tok[order])
    pos = jnp.zeros((a,), jnp.int32).at[order].set(dest).reshape(n, TOP_K)
    block_e = jnp.minimum(jnp.searchsorted(ends, jnp.arange(nblk) * tm, side='right'),
                          n_experts - 1).astype(jnp.int32)
    n_real = (ends[-1:] // tm).astype(jnp.int32)
    return weights, slot_tok, pos, block_e, n_real


def kernel(x_prompt, x_sample, state_hgrn, cache_conv, c_prompt, c_sample, g_mix, w_ada, b_ada, w_in, lb_param, g_onorm, w_dw, b_dw, ln_g, ln_b, w_pw2, b_pw2, w_out, g_ffn, w_r1, b_r1, w_r2, b_r2, moe_w1, moe_w3, moe_w2, g_final):
    bp, t_p, d = x_prompt.shape
    bs, t_s, _ = x_sample.shape
    depth, _, n_heads, head_k, head_v = state_hgrn.shape
    assert depth == 1 and head_k == LANES and head_v == LANES
    d_hgrn = n_heads * head_k
    d_conv = w_dw.shape[-1]
    n_groups, n_experts = w_r1.shape[-1], w_r2.shape[-1]
    np_, ns = bp * t_p, bs * t_s
    n = np_ + ns
    rows = _Rows(np_, t_p, ns, t_s, d)
    vec = lambda a: a.reshape(1, -1)

    nb = bp + bs
    nb_pad = -(-nb // 16) * 16
    c_all = jnp.concatenate([c_prompt, c_sample, jnp.zeros((nb_pad - nb, d), F32)], axis=0)
    ada = _ada_call(c_all, w_ada[0], b_ada[0])
    ada_p = jnp.repeat(ada[:bp], SUBLANES, axis=0)
    ada_s = jnp.repeat(ada[bp:nb], t_s, axis=0)
    SH_M, SC_M, GT_M, SH_F, SC_F, GT_F = range(6)
    mod = lambda which: (lambda sample: rows.mod(sample, which))
    ada_of = lambda sample: ada_s if sample else ada_p
    xs = (x_prompt.reshape(np_, d), x_sample.reshape(ns, d))
    x_spec = lambda sample: rows.rows(sample, d, offset=False)
    flat = lambda width: (lambda sample: rows.rows(sample, width))
    const = lambda shape: (lambda sample: rows.const(shape))

    (h,) = _rowwise(
        _norm1_body, rows, "norm1",
        [xs[0], vec(g_mix[0]), ada_p, ada_p], [xs[1], vec(g_mix[0]), ada_s, ada_s],
        [x_spec, const((1, d)), mod(SC_M), mod(SH_M)], [d], [BF16])
    proj = _mm_call(h, w_in[0], F32, 1088, 512, "in_proj")

    lb = jax.nn.softmax(lb_param.astype(F32), axis=0)[0:1]
    ma, state_p = _hgrn_prompt_call(proj, lb, vec(g_onorm[0]), bp, t_p, n_heads, d_conv, n)
    ma, state_s = _hgrn_sample_call(proj, lb, vec(g_onorm[0]), state_hgrn[0], ma, np_, t_s, d_conv)

    vconv, cache_p = _conv_prompt_call(proj, w_dw[0], b_dw[0], ln_g[0], ln_b[0], bp, t_p, d_hgrn, n)
    vconv, cache_s = _conv_sample_call(proj, cache_conv[0], w_dw[0], b_dw[0], ln_g[0], ln_b[0],
                                       vconv, np_, t_s, d_hgrn)

    merged = _mm_pw2_call(vconv, w_pw2[0], b_pw2[0], ma, proj, 4 * d_hgrn + 2 * d_conv + d)
    attn = _mm_call(merged, w_out[0], F32, 1088, 512, "out_proj")

    n_r = n_groups + n_experts
    n_r_pad = -(-n_r // LANES) * LANES
    w_r = jnp.concatenate([w_r1[0], w_r2[0], jnp.zeros((d, n_r_pad - n_r), F32)], axis=1)
    b_r = jnp.concatenate([b_r1[0], b_r2[0], jnp.zeros((n_r_pad - n_r,), F32)]).reshape(1, n_r_pad)
    x1, h2, logits = _rowwise(
        _norm2_body, rows, "norm2",
        [xs[0], attn, ada_p, vec(g_ffn[0]), ada_p, ada_p, w_r, b_r],
        [xs[1], attn, ada_s, vec(g_ffn[0]), ada_s, ada_s, w_r, b_r],
        [x_spec, flat(d), mod(GT_M), const((1, d)), mod(SC_F), mod(SH_F),
         const((d, n_r_pad)), const((1, n_r_pad))],
        [d, d, n_r_pad], [F32, F32, F32])

    tm = 256
    weights, slot_tok, pos, block_e, n_real = _route(logits, n_groups, n_experts, tm)
    x_sorted = _dispatch_call(h2, slot_tok, n_real, tm)
    y_sorted = _moe_experts(x_sorted, block_e, n_real, moe_w1[0], moe_w3[0], moe_w2[0], tm)
    pos_flat = pos.T.reshape(-1)

    outs = []
    for sample in (False, True):
        tt = rows.tt_s if sample else rows.tt_p
        wrap = lambda spec: pl.BlockSpec(spec.block_shape, lambda i, p, f=spec.index_map: f(i))
        outs.append(pl.pallas_call(
            functools.partial(_final_body, row0=np_ if sample else 0),
            grid_spec=pltpu.PrefetchScalarGridSpec(
                num_scalar_prefetch=1,
                grid=rows.grid(sample),
                in_specs=[wrap(rows.rows(sample, d)), wrap(rows.rows(sample, TOP_K)),
                          wrap(rows.mod(sample, GT_F)), wrap(rows.const((1, d))),
                          pl.BlockSpec(memory_space=pl.ANY)],
                out_specs=wrap(rows.rows(sample, d, offset=False)),
                scratch_shapes=[pltpu.VMEM((2, TOP_K, tt, d), F32), pltpu.SemaphoreType.DMA((2,))]),
            out_shape=jax.ShapeDtypeStruct((ns if sample else np_, d), F32),
            compiler_params=_params("arbitrary"),
            name="final_sample" if sample else "final_prompt",
        )(pos_flat, x1, weights, ada_of(sample), vec(g_final), y_sorted))
    y_prompt = outs[0].reshape(bp, t_p, d)
    y_sample = outs[1].reshape(bs, t_s, d)
    return (y_prompt, y_sample, state_p[None], cache_p[None], state_s[None], cache_s[None])
```

```python
import functools

import jax
import jax.numpy as jnp
from jax import lax
from jax.experimental import pallas as pl
from jax.experimental.pallas import tpu as pltpu

F32 = jnp.float32
BF16 = jnp.bfloat16
EPS = 1e-6
TOP_K = 2
LANES = 128
SUBLANES = 8
VMEM_LIMIT = 56 * 1024 * 1024
FACTORED_DECAY_MAX_SPREAD = 80.0


def _params(*sem):
    return pltpu.CompilerParams(dimension_semantics=sem, vmem_limit_bytes=VMEM_LIMIT)


def _tile(n, target, mult):
    best = None
    for t in range(mult, min(n, target) + 1, mult):
        if n % t == 0:
            best = t
    assert best is not None, (n, target, mult)
    return best


def _sigmoid(x):
    return 0.5 * jnp.tanh(0.5 * x) + 0.5


def _forget_gate(f_logit, lb):
    return lb + (1.0 - lb) / (1.0 + jnp.exp(-f_logit))


def _silu(x):
    return x * _sigmoid(x)


def _ada_body(c_ref, w_ref, b_ref, o_ref):
    s = _silu(c_ref[...]).astype(BF16)
    o_ref[...] = jnp.dot(s, w_ref[...].astype(BF16), preferred_element_type=F32) + b_ref[...]


def _ada_call(c_all, w_ada, b_ada):
    bp, d = c_all.shape
    n6 = w_ada.shape[1]
    tn = _tile(n6, 512, LANES)
    return pl.pallas_call(
        _ada_body,
        grid=(n6 // tn,),
        in_specs=[pl.BlockSpec((bp, d), lambda j: (0, 0)),
                  pl.BlockSpec((d, tn), lambda j: (0, j)),
                  pl.BlockSpec((1, tn), lambda j: (0, j))],
        out_specs=pl.BlockSpec((bp, tn), lambda j: (0, j)),
        out_shape=jax.ShapeDtypeStruct((bp, n6), F32),
        compiler_params=_params("arbitrary"),
        name="ada",
    )(c_all, w_ada, b_ada.reshape(1, n6))


def _rms(x, g):
    return x * lax.rsqrt(jnp.mean(x * x, axis=-1, keepdims=True) + EPS) * g


def _mod(ref, n_rows):
    return ref[...] if ref.shape[0] == n_rows else ref[0:1, :]


def _norm1_body(x_ref, g_ref, sc_ref, sh_ref, *rest):
    o_ref = rest[-1]
    x = x_ref[...]
    tt = x.shape[0]
    y = _rms(x, g_ref[...])
    o_ref[...] = (y * (1.0 + _mod(sc_ref, tt)) + _mod(sh_ref, tt)).astype(o_ref.dtype)


def _norm2_body(x_ref, a_ref, gt_ref, g_ref, sc_ref, sh_ref, wr_ref, br_ref, *rest):
    x1_ref, h2_ref, lg_ref = rest[-3:]
    tt = x_ref.shape[0]
    x1 = x_ref[...] + _mod(gt_ref, tt) * a_ref[...]
    x1_ref[...] = x1
    h2 = _rms(x1, g_ref[...]) * (1.0 + _mod(sc_ref, tt)) + _mod(sh_ref, tt)
    h2_ref[...] = h2
    lg_ref[...] = jnp.dot(h2, wr_ref[...], preferred_element_type=F32,
                          precision=lax.Precision.HIGHEST) + br_ref[...]


def _row_gather_start(idx_ref, base, n_rows, src_hbm, dst, sem):
    def body(i, carry):
        pltpu.make_async_copy(src_hbm.at[pl.ds(idx_ref[base + i], 1)], dst.at[pl.ds(i, 1)], sem).start()
        return carry

    lax.fori_loop(0, n_rows, body, 0, unroll=8)


def _row_gather_wait(n_rows, src_hbm, dst, sem):
    pltpu.make_async_copy(src_hbm.at[pl.ds(0, n_rows)], dst, sem).wait()


def _final_body(pos_ref, x1_ref, w_ref, gt_ref, g_ref, y_hbm, o_ref, buf_ref, sem_ref, *, row0):
    i = pl.program_id(0)
    tt = x1_ref.shape[0]

    def start(step, slot):
        for k in range(TOP_K):
            _row_gather_start(pos_ref, k * pos_ref.shape[0] // TOP_K + row0 + step * tt, tt, y_hbm,
                              buf_ref.at[slot, k], sem_ref.at[slot])

    @pl.when(i == 0)
    def _():
        start(0, 0)

    @pl.when(i + 1 < pl.num_programs(0))
    def _():
        start(i + 1, (i + 1) % 2)

    slot = i % 2
    for k in range(TOP_K):
        _row_gather_wait(tt, y_hbm, buf_ref.at[slot, k], sem_ref.at[slot])
    per_row_gate = gt_ref.shape[0] == tt

    def row_group(c, carry):
        rs = pl.ds(pl.multiple_of(c * SUBLANES, SUBLANES), SUBLANES)
        moe = jnp.zeros((SUBLANES, x1_ref.shape[1]), F32)
        for k in range(TOP_K):
            moe = moe + w_ref[rs, k:k + 1] * buf_ref[slot, k, rs, :]
        gate = gt_ref[rs, :] if per_row_gate else gt_ref[0:1, :]
        o_ref[rs, :] = _rms(x1_ref[rs, :] + gate * moe, g_ref[...])
        return carry

    lax.fori_loop(0, tt // SUBLANES, row_group, 0, unroll=2)


class _Rows:
    def __init__(self, np_, t_p, ns, t_s, d):
        self.np_, self.t_p, self.ns, self.t_s, self.d = np_, t_p, ns, t_s, d
        self.tt_p = _tile(t_p, 256, 16)
        self.tt_s = _tile(ns, 128, 16)
        assert np_ % self.tt_s == 0
        self.n = np_ + ns

    def grid(self, sample):
        return (self.ns // self.tt_s,) if sample else (self.np_ // self.tt_p,)

    def rows(self, sample, width, offset=True):
        if sample:
            base = self.np_ // self.tt_s if offset else 0
            return pl.BlockSpec((self.tt_s, width), lambda i: (base + i, 0))
        return pl.BlockSpec((self.tt_p, width), lambda i: (i, 0))

    def mod(self, sample, which):
        if sample:
            return pl.BlockSpec((self.tt_s, self.d), lambda i: (i, which))
        per = self.t_p // self.tt_p
        return pl.BlockSpec((SUBLANES, self.d), lambda i: (i // per, which))

    def const(self, shape):
        return pl.BlockSpec(shape, lambda i: tuple(0 for _ in shape))


def _rowwise(body, rows, name, prompt_in, sample_in, specs, out_widths, out_dtypes):
    outs = None
    for sample, ops in ((False, prompt_in), (True, sample_in)):
        in_specs = [s(sample) for s in specs]
        ops = list(ops)
        aliases = {}
        if outs is not None:
            for k, o in enumerate(outs):
                aliases[len(ops)] = k
                ops.append(o)
                in_specs.append(pl.BlockSpec(memory_space=pl.ANY))
        outs = pl.pallas_call(
            body,
            grid=rows.grid(sample),
            in_specs=in_specs,
            out_specs=[rows.rows(sample, w) for w in out_widths],
            out_shape=[jax.ShapeDtypeStruct((rows.n, w), dt) for w, dt in zip(out_widths, out_dtypes)],
            input_output_aliases=aliases,
            compiler_params=_params("arbitrary"),
            name=name + ("_sample" if sample else "_prompt"),
        )(*ops)
    return outs


def _mm_body(x_ref, w_ref, o_ref, wb_ref):
    @pl.when(pl.program_id(1) == 0)
    def _():
        wb_ref[...] = w_ref[...].astype(BF16)
    o_ref[...] = jnp.dot(x_ref[...], wb_ref[...], preferred_element_type=F32).astype(o_ref.dtype)


def _mm_pw2_body(x_ref, w_ref, b_ref, ma_ref, gb_ref, o_ref, wb_ref):
    @pl.when(pl.program_id(1) == 0)
    def _():
        wb_ref[...] = w_ref[...].astype(BF16)
    yb = jnp.dot(x_ref[...], wb_ref[...], preferred_element_type=F32) + b_ref[...]
    o_ref[...] = (ma_ref[...].astype(F32) + _sigmoid(gb_ref[...]) * yb).astype(o_ref.dtype)


def _mm_call(x, w, out_dtype, tm_target, tn_target, name):
    m, k = x.shape
    n = w.shape[1]
    tm = _tile(m, tm_target, 16)
    tn = _tile(n, tn_target, LANES)
    return pl.pallas_call(
        _mm_body,
        grid=(n // tn, m // tm),
        in_specs=[pl.BlockSpec((tm, k), lambda j, i: (i, 0)),
                  pl.BlockSpec((k, tn), lambda j, i: (0, j))],
        out_specs=pl.BlockSpec((tm, tn), lambda j, i: (i, j)),
        out_shape=jax.ShapeDtypeStruct((m, n), out_dtype),
        scratch_shapes=[pltpu.VMEM((k, tn), BF16)],
        compiler_params=_params("arbitrary", "arbitrary"),
        name=name,
    )(x, w)


def _mm_pw2_call(x, w, bias, ma, proj, gate_b_col):
    m, k = x.shape
    n = w.shape[1]
    tm = _tile(m, 1088, 16)
    tn = _tile(n, 512, LANES)
    assert gate_b_col % tn == 0
    gb0 = gate_b_col // tn
    return pl.pallas_call(
        _mm_pw2_body,
        grid=(n // tn, m // tm),
        in_specs=[pl.BlockSpec((tm, k), lambda j, i: (i, 0)),
                  pl.BlockSpec((k, tn), lambda j, i: (0, j)),
                  pl.BlockSpec((1, tn), lambda j, i: (0, j)),
                  pl.BlockSpec((tm, tn), lambda j, i: (i, j)),
                  pl.BlockSpec((tm, tn), lambda j, i: (i, gb0 + j))],
        out_specs=pl.BlockSpec((tm, tn), lambda j, i: (i, j)),
        out_shape=jax.ShapeDtypeStruct((m, n), BF16),
        scratch_shapes=[pltpu.VMEM((k, tn), BF16)],
        compiler_params=_params("arbitrary", "arbitrary"),
        name="pw2",
    )(x, w, bias.reshape(1, n), ma, proj)


def _bf16_terms(x):
    hi = x.astype(BF16).astype(F32)
    r = x - hi
    mid = r.astype(BF16).astype(F32)
    return hi, mid, r - mid


def _gate_epilogue(o, gon, og, ga):
    ya = _rms(o, gon) * _silu(og)
    return (_sigmoid(ga) * ya).astype(BF16)


def _hgrn_prompt_body(q_ref, f_ref, i_ref, og_ref, ga_ref, lb_ref, gon_ref, ma_ref, so_ref,
                      st_ref, b_scr, k_scr, *, chunk, n_sub, hb):
    c_idx = pl.program_id(2)
    C = chunk

    @pl.when(c_idx == 0)
    def _():
        st_ref[...] = jnp.zeros_like(st_ref)

    row_i = lax.broadcasted_iota(jnp.int32, (C, C), 0)
    col_i = lax.broadcasted_iota(jnp.int32, (C, C), 1)
    causal = row_i >= col_i
    tri = causal.astype(BF16)

    lb = lb_ref[...]
    f = _forget_gate(f_ref[...], lb)
    log_f = jnp.log(f)
    k_scr[...] = 1.0 - f
    spread = jnp.zeros((1, hb * LANES), F32)
    for c in range(n_sub):
        rows = slice(c * C, (c + 1) * C)
        b = sum(jnp.dot(tri, term.astype(BF16), preferred_element_type=F32)
                for term in _bf16_terms(log_f[rows]))
        b_scr[rows, :] = b
        b_mid = b[C // 2 - 1:C // 2, :]
        spread = jnp.maximum(spread, jnp.maximum(b[0:1, :] - b_mid, b_mid - b[C - 1:C, :]))
    factored_ok = jnp.max(spread) < FACTORED_DECAY_MAX_SPREAD

    def pairwise_scores(q, c, sl):
        rows = slice(c * C, (c + 1) * C)
        b = b_scr[rows, sl]
        t_i = lax.broadcasted_iota(jnp.int32, (C, 1), 0)
        s_i = lax.broadcasted_iota(jnp.int32, (1, C), 1)

        def column_group(g, acc):
            grp = pl.ds(pl.multiple_of(c * C + g * SUBLANES, SUBLANES), SUBLANES)
            b_g = b_scr[grp, sl]
            k_g = k_scr[grp, sl]
            for r in range(SUBLANES):
                s = g * SUBLANES + r
                decay = jnp.exp(jnp.where(t_i >= s, b - b_g[r:r + 1], -jnp.inf))
                col = jnp.sum(q * k_g[r:r + 1] * decay, axis=-1, keepdims=True)
                acc = jnp.where(s_i == s, col, acc)
            return acc

        return lax.fori_loop(0, C // SUBLANES, column_group, jnp.zeros((C, C), F32))

    def run(factored):
        for hh in range(hb):
            sl = slice(hh * LANES, (hh + 1) * LANES)
            st = st_ref[hh]
            for c in range(n_sub):
                rows = slice(c * C, (c + 1) * C)
                q = q_ref[rows, sl]
                v = i_ref[rows, sl]
                b = b_scr[rows, sl]
                k = k_scr[rows, sl]
                b_last = b[C - 1:C, :]
                if factored:
                    b_mid = b[C // 2 - 1:C // 2, :]
                    q_dec = q * jnp.exp(b - b_mid)
                    k_dec = k * jnp.exp(b_mid - b)
                    scores = lax.dot_general(q_dec.astype(BF16), k_dec.astype(BF16),
                                             (((1,), (1,)), ((), ())), preferred_element_type=F32)
                    scores = jnp.where(causal, scores, 0.0)
                    qe = q_dec * jnp.exp(b_mid)
                    kl = k_dec * jnp.exp(b_last - b_mid)
                else:
                    scores = pairwise_scores(q, c, sl)
                    qe = q * jnp.exp(b)
                    kl = k * jnp.exp(b_last - b)
                intra = jnp.dot(scores.astype(BF16), v.astype(BF16), preferred_element_type=F32)
                inter = lax.dot_general(qe.astype(BF16), st.astype(BF16), (((1,), (1,)), ((), ())),
                                        preferred_element_type=F32)
                st = st * jnp.exp(b_last) + jnp.dot(v.T.astype(BF16), kl.astype(BF16),
                                                    preferred_element_type=F32)
                ma_ref[rows, sl] = _gate_epilogue(inter + intra, gon_ref[:, sl],
                                                  og_ref[rows, sl], ga_ref[rows, sl])
            st_ref[hh] = st

    @pl.when(factored_ok)
    def _():
        run(True)

    @pl.when(jnp.logical_not(factored_ok))
    def _():
        run(False)

    @pl.when(c_idx == pl.num_programs(2) - 1)
    def _():
        for hh in range(hb):
            so_ref[hh] = st_ref[hh].T


def _hgrn_prompt_call(proj, lb, g_onorm, bsz, t, n_heads, d_conv, n_rows):
    d = n_heads * LANES
    chunk = _tile(t, 128, LANES)
    n_sub = 2 if t % (2 * chunk) == 0 else 1
    tc = chunk * n_sub
    hb = 4 if n_heads % 4 == 0 else 1
    w = hb * LANES
    hcols = n_heads // hb
    ga0 = (4 * d + 2 * d_conv) // w
    per_b = t // tc

    def col(base):
        return pl.BlockSpec((tc, w), lambda b, h, c: (b * per_b + c, base + h))

    body = functools.partial(_hgrn_prompt_body, chunk=chunk, n_sub=n_sub, hb=hb)
    return pl.pallas_call(
        body,
        grid=(bsz, hcols, per_b),
        in_specs=[col(0), col(hcols), col(2 * hcols), col(3 * hcols), col(ga0),
                  pl.BlockSpec((1, w), lambda b, h, c: (0, h)),
                  pl.BlockSpec((1, w), lambda b, h, c: (0, h))],
        out_specs=[pl.BlockSpec((tc, w), lambda b, h, c: (b * per_b + c, h)),
                   pl.BlockSpec((None, hb, LANES, LANES), lambda b, h, c: (b, h, 0, 0))],
        out_shape=[jax.ShapeDtypeStruct((n_rows, d), BF16),
                   jax.ShapeDtypeStruct((bsz, n_heads, LANES, LANES), F32)],
        scratch_shapes=[pltpu.VMEM((hb, LANES, LANES), F32),
                        pltpu.VMEM((tc, w), F32),
                        pltpu.VMEM((tc, w), F32)],
        compiler_params=_params("arbitrary", "arbitrary", "arbitrary"),
        name="hgrn_prompt",
    )(proj, proj, proj, proj, proj, lb, g_onorm)


def _hgrn_sample_body(q_ref, f_ref, i_ref, og_ref, ga_ref, lb_ref, gon_ref, s_ref, ma_in_ref,
                      ma_ref, so_ref, *, t_s, n_heads):
    del ma_in_ref
    nb = SUBLANES // t_s
    row = lax.broadcasted_iota(jnp.int32, (SUBLANES, 1), 0)
    tok = row % t_s
    contract0 = (((0,), (0,)), ((), ()))

    def head(h, carry):
        sl = pl.ds(pl.multiple_of(h * LANES, LANES), LANES)
        q = q_ref[:, sl]
        v = i_ref[:, sl]
        lb = lb_ref[:, sl]
        f = _forget_gate(f_ref[:, sl], lb)
        log_f = jnp.log(f)
        k = 1.0 - f
        b = log_f
        for d in range(1, t_s):
            b = b + jnp.where(tok >= d, pltpu.roll(log_f, d, 0), 0.0)
        o = jnp.zeros((SUBLANES, LANES), F32)
        for d in range(t_s):
            kd, bd, vd = (k, b, v) if d == 0 else (pltpu.roll(k, d, 0), pltpu.roll(b, d, 0),
                                                  pltpu.roll(v, d, 0))
            decay = jnp.exp(jnp.where(tok >= d, b - bd, -jnp.inf))
            score = jnp.sum(q * kd * decay, axis=-1, keepdims=True)
            o = o + score * jnp.where(tok >= d, vd, 0.0)
        qe = (q * jnp.exp(b)).astype(BF16)
        for bi in range(nb):
            in_b = (row >= bi * t_s) & (row < (bi + 1) * t_s)
            s0 = s_ref[bi, h]
            inter = jnp.dot(qe, s0.astype(BF16), preferred_element_type=F32)
            o = o + jnp.where(in_b, inter, 0.0)
            b_last = b[(bi + 1) * t_s - 1:(bi + 1) * t_s, :]
            kk = jnp.where(in_b, k * jnp.exp(jnp.where(in_b, b_last - b, 0.0)), 0.0)
            vv = jnp.where(in_b, v, 0.0)
            upd = lax.dot_general(kk.astype(BF16), vv.astype(BF16), contract0,
                                  preferred_element_type=F32)
            hi, mid, lo = _bf16_terms(jnp.exp(b_last))
            d_rows = jnp.where(row == 0, hi, jnp.where(row == 1, mid, jnp.where(row == 2, lo, 0.0)))
            one_rows = jnp.where(row < 3, jnp.ones((SUBLANES, LANES), F32), 0.0)
            decay_kv = lax.dot_general(d_rows.astype(BF16), one_rows.astype(BF16), contract0,
                                       preferred_element_type=F32)
            so_ref[bi, h] = s0 * decay_kv + upd
        ma_ref[:, sl] = _gate_epilogue(o, gon_ref[:, sl], og_ref[:, sl], ga_ref[:, sl])
        return carry

    lax.fori_loop(0, n_heads, head, 0, unroll=4 if n_heads % 4 == 0 else 1)


def _hgrn_sample_call(proj, lb, g_onorm, state, ma, row0, t_s, d_conv):
    bs, n_heads = state.shape[0], state.shape[1]
    d = n_heads * LANES
    assert SUBLANES % t_s == 0 and row0 % SUBLANES == 0
    nb = SUBLANES // t_s
    r0 = row0 // SUBLANES
    ga0 = (4 * d + 2 * d_conv) // d

    def col(base):
        return pl.BlockSpec((SUBLANES, d), lambda i: (r0 + i, base))

    body = functools.partial(_hgrn_sample_body, t_s=t_s, n_heads=n_heads)
    return pl.pallas_call(
        body,
        grid=(bs // nb,),
        in_specs=[col(0), col(1), col(2), col(3), col(ga0),
                  pl.BlockSpec((1, d), lambda i: (0, 0)),
                  pl.BlockSpec((1, d), lambda i: (0, 0)),
                  pl.BlockSpec((nb, n_heads, LANES, LANES), lambda i: (i, 0, 0, 0)),
                  pl.BlockSpec(memory_space=pl.ANY)],
        out_specs=[pl.BlockSpec((SUBLANES, d), lambda i: (r0 + i, 0)),
                   pl.BlockSpec((nb, n_heads, LANES, LANES), lambda i: (i, 0, 0, 0))],
        out_shape=[jax.ShapeDtypeStruct(ma.shape, ma.dtype),
                   jax.ShapeDtypeStruct(state.shape, F32)],
        input_output_aliases={8: 0},
        compiler_params=_params("arbitrary"),
        name="hgrn_sample",
    )(proj, proj, proj, proj, proj, lb, g_onorm, state, ma)


def _ln_swish(x, g, b):
    mu = jnp.mean(x, axis=-1, keepdims=True)
    xc = x - mu
    var = jnp.mean(xc * xc, axis=-1, keepdims=True)
    return _silu(xc * lax.rsqrt(var + EPS) * g + b)


def _conv_prompt_body(ga_ref, gb_ref, w_ref, bdw_ref, lg_ref, lbias_ref, o_ref, cache_ref,
                      ext_ref, conv_ref, *, width, hist, rc):
    t_idx = pl.program_id(1)
    tt = ga_ref.shape[0]
    dc = ga_ref.shape[1]

    @pl.when(t_idx == 0)
    def _():
        ext_ref[0:hist, :] = jnp.zeros((hist, dc), F32)

    @pl.when(t_idx > 0)
    def _():
        ext_ref[0:hist, :] = ext_ref[tt:tt + hist, :]

    ext_ref[hist:hist + tt, :] = ga_ref[...] * _sigmoid(gb_ref[...])
    off = hist - (width - 1)

    def strip(l, carry):
        lanes = pl.ds(pl.multiple_of(l * LANES, LANES), LANES)
        for r0 in range(0, tt, rc):
            win = ext_ref[r0:r0 + rc + hist, lanes]
            acc = jnp.zeros((rc, LANES), F32) + bdw_ref[:, lanes]
            for s in range(SUBLANES):
                taps = [j for j in range(width) if (off + j) % SUBLANES == s]
                if not taps:
                    continue
                shifted = win if s == 0 else pltpu.roll(win, rc + hist - s, 0)
                for j in taps:
                    a = (off + j) // SUBLANES * SUBLANES
                    acc = acc + w_ref[j:j + 1, lanes] * shifted[a:a + rc]
            conv_ref[r0:r0 + rc, lanes] = acc
        return carry

    lax.fori_loop(0, dc // LANES, strip, 0)

    def norm(i, carry):
        rows = pl.ds(pl.multiple_of(i * rc, rc), rc)
        o_ref[rows, :] = _ln_swish(conv_ref[rows, :], lg_ref[...], lbias_ref[...]).astype(o_ref.dtype)
        return carry

    lax.fori_loop(0, tt // rc, norm, 0)

    @pl.when(t_idx == pl.num_programs(1) - 1)
    def _():
        cache_ref[...] = ext_ref[hist + tt - (width - 1):hist + tt, :]


def _conv_prompt_call(proj, w_dw, b_dw, ln_g, ln_b, bsz, t, d_hgrn, n_rows):
    width, dc = w_dw.shape
    tt = _tile(t, 256, 32)
    hist = 32
    assert width - 1 <= hist <= tt and (4 * d_hgrn) % dc == 0
    c0 = 4 * d_hgrn // dc
    per_b = t // tt
    vec = lambda a: a.reshape(1, dc)
    body = functools.partial(_conv_prompt_body, width=width, hist=hist, rc=32)
    return pl.pallas_call(
        body,
        grid=(bsz, per_b),
        in_specs=[pl.BlockSpec((tt, dc), lambda b, i: (b * per_b + i, c0)),
                  pl.BlockSpec((tt, dc), lambda b, i: (b * per_b + i, c0 + 1)),
                  pl.BlockSpec((width, dc), lambda b, i: (0, 0)),
                  pl.BlockSpec((1, dc), lambda b, i: (0, 0)),
                  pl.BlockSpec((1, dc), lambda b, i: (0, 0)),
                  pl.BlockSpec((1, dc), lambda b, i: (0, 0))],
        out_specs=[pl.BlockSpec((tt, dc), lambda b, i: (b * per_b + i, 0)),
                   pl.BlockSpec((None, width - 1, dc), lambda b, i: (b, 0, 0))],
        out_shape=[jax.ShapeDtypeStruct((n_rows, dc), BF16),
                   jax.ShapeDtypeStruct((bsz, width - 1, dc), F32)],
        scratch_shapes=[pltpu.VMEM((hist + tt, dc), F32), pltpu.VMEM((tt, dc), F32)],
        compiler_params=_params("arbitrary", "arbitrary"),
        name="conv_prompt",
    )(proj, proj, w_dw, vec(b_dw), vec(ln_g), vec(ln_b))


def _conv_sample_body(ga_ref, gb_ref, cache_ref, w_ref, bdw_ref, lg_ref, lbias_ref, vc_in_ref,
                      o_ref, ncache_ref, ext_ref, *, width, t_s, bb):
    del vc_in_ref
    u = ga_ref[...] * _sigmoid(gb_ref[...])
    for bi in range(bb):
        ext_ref[0:width - 1, :] = cache_ref[bi]
        ext_ref[width - 1:width - 1 + t_s, :] = u[bi * t_s:(bi + 1) * t_s]
        acc = jnp.zeros((t_s, u.shape[1]), F32) + bdw_ref[...]
        for j in range(width):
            acc = acc + w_ref[j:j + 1, :] * ext_ref[j:j + t_s, :]
        ncache_ref[bi] = ext_ref[t_s:t_s + width - 1, :]
        o_ref[bi * t_s:(bi + 1) * t_s, :] = _ln_swish(acc, lg_ref[...], lbias_ref[...]).astype(o_ref.dtype)


def _conv_sample_call(proj, cache, w_dw, b_dw, ln_g, ln_b, vconv, row0, t_s, d_hgrn):
    width, dc = w_dw.shape
    bs = cache.shape[0]
    bb = _tile(bs, 8, 1)
    rows = bb * t_s
    assert rows % 16 == 0 and row0 % rows == 0
    r0 = row0 // rows
    c0 = 4 * d_hgrn // dc
    vec = lambda a: a.reshape(1, dc)
    body = functools.partial(_conv_sample_body, width=width, t_s=t_s, bb=bb)
    return pl.pallas_call(
        body,
        grid=(bs // bb,),
        in_specs=[pl.BlockSpec((rows, dc), lambda i: (r0 + i, c0)),
                  pl.BlockSpec((rows, dc), lambda i: (r0 + i, c0 + 1)),
                  pl.BlockSpec((bb, width - 1, dc), lambda i: (i, 0, 0)),
                  pl.BlockSpec((width, dc), lambda i: (0, 0)),
                  pl.BlockSpec((1, dc), lambda i: (0, 0)),
                  pl.BlockSpec((1, dc), lambda i: (0, 0)),
                  pl.BlockSpec((1, dc), lambda i: (0, 0)),
                  pl.BlockSpec(memory_space=pl.ANY)],
        out_specs=[pl.BlockSpec((rows, dc), lambda i: (r0 + i, 0)),
                   pl.BlockSpec((bb, width - 1, dc), lambda i: (i, 0, 0))],
        out_shape=[jax.ShapeDtypeStruct(vconv.shape, vconv.dtype),
                   jax.ShapeDtypeStruct(cache.shape, F32)],
        scratch_shapes=[pltpu.VMEM((width - 1 + t_s + SUBLANES, dc), F32)],
        input_output_aliases={7: 0},
        compiler_params=_params("arbitrary"),
        name="conv_sample",
    )(proj, proj, cache, w_dw, vec(b_dw), vec(ln_g), vec(ln_b), vconv)


def _expert_changed(be_ref, r):
    return jnp.logical_or(r == 0, be_ref[r] != be_ref[jnp.maximum(r - 1, 0)])


def _weight_copies(w_hbms, buf_refs, sem_ref, e, j, slot):
    copies = []
    for w_hbm, buf_ref in zip(w_hbms, buf_refs):
        tcol = buf_ref.shape[-1]
        cols = pl.ds(pl.multiple_of(j * tcol, tcol), tcol)
        copies.append(pltpu.make_async_copy(w_hbm.at[e, :, cols], buf_ref.at[slot], sem_ref.at[slot]))
    return copies


def _stage_expert_weights(be_ref, nr_ref, nx_ref, w_hbms, buf_refs, bf16_refs, sem_ref, cnt_ref):
    j, r = pl.program_id(0), pl.program_id(1)

    @pl.when(jnp.logical_and(j == 0, r == 0))
    def _():
        cnt_ref[0] = 0
        for c in _weight_copies(w_hbms, buf_refs, sem_ref, be_ref[0], 0, 0):
            c.start()

    @pl.when(jnp.logical_and(r < nr_ref[0], _expert_changed(be_ref, r)))
    def _():
        slot = cnt_ref[0] % 2
        for c in _weight_copies(w_hbms, buf_refs, sem_ref, be_ref[r], j, slot):
            c.wait()
        wraps = nx_ref[r] >= nr_ref[0]
        e_next = be_ref[jnp.where(wraps, 0, nx_ref[r])]
        j_next = jnp.where(wraps, j + 1, j)

        @pl.when(j_next < pl.num_programs(0))
        def _():
            for c in _weight_copies(w_hbms, buf_refs, sem_ref, e_next, j_next, 1 - slot):
                c.start()

        for buf_ref, dst_ref in zip(buf_refs, bf16_refs):
            dst_ref[...] = buf_ref[slot].astype(BF16)
        cnt_ref[0] = cnt_ref[0] + 1


def _moe_up_body(be_ref, nr_ref, nx_ref, x_ref, w1_hbm, w3_hbm, g_ref,
                 w1f_ref, w3f_ref, w1b_ref, w3b_ref, sem_ref, cnt_ref):
    _stage_expert_weights(be_ref, nr_ref, nx_ref, (w1_hbm, w3_hbm), (w1f_ref, w3f_ref),
                          (w1b_ref, w3b_ref), sem_ref, cnt_ref)

    @pl.when(pl.program_id(1) < nr_ref[0])
    def _():
        x = x_ref[...]
        a = jnp.dot(x, w1b_ref[...], preferred_element_type=F32)
        b = jnp.dot(x, w3b_ref[...], preferred_element_type=F32)
        g_ref[...] = (_silu(a) * b).astype(g_ref.dtype)


def _moe_down_body(be_ref, nr_ref, nx_ref, g_ref, w2_hbm, y_ref, w2f_ref, w2b_ref, sem_ref, cnt_ref):
    _stage_expert_weights(be_ref, nr_ref, nx_ref, (w2_hbm,), (w2f_ref,), (w2b_ref,), sem_ref, cnt_ref)

    @pl.when(pl.program_id(1) < nr_ref[0])
    def _():
        y_ref[...] = jnp.dot(g_ref[...], w2b_ref[...], preferred_element_type=F32)


def _dispatch_body(tok_ref, nr_ref, cnt_ref, h2_hbm, o_ref, buf_ref, sem_ref):
    r = pl.program_id(0)
    tm = o_ref.shape[0]

    def start(step, slot):
        @pl.when(cnt_ref[step] == tm)
        def _():
            _row_gather_start(tok_ref, step * tm, tm, h2_hbm, buf_ref.at[slot], sem_ref.at[slot])

        @pl.when(cnt_ref[step] < tm)
        def _():
            def body(i, carry):
                pltpu.make_async_copy(h2_hbm.at[pl.ds(tok_ref[step * tm + i], 1)],
                                      buf_ref.at[slot, pl.ds(i, 1)], sem_ref.at[slot]).start()
                return carry

            lax.fori_loop(0, cnt_ref[step], body, 0)

    @pl.when(r == 0)
    def _():
        buf_ref[...] = jnp.zeros_like(buf_ref)
        start(0, 0)

    @pl.when(r + 1 < nr_ref[0])
    def _():
        start(r + 1, (r + 1) % 2)

    @pl.when(r < nr_ref[0])
    def _():
        slot = r % 2
        rows = pl.ds(0, pl.multiple_of(cnt_ref[r], SUBLANES))
        pltpu.make_async_copy(h2_hbm.at[rows], buf_ref.at[slot, rows], sem_ref.at[slot]).wait()
        o_ref[...] = buf_ref[slot].astype(o_ref.dtype)


def _dispatch_call(h2, slot_tok, n_real, block_cnt, tm):
    n, d = h2.shape
    nblk = slot_tok.shape[0] // tm
    return pl.pallas_call(
        _dispatch_body,
        grid_spec=pltpu.PrefetchScalarGridSpec(
            num_scalar_prefetch=3,
            grid=(nblk,),
            in_specs=[pl.BlockSpec(memory_space=pl.ANY)],
            out_specs=pl.BlockSpec((tm, d), lambda r, tok, nr, cnt: (jnp.minimum(r, nr[0] - 1), 0)),
            scratch_shapes=[pltpu.VMEM((2, tm, d), F32), pltpu.SemaphoreType.DMA((2,))]),
        out_shape=jax.ShapeDtypeStruct((nblk * tm, d), BF16),
        compiler_params=_params("arbitrary"),
        name="moe_dispatch",
    )(slot_tok, n_real, block_cnt, h2)


def _moe_experts(x_sorted, block_e, n_real, next_run, w1, w3, w2, tm):
    l, d = x_sorted.shape
    n_exp, _, dh = w1.shape
    nblk = l // tm
    th = _tile(dh, 512, LANES)
    tn = _tile(d, 2048, LANES)
    rr = lambda r, nr: jnp.minimum(r, nr[0] - 1)
    hbm = pl.BlockSpec(memory_space=pl.ANY)

    g = pl.pallas_call(
        _moe_up_body,
        grid_spec=pltpu.PrefetchScalarGridSpec(
            num_scalar_prefetch=3,
            grid=(dh // th, nblk),
            in_specs=[pl.BlockSpec((tm, d), lambda j, r, be, nr, nx: (rr(r, nr), 0)), hbm, hbm],
            out_specs=pl.BlockSpec((tm, th), lambda j, r, be, nr, nx: (rr(r, nr), j)),
            scratch_shapes=[pltpu.VMEM((2, d, th), F32), pltpu.VMEM((2, d, th), F32),
                            pltpu.VMEM((d, th), BF16), pltpu.VMEM((d, th), BF16),
                            pltpu.SemaphoreType.DMA((2,)), pltpu.SMEM((1,), jnp.int32)]),
        out_shape=jax.ShapeDtypeStruct((l, dh), BF16),
        compiler_params=_params("arbitrary", "arbitrary"),
        name="moe_up",
    )(block_e, n_real, next_run, x_sorted, w1, w3)

    return pl.pallas_call(
        _moe_down_body,
        grid_spec=pltpu.PrefetchScalarGridSpec(
            num_scalar_prefetch=3,
            grid=(d // tn, nblk),
            in_specs=[pl.BlockSpec((tm, dh), lambda j, r, be, nr, nx: (rr(r, nr), 0)), hbm],
            out_specs=pl.BlockSpec((tm, tn), lambda j, r, be, nr, nx: (rr(r, nr), j)),
            scratch_shapes=[pltpu.VMEM((2, dh, tn), F32), pltpu.VMEM((dh, tn), BF16),
                            pltpu.SemaphoreType.DMA((2,)), pltpu.SMEM((1,), jnp.int32)]),
        out_shape=jax.ShapeDtypeStruct((l, d), F32),
        compiler_params=_params("arbitrary", "arbitrary"),
        name="moe_down",
    )(block_e, n_real, next_run, g, w2)


def _route(logits, n_groups, n_experts, tm):
    n = logits.shape[0]
    eg = n_experts // n_groups
    l1 = logits[:, :n_groups]
    p1 = jax.nn.softmax(l1, axis=-1)
    grp = jnp.argmax(l1, axis=-1)
    p_grp = jnp.max(p1, axis=-1, keepdims=True)
    l2 = logits[:, n_groups:n_groups + n_experts].reshape(n, n_groups, eg)
    l2g = jnp.take_along_axis(l2, grp[:, None, None], axis=1)[:, 0]
    top_v, top_i = lax.top_k(l2g, TOP_K)
    weights = p_grp * jax.nn.softmax(top_v, axis=-1)
    expert = (grp[:, None] * eg + top_i).astype(jnp.int32)

    a = n * TOP_K
    i32 = lambda v: v.astype(jnp.int32)
    e = expert.reshape(-1)
    counts = i32(jnp.bincount(e, length=n_experts))
    padded = (counts + tm - 1) // tm * tm
    ends = jnp.cumsum(padded)
    starts = ends - padded
    seg_start = jnp.cumsum(counts) - counts
    order = jnp.argsort(e)
    rank = jnp.argsort(order)
    pos = i32(starts[e] + rank - seg_start[e]).reshape(n, TOP_K)
    nblk = -(-a // tm) + n_experts
    block_e = i32(jnp.minimum(jnp.searchsorted(ends, jnp.arange(nblk) * tm, side='right'),
                              n_experts - 1))
    n_real = i32(ends[-1:] // tm)
    block_cnt = jnp.clip(counts[block_e] - (jnp.arange(nblk) * tm - starts[block_e]), 0, tm)
    block_cnt = i32(-(-block_cnt // SUBLANES) * SUBLANES)
    next_run = i32(ends[block_e] // tm)
    slot = jnp.arange(nblk * tm)
    slot_e = jnp.repeat(block_e, tm)
    off = slot - starts[slot_e]
    src = order[jnp.clip(seg_start[slot_e] + off, 0, a - 1)] // TOP_K
    slot_tok = i32(jnp.where(off < counts[slot_e], src, 0))
    return weights, slot_tok, pos, block_e, n_real, block_cnt, next_run


def kernel(x_prompt, x_sample, state_hgrn, cache_conv, c_prompt, c_sample, g_mix, w_ada, b_ada, w_in, lb_param, g_onorm, w_dw, b_dw, ln_g, ln_b, w_pw2, b_pw2, w_out, g_ffn, w_r1, b_r1, w_r2, b_r2, moe_w1, moe_w3, moe_w2, g_final):
    bp, t_p, d = x_prompt.shape
    bs, t_s, _ = x_sample.shape
    depth, _, n_heads, head_k, head_v = state_hgrn.shape
    assert depth == 1 and head_k == LANES and head_v == LANES
    d_hgrn = n_heads * head_k
    d_conv = w_dw.shape[-1]
    n_groups, n_experts = w_r1.shape[-1], w_r2.shape[-1]
    np_, ns = bp * t_p, bs * t_s
    n = np_ + ns
    rows = _Rows(np_, t_p, ns, t_s, d)
    vec = lambda a: a.reshape(1, -1)

    nb = bp + bs
    nb_pad = -(-nb // 16) * 16
    c_all = jnp.concatenate([c_prompt, c_sample, jnp.zeros((nb_pad - nb, d), F32)], axis=0)
    ada = _ada_call(c_all, w_ada[0], b_ada[0])
    ada_p = jnp.repeat(ada[:bp], SUBLANES, axis=0)
    ada_s = jnp.repeat(ada[bp:nb], t_s, axis=0)
    SH_M, SC_M, GT_M, SH_F, SC_F, GT_F = range(6)
    mod = lambda which: (lambda sample: rows.mod(sample, which))
    ada_of = lambda sample: ada_s if sample else ada_p
    xs = (x_prompt.reshape(np_, d), x_sample.reshape(ns, d))
    x_spec = lambda sample: rows.rows(sample, d, offset=False)
    flat = lambda width: (lambda sample: rows.rows(sample, width))
    const = lambda shape: (lambda sample: rows.const(shape))

    (h,) = _rowwise(
        _norm1_body, rows, "norm1",
        [xs[0], vec(g_mix[0]), ada_p, ada_p], [xs[1], vec(g_mix[0]), ada_s, ada_s],
        [x_spec, const((1, d)), mod(SC_M), mod(SH_M)], [d], [BF16])
    proj = _mm_call(h, w_in[0], F32, 1088, 512, "in_proj")

    lb = jax.nn.softmax(lb_param.astype(F32), axis=0)[0:1]
    ma, state_p = _hgrn_prompt_call(proj, lb, vec(g_onorm[0]), bp, t_p, n_heads, d_conv, n)
    ma, state_s = _hgrn_sample_call(proj, lb, vec(g_onorm[0]), state_hgrn[0], ma, np_, t_s, d_conv)

    vconv, cache_p = _conv_prompt_call(proj, w_dw[0], b_dw[0], ln_g[0], ln_b[0], bp, t_p, d_hgrn, n)
    vconv, cache_s = _conv_sample_call(proj, cache_conv[0], w_dw[0], b_dw[0], ln_g[0], ln_b[0],
                                       vconv, np_, t_s, d_hgrn)

    merged = _mm_pw2_call(vconv, w_pw2[0], b_pw2[0], ma, proj, 4 * d_hgrn + 2 * d_conv + d)
    attn = _mm_call(merged, w_out[0], F32, 1088, 512, "out_proj")

    n_r = n_groups + n_experts
    n_r_pad = -(-n_r // LANES) * LANES
    w_r = jnp.concatenate([w_r1[0], w_r2[0], jnp.zeros((d, n_r_pad - n_r), F32)], axis=1)
    b_r = jnp.concatenate([b_r1[0], b_r2[0], jnp.zeros((n_r_pad - n_r,), F32)]).reshape(1, n_r_pad)
    x1, h2, logits = _rowwise(
        _norm2_body, rows, "norm2",
        [xs[0], attn, ada_p, vec(g_ffn[0]), ada_p, ada_p, w_r, b_r],
        [xs[1], attn, ada_s, vec(g_ffn[0]), ada_s, ada_s, w_r, b_r],
        [x_spec, flat(d), mod(GT_M), const((1, d)), mod(SC_F), mod(SH_F),
         const((d, n_r_pad)), const((1, n_r_pad))],
        [d, d, n_r_pad], [F32, F32, F32])

    tm = 256
    weights, slot_tok, pos, block_e, n_real, block_cnt, next_run = _route(logits, n_groups, n_experts, tm)
    x_sorted = _dispatch_call(h2, slot_tok, n_real, block_cnt, tm)
    y_sorted = _moe_experts(x_sorted, block_e, n_real, next_run, moe_w1[0], moe_w3[0], moe_w2[0], tm)
    pos_flat = pos.T.reshape(-1)

    outs = []
    for sample in (False, True):
        tt = rows.tt_s if sample else rows.tt_p
        wrap = lambda spec: pl.BlockSpec(spec.block_shape, lambda i, p, f=spec.index_map: f(i))
        outs.append(pl.pallas_call(
            functools.partial(_final_body, row0=np_ if sample else 0),
            grid_spec=pltpu.PrefetchScalarGridSpec(
                num_scalar_prefetch=1,
                grid=rows.grid(sample),
                in_specs=[wrap(rows.rows(sample, d)), wrap(rows.rows(sample, TOP_K)),
                          wrap(rows.mod(sample, GT_F)), wrap(rows.const((1, d))),
                          pl.BlockSpec(memory_space=pl.ANY)],
                out_specs=wrap(rows.rows(sample, d, offset=False)),
                scratch_shapes=[pltpu.VMEM((2, TOP_K, tt, d), F32), pltpu.SemaphoreType.DMA((2,))]),
            out_shape=jax.ShapeDtypeStruct((ns if sample else np_, d), F32),
            compiler_params=_params("arbitrary"),
            name="final_sample" if sample else "final_prompt",
        )(pos_flat, x1, weights, ada_of(sample), vec(g_final), y_sorted))
    y_prompt = outs[0].reshape(bp, t_p, d)
    y_sample = outs[1].reshape(bs, t_s, d)
    return (y_prompt, y_sample, state_p[None], cache_p[None], state_s[None], cache_s[None])
```

```python
import functools

import jax
import jax.numpy as jnp
from jax import lax
from jax.experimental import pallas as pl
from jax.experimental.pallas import tpu as pltpu

F32 = jnp.float32
BF16 = jnp.bfloat16
EPS = 1e-6
TOP_K = 2
LANES = 128
SUBLANES = 8
VMEM_LIMIT = 56 * 1024 * 1024
FACTORED_DECAY_MAX_SPREAD = 80.0


def _params(*sem):
    return pltpu.CompilerParams(dimension_semantics=sem, vmem_limit_bytes=VMEM_LIMIT)


def _tile(n, target, mult):
    best = None
    for t in range(mult, min(n, target) + 1, mult):
        if n % t == 0:
            best = t
    assert best is not None, (n, target, mult)
    return best


def _sigmoid(x):
    return 0.5 * jnp.tanh(0.5 * x) + 0.5


def _forget_gate(f_logit, lb):
    return lb + (1.0 - lb) / (1.0 + jnp.exp(-f_logit))


def _silu(x):
    return x * _sigmoid(x)


def _ada_body(c_ref, w_ref, b_ref, o_ref, *, bs, t_s, bp):
    s = _silu(c_ref[...]).astype(BF16)
    ada = jnp.dot(s, w_ref[...].astype(BF16), preferred_element_type=F32) + b_ref[...]
    ns = bs * t_s
    r_i = lax.broadcasted_iota(jnp.int32, (ns, bs), 0)
    b_i = lax.broadcasted_iota(jnp.int32, (ns, bs), 1)
    lo = b_i * t_s
    expand = jnp.where((r_i >= lo) & (r_i < lo + t_s), 1.0, 0.0).astype(BF16)
    o_ref[0:ns, :] = sum(jnp.dot(expand, term.astype(BF16), preferred_element_type=F32)
                         for term in _bf16_terms(ada[0:bs]))
    for b in range(bp):
        o_ref[ns + b * SUBLANES:ns + (b + 1) * SUBLANES, :] = jnp.broadcast_to(
            ada[bs + b:bs + b + 1], (SUBLANES, ada.shape[1]))


def _ada_call(c_all, w_ada, b_ada, bs, t_s, bp):
    rows, d = c_all.shape
    n6 = w_ada.shape[1]
    tn = _tile(n6, 512, LANES)
    out_rows = bs * t_s + bp * SUBLANES
    return pl.pallas_call(
        functools.partial(_ada_body, bs=bs, t_s=t_s, bp=bp),
        grid=(n6 // tn,),
        in_specs=[pl.BlockSpec((rows, d), lambda j: (0, 0)),
                  pl.BlockSpec((d, tn), lambda j: (0, j)),
                  pl.BlockSpec((1, tn), lambda j: (0, j))],
        out_specs=pl.BlockSpec((out_rows, tn), lambda j: (0, j)),
        out_shape=jax.ShapeDtypeStruct((out_rows, n6), F32),
        compiler_params=_params("arbitrary"),
        name="ada",
    )(c_all, w_ada, b_ada.reshape(1, n6))


def _rms(x, g):
    return x * lax.rsqrt(jnp.mean(x * x, axis=-1, keepdims=True) + EPS) * g


def _mod(ref, n_rows):
    return ref[...] if ref.shape[0] == n_rows else ref[0:1, :]


def _norm1_body(x_ref, g_ref, sc_ref, sh_ref, *rest):
    o_ref = rest[-1]
    x = x_ref[...]
    tt = x.shape[0]
    y = _rms(x, g_ref[...])
    o_ref[...] = (y * (1.0 + _mod(sc_ref, tt)) + _mod(sh_ref, tt)).astype(o_ref.dtype)


def _norm2_body(x_ref, a_ref, gt_ref, g_ref, sc_ref, sh_ref, wr_ref, br_ref, *rest):
    x1_ref, h2_ref, lg_ref = rest[-3:]
    tt = x_ref.shape[0]
    x1 = x_ref[...] + _mod(gt_ref, tt) * a_ref[...]
    x1_ref[...] = x1
    h2 = _rms(x1, g_ref[...]) * (1.0 + _mod(sc_ref, tt)) + _mod(sh_ref, tt)
    h2_ref[...] = h2
    lg_ref[...] = jnp.dot(h2, wr_ref[...], preferred_element_type=F32,
                          precision=lax.Precision.HIGHEST) + br_ref[...]


def _row_gather_start(idx_ref, base, n_rows, src_hbm, dst, sem):
    def body(i, carry):
        pltpu.make_async_copy(src_hbm.at[pl.ds(idx_ref[base + i], 1)], dst.at[pl.ds(i, 1)], sem).start()
        return carry

    lax.fori_loop(0, n_rows, body, 0, unroll=8)


def _row_gather_wait(n_rows, src_hbm, dst, sem):
    pltpu.make_async_copy(src_hbm.at[pl.ds(0, n_rows)], dst, sem).wait()


def _final_body(pos_ref, x1_ref, w_ref, gt_ref, g_ref, y_hbm, o_ref, buf_ref, sem_ref, *, row0):
    i = pl.program_id(0)
    tt = x1_ref.shape[0]

    def start(step, slot):
        for k in range(TOP_K):
            _row_gather_start(pos_ref, k * pos_ref.shape[0] // TOP_K + row0 + step * tt, tt, y_hbm,
                              buf_ref.at[slot, k], sem_ref.at[slot])

    @pl.when(i == 0)
    def _():
        start(0, 0)

    @pl.when(i + 1 < pl.num_programs(0))
    def _():
        start(i + 1, (i + 1) % 2)

    slot = i % 2
    for k in range(TOP_K):
        _row_gather_wait(tt, y_hbm, buf_ref.at[slot, k], sem_ref.at[slot])
    per_row_gate = gt_ref.shape[0] == tt

    def row_group(c, carry):
        rs = pl.ds(pl.multiple_of(c * SUBLANES, SUBLANES), SUBLANES)
        moe = jnp.zeros((SUBLANES, x1_ref.shape[1]), F32)
        for k in range(TOP_K):
            moe = moe + w_ref[rs, k:k + 1] * buf_ref[slot, k, rs, :]
        gate = gt_ref[rs, :] if per_row_gate else gt_ref[0:1, :]
        o_ref[rs, :] = _rms(x1_ref[rs, :] + gate * moe, g_ref[...])
        return carry

    lax.fori_loop(0, tt // SUBLANES, row_group, 0, unroll=2)


class _Rows:
    def __init__(self, np_, t_p, ns, t_s, d):
        self.np_, self.t_p, self.ns, self.t_s, self.d = np_, t_p, ns, t_s, d
        self.tt_p = _tile(t_p, 256, 16)
        self.tt_s = _tile(ns, 128, 16)
        assert np_ % self.tt_s == 0
        self.n = np_ + ns

    def grid(self, sample):
        return (self.ns // self.tt_s,) if sample else (self.np_ // self.tt_p,)

    def rows(self, sample, width, offset=True):
        if sample:
            base = self.np_ // self.tt_s if offset else 0
            return pl.BlockSpec((self.tt_s, width), lambda i: (base + i, 0))
        return pl.BlockSpec((self.tt_p, width), lambda i: (i, 0))

    def mod(self, sample, which):
        if sample:
            return pl.BlockSpec((self.tt_s, self.d), lambda i: (i, which))
        per = self.t_p // self.tt_p
        base = self.ns // SUBLANES
        return pl.BlockSpec((SUBLANES, self.d), lambda i: (base + i // per, which))

    def const(self, shape):
        return pl.BlockSpec(shape, lambda i: tuple(0 for _ in shape))


def _rowwise(body, rows, name, prompt_in, sample_in, specs, out_widths, out_dtypes):
    outs = None
    for sample, ops in ((False, prompt_in), (True, sample_in)):
        in_specs = [s(sample) for s in specs]
        ops = list(ops)
        aliases = {}
        if outs is not None:
            for k, o in enumerate(outs):
                aliases[len(ops)] = k
                ops.append(o)
                in_specs.append(pl.BlockSpec(memory_space=pl.ANY))
        outs = pl.pallas_call(
            body,
            grid=rows.grid(sample),
            in_specs=in_specs,
            out_specs=[rows.rows(sample, w) for w in out_widths],
            out_shape=[jax.ShapeDtypeStruct((rows.n, w), dt) for w, dt in zip(out_widths, out_dtypes)],
            input_output_aliases=aliases,
            compiler_params=_params("arbitrary"),
            name=name + ("_sample" if sample else "_prompt"),
        )(*ops)
    return outs


def _mm_body(x_ref, w_ref, o_ref, wb_ref):
    @pl.when(pl.program_id(1) == 0)
    def _():
        wb_ref[...] = w_ref[...].astype(BF16)
    o_ref[...] = jnp.dot(x_ref[...], wb_ref[...], preferred_element_type=F32).astype(o_ref.dtype)


def _mm_pw2_body(x_ref, w_ref, b_ref, ma_ref, gb_ref, o_ref, wb_ref):
    @pl.when(pl.program_id(1) == 0)
    def _():
        wb_ref[...] = w_ref[...].astype(BF16)
    yb = jnp.dot(x_ref[...], wb_ref[...], preferred_element_type=F32) + b_ref[...]
    o_ref[...] = (ma_ref[...].astype(F32) + _sigmoid(gb_ref[...]) * yb).astype(o_ref.dtype)


def _mm_call(x, w, out_dtype, tm_target, tn_target, name):
    m, k = x.shape
    n = w.shape[1]
    tm = _tile(m, tm_target, 16)
    tn = _tile(n, tn_target, LANES)
    return pl.pallas_call(
        _mm_body,
        grid=(n // tn, m // tm),
        in_specs=[pl.BlockSpec((tm, k), lambda j, i: (i, 0)),
                  pl.BlockSpec((k, tn), lambda j, i: (0, j))],
        out_specs=pl.BlockSpec((tm, tn), lambda j, i: (i, j)),
        out_shape=jax.ShapeDtypeStruct((m, n), out_dtype),
        scratch_shapes=[pltpu.VMEM((k, tn), BF16)],
        compiler_params=_params("arbitrary", "arbitrary"),
        name=name,
    )(x, w)


def _mm_pw2_call(x, w, bias, ma, proj, gate_b_col):
    m, k = x.shape
    n = w.shape[1]
    tm = _tile(m, 1088, 16)
    tn = _tile(n, 512, LANES)
    assert gate_b_col % tn == 0
    gb0 = gate_b_col // tn
    return pl.pallas_call(
        _mm_pw2_body,
        grid=(n // tn, m // tm),
        in_specs=[pl.BlockSpec((tm, k), lambda j, i: (i, 0)),
                  pl.BlockSpec((k, tn), lambda j, i: (0, j)),
                  pl.BlockSpec((1, tn), lambda j, i: (0, j)),
                  pl.BlockSpec((tm, tn), lambda j, i: (i, j)),
                  pl.BlockSpec((tm, tn), lambda j, i: (i, gb0 + j))],
        out_specs=pl.BlockSpec((tm, tn), lambda j, i: (i, j)),
        out_shape=jax.ShapeDtypeStruct((m, n), BF16),
        scratch_shapes=[pltpu.VMEM((k, tn), BF16)],
        compiler_params=_params("arbitrary", "arbitrary"),
        name="pw2",
    )(x, w, bias.reshape(1, n), ma, proj)


def _bf16_terms(x):
    hi = x.astype(BF16).astype(F32)
    r = x - hi
    mid = r.astype(BF16).astype(F32)
    return hi, mid, r - mid


def _gate_epilogue(o, gon, og, ga):
    ya = _rms(o, gon) * _silu(og)
    return (_sigmoid(ga) * ya).astype(BF16)


def _hgrn_prompt_body(q_ref, f_ref, i_ref, og_ref, ga_ref, lb_ref, gon_ref, ma_ref, so_ref,
                      st_ref, b_scr, k_scr, *, chunk, n_sub, hb):
    c_idx = pl.program_id(2)
    C = chunk

    @pl.when(c_idx == 0)
    def _():
        st_ref[...] = jnp.zeros_like(st_ref)

    row_i = lax.broadcasted_iota(jnp.int32, (C, C), 0)
    col_i = lax.broadcasted_iota(jnp.int32, (C, C), 1)
    causal = row_i >= col_i
    tri = causal.astype(BF16)

    lb = lb_ref[...]
    f = _forget_gate(f_ref[...], lb)
    log_f = jnp.log(f)
    k_scr[...] = 1.0 - f
    spread = jnp.zeros((1, hb * LANES), F32)
    for c in range(n_sub):
        rows = slice(c * C, (c + 1) * C)
        b = sum(jnp.dot(tri, term.astype(BF16), preferred_element_type=F32)
                for term in _bf16_terms(log_f[rows]))
        b_scr[rows, :] = b
        b_mid = b[C // 2 - 1:C // 2, :]
        spread = jnp.maximum(spread, jnp.maximum(b[0:1, :] - b_mid, b_mid - b[C - 1:C, :]))
    factored_ok = jnp.max(spread) < FACTORED_DECAY_MAX_SPREAD

    def pairwise_scores(q, c, sl):
        rows = slice(c * C, (c + 1) * C)
        b = b_scr[rows, sl]
        t_i = lax.broadcasted_iota(jnp.int32, (C, 1), 0)
        s_i = lax.broadcasted_iota(jnp.int32, (1, C), 1)

        def column_group(g, acc):
            grp = pl.ds(pl.multiple_of(c * C + g * SUBLANES, SUBLANES), SUBLANES)
            b_g = b_scr[grp, sl]
            k_g = k_scr[grp, sl]
            for r in range(SUBLANES):
                s = g * SUBLANES + r
                decay = jnp.exp(jnp.where(t_i >= s, b - b_g[r:r + 1], -jnp.inf))
                col = jnp.sum(q * k_g[r:r + 1] * decay, axis=-1, keepdims=True)
                acc = jnp.where(s_i == s, col, acc)
            return acc

        return lax.fori_loop(0, C // SUBLANES, column_group, jnp.zeros((C, C), F32))

    nt = (((1,), (1,)), ((), ()))

    def run(factored):
        for hh in range(hb):
            sl = slice(hh * LANES, (hh + 1) * LANES)
            st = st_ref[hh]
            for c in range(n_sub):
                rows = slice(c * C, (c + 1) * C)
                q = q_ref[rows, sl]
                b = b_scr[rows, sl]
                k = k_scr[rows, sl]
                b_last = b[C - 1:C, :]
                if factored:
                    b_mid = b[C // 2 - 1:C // 2, :]
                    q_dec = q * jnp.exp(b - b_mid)
                    k_dec = k * jnp.exp(b_mid - b)
                    scores = lax.dot_general(q_dec.astype(BF16), k_dec.astype(BF16), nt,
                                             preferred_element_type=F32)
                    scores = jnp.where(causal, scores, 0.0)
                    qe = q_dec * jnp.exp(b_mid)
                    kl = k_dec * jnp.exp(b_last - b_mid)
                else:
                    scores = pairwise_scores(q, c, sl)
                    qe = q * jnp.exp(b)
                    kl = k * jnp.exp(b_last - b)
                v = i_ref[rows, sl]
                intra = jnp.dot(scores.astype(BF16), v.astype(BF16), preferred_element_type=F32)
                inter = lax.dot_general(qe.astype(BF16), st.astype(BF16), nt, preferred_element_type=F32)
                st = st * jnp.exp(b_last) + jnp.dot(v.T.astype(BF16), kl.astype(BF16),
                                                    preferred_element_type=F32)
                ma_ref[rows, sl] = _gate_epilogue(inter + intra, gon_ref[:, sl],
                                                  og_ref[rows, sl], ga_ref[rows, sl])
            st_ref[hh] = st

    @pl.when(factored_ok)
    def _():
        run(True)

    @pl.when(jnp.logical_not(factored_ok))
    def _():
        run(False)

    @pl.when(c_idx == pl.num_programs(2) - 1)
    def _():
        for hh in range(hb):
            so_ref[hh] = st_ref[hh].T


def _hgrn_prompt_call(proj, lb, g_onorm, bsz, t, n_heads, d_conv, n_rows):
    d = n_heads * LANES
    chunk = _tile(t, 128, LANES)
    n_sub = 2 if t % (2 * chunk) == 0 else 1
    tc = chunk * n_sub
    hb = _tile(n_heads, 8, 1)
    w = hb * LANES
    hcols = n_heads // hb
    ga0 = (4 * d + 2 * d_conv) // w
    per_b = t // tc

    def col(base):
        return pl.BlockSpec((tc, w), lambda b, h, c: (b * per_b + c, base + h))

    body = functools.partial(_hgrn_prompt_body, chunk=chunk, n_sub=n_sub, hb=hb)
    return pl.pallas_call(
        body,
        grid=(bsz, hcols, per_b),
        in_specs=[col(0), col(hcols), col(2 * hcols), col(3 * hcols), col(ga0),
                  pl.BlockSpec((1, w), lambda b, h, c: (0, h)),
                  pl.BlockSpec((1, w), lambda b, h, c: (0, h))],
        out_specs=[pl.BlockSpec((tc, w), lambda b, h, c: (b * per_b + c, h)),
                   pl.BlockSpec((None, hb, LANES, LANES), lambda b, h, c: (b, h, 0, 0))],
        out_shape=[jax.ShapeDtypeStruct((n_rows, d), BF16),
                   jax.ShapeDtypeStruct((bsz, n_heads, LANES, LANES), F32)],
        scratch_shapes=[pltpu.VMEM((hb, LANES, LANES), F32)] + [pltpu.VMEM((tc, w), F32)] * 2,
        compiler_params=_params("arbitrary", "arbitrary", "arbitrary"),
        name="hgrn_prompt",
    )(proj, proj, proj, proj, proj, lb, g_onorm)


def _hgrn_sample_body(q_ref, f_ref, i_ref, og_ref, ga_ref, lb_ref, gon_ref, s_ref, ma_in_ref,
                      ma_ref, so_ref, *, t_s, n_heads):
    del ma_in_ref
    nb = SUBLANES // t_s
    row = lax.broadcasted_iota(jnp.int32, (SUBLANES, 1), 0)
    tok = row % t_s
    contract0 = (((0,), (0,)), ((), ()))

    def head(h, carry):
        sl = pl.ds(pl.multiple_of(h * LANES, LANES), LANES)
        q = q_ref[:, sl]
        v = i_ref[:, sl]
        lb = lb_ref[:, sl]
        f = _forget_gate(f_ref[:, sl], lb)
        log_f = jnp.log(f)
        k = 1.0 - f
        b = log_f
        for d in range(1, t_s):
            b = b + jnp.where(tok >= d, pltpu.roll(log_f, d, 0), 0.0)
        o = jnp.zeros((SUBLANES, LANES), F32)
        for d in range(t_s):
            kd, bd, vd = (k, b, v) if d == 0 else (pltpu.roll(k, d, 0), pltpu.roll(b, d, 0),
                                                  pltpu.roll(v, d, 0))
            decay = jnp.exp(jnp.where(tok >= d, b - bd, -jnp.inf))
            score = jnp.sum(q * kd * decay, axis=-1, keepdims=True)
            o = o + score * jnp.where(tok >= d, vd, 0.0)
        qe = (q * jnp.exp(b)).astype(BF16)
        for bi in range(nb):
            in_b = (row >= bi * t_s) & (row < (bi + 1) * t_s)
            s0 = s_ref[bi, h]
            inter = jnp.dot(qe, s0.astype(BF16), preferred_element_type=F32)
            o = o + jnp.where(in_b, inter, 0.0)
            b_last = b[(bi + 1) * t_s - 1:(bi + 1) * t_s, :]
            kk = jnp.where(in_b, k * jnp.exp(jnp.where(in_b, b_last - b, 0.0)), 0.0)
            vv = jnp.where(in_b, v, 0.0)
            upd = lax.dot_general(kk.astype(BF16), vv.astype(BF16), contract0,
                                  preferred_element_type=F32)
            hi, mid, lo = _bf16_terms(jnp.exp(b_last))
            d_rows = jnp.where(row == 0, hi, jnp.where(row == 1, mid, jnp.where(row == 2, lo, 0.0)))
            one_rows = jnp.where(row < 3, jnp.ones((SUBLANES, LANES), F32), 0.0)
            decay_kv = lax.dot_general(d_rows.astype(BF16), one_rows.astype(BF16), contract0,
                                       preferred_element_type=F32)
            so_ref[bi, h] = s0 * decay_kv + upd
        ma_ref[:, sl] = _gate_epilogue(o, gon_ref[:, sl], og_ref[:, sl], ga_ref[:, sl])
        return carry

    lax.fori_loop(0, n_heads, head, 0, unroll=_tile(n_heads, 8, 1))


def _hgrn_sample_call(proj, lb, g_onorm, state, ma, row0, t_s, d_conv):
    bs, n_heads = state.shape[0], state.shape[1]
    d = n_heads * LANES
    assert SUBLANES % t_s == 0 and row0 % SUBLANES == 0
    nb = SUBLANES // t_s
    r0 = row0 // SUBLANES
    ga0 = (4 * d + 2 * d_conv) // d

    def col(base):
        return pl.BlockSpec((SUBLANES, d), lambda i: (r0 + i, base))

    body = functools.partial(_hgrn_sample_body, t_s=t_s, n_heads=n_heads)
    return pl.pallas_call(
        body,
        grid=(bs // nb,),
        in_specs=[col(0), col(1), col(2), col(3), col(ga0),
                  pl.BlockSpec((1, d), lambda i: (0, 0)),
                  pl.BlockSpec((1, d), lambda i: (0, 0)),
                  pl.BlockSpec((nb, n_heads, LANES, LANES), lambda i: (i, 0, 0, 0)),
                  pl.BlockSpec(memory_space=pl.ANY)],
        out_specs=[pl.BlockSpec((SUBLANES, d), lambda i: (r0 + i, 0)),
                   pl.BlockSpec((nb, n_heads, LANES, LANES), lambda i: (i, 0, 0, 0))],
        out_shape=[jax.ShapeDtypeStruct(ma.shape, ma.dtype),
                   jax.ShapeDtypeStruct(state.shape, F32)],
        input_output_aliases={8: 0},
        compiler_params=_params("arbitrary"),
        name="hgrn_sample",
    )(proj, proj, proj, proj, proj, lb, g_onorm, state, ma)


def _ln_swish(x, g, b):
    mu = jnp.mean(x, axis=-1, keepdims=True)
    xc = x - mu
    var = jnp.mean(xc * xc, axis=-1, keepdims=True)
    return _silu(xc * lax.rsqrt(var + EPS) * g + b)


def _conv_prompt_body(ga_ref, gb_ref, w_ref, bdw_ref, lg_ref, lbias_ref, o_ref, cache_ref,
                      ext_ref, conv_ref, *, width, hist, rc):
    t_idx = pl.program_id(1)
    tt = ga_ref.shape[0]
    dc = ga_ref.shape[1]

    @pl.when(t_idx == 0)
    def _():
        ext_ref[0:hist, :] = jnp.zeros((hist, dc), F32)

    @pl.when(t_idx > 0)
    def _():
        ext_ref[0:hist, :] = ext_ref[tt:tt + hist, :]

    ext_ref[hist:hist + tt, :] = ga_ref[...] * _sigmoid(gb_ref[...])
    off = hist - (width - 1)

    def strip(l, carry):
        lanes = pl.ds(pl.multiple_of(l * LANES, LANES), LANES)
        for r0 in range(0, tt, rc):
            win = ext_ref[r0:r0 + rc + hist, lanes]
            acc = jnp.zeros((rc, LANES), F32) + bdw_ref[:, lanes]
            for s in range(SUBLANES):
                taps = [j for j in range(width) if (off + j) % SUBLANES == s]
                if not taps:
                    continue
                shifted = win if s == 0 else pltpu.roll(win, rc + hist - s, 0)
                for j in taps:
                    a = (off + j) // SUBLANES * SUBLANES
                    acc = acc + w_ref[j:j + 1, lanes] * shifted[a:a + rc]
            conv_ref[r0:r0 + rc, lanes] = acc
        return carry

    lax.fori_loop(0, dc // LANES, strip, 0)

    def norm(i, carry):
        rows = pl.ds(pl.multiple_of(i * rc, rc), rc)
        o_ref[rows, :] = _ln_swish(conv_ref[rows, :], lg_ref[...], lbias_ref[...]).astype(o_ref.dtype)
        return carry

    lax.fori_loop(0, tt // rc, norm, 0)

    @pl.when(t_idx == pl.num_programs(1) - 1)
    def _():
        cache_ref[...] = ext_ref[hist + tt - (width - 1):hist + tt, :]


def _conv_prompt_call(proj, w_dw, b_dw, ln_g, ln_b, bsz, t, d_hgrn, n_rows):
    width, dc = w_dw.shape
    tt = _tile(t, 256, 32)
    hist = 32
    assert width - 1 <= hist <= tt and (4 * d_hgrn) % dc == 0
    c0 = 4 * d_hgrn // dc
    per_b = t // tt
    vec = lambda a: a.reshape(1, dc)
    body = functools.partial(_conv_prompt_body, width=width, hist=hist, rc=32)
    return pl.pallas_call(
        body,
        grid=(bsz, per_b),
        in_specs=[pl.BlockSpec((tt, dc), lambda b, i: (b * per_b + i, c0)),
                  pl.BlockSpec((tt, dc), lambda b, i: (b * per_b + i, c0 + 1)),
                  pl.BlockSpec((width, dc), lambda b, i: (0, 0)),
                  pl.BlockSpec((1, dc), lambda b, i: (0, 0)),
                  pl.BlockSpec((1, dc), lambda b, i: (0, 0)),
                  pl.BlockSpec((1, dc), lambda b, i: (0, 0))],
        out_specs=[pl.BlockSpec((tt, dc), lambda b, i: (b * per_b + i, 0)),
                   pl.BlockSpec((None, width - 1, dc), lambda b, i: (b, 0, 0))],
        out_shape=[jax.ShapeDtypeStruct((n_rows, dc), BF16),
                   jax.ShapeDtypeStruct((bsz, width - 1, dc), F32)],
        scratch_shapes=[pltpu.VMEM((hist + tt, dc), F32), pltpu.VMEM((tt, dc), F32)],
        compiler_params=_params("arbitrary", "arbitrary"),
        name="conv_prompt",
    )(proj, proj, w_dw, vec(b_dw), vec(ln_g), vec(ln_b))


def _conv_sample_body(ga_ref, gb_ref, cache_ref, w_ref, bdw_ref, lg_ref, lbias_ref, vc_in_ref,
                      o_ref, ncache_ref, ext_ref, *, width, t_s, bb):
    del vc_in_ref
    u = ga_ref[...] * _sigmoid(gb_ref[...])
    for bi in range(bb):
        ext_ref[0:width - 1, :] = cache_ref[bi]
        ext_ref[width - 1:width - 1 + t_s, :] = u[bi * t_s:(bi + 1) * t_s]
        acc = jnp.zeros((t_s, u.shape[1]), F32) + bdw_ref[...]
        for j in range(width):
            acc = acc + w_ref[j:j + 1, :] * ext_ref[j:j + t_s, :]
        ncache_ref[bi] = ext_ref[t_s:t_s + width - 1, :]
        o_ref[bi * t_s:(bi + 1) * t_s, :] = _ln_swish(acc, lg_ref[...], lbias_ref[...]).astype(o_ref.dtype)


def _conv_sample_call(proj, cache, w_dw, b_dw, ln_g, ln_b, vconv, row0, t_s, d_hgrn):
    width, dc = w_dw.shape
    bs = cache.shape[0]
    bb = _tile(bs, 8, 1)
    rows = bb * t_s
    assert rows % 16 == 0 and row0 % rows == 0
    r0 = row0 // rows
    c0 = 4 * d_hgrn // dc
    vec = lambda a: a.reshape(1, dc)
    body = functools.partial(_conv_sample_body, width=width, t_s=t_s, bb=bb)
    return pl.pallas_call(
        body,
        grid=(bs // bb,),
        in_specs=[pl.BlockSpec((rows, dc), lambda i: (r0 + i, c0)),
                  pl.BlockSpec((rows, dc), lambda i: (r0 + i, c0 + 1)),
                  pl.BlockSpec((bb, width - 1, dc), lambda i: (i, 0, 0)),
                  pl.BlockSpec((width, dc), lambda i: (0, 0)),
                  pl.BlockSpec((1, dc), lambda i: (0, 0)),
                  pl.BlockSpec((1, dc), lambda i: (0, 0)),
                  pl.BlockSpec((1, dc), lambda i: (0, 0)),
                  pl.BlockSpec(memory_space=pl.ANY)],
        out_specs=[pl.BlockSpec((rows, dc), lambda i: (r0 + i, 0)),
                   pl.BlockSpec((bb, width - 1, dc), lambda i: (i, 0, 0))],
        out_shape=[jax.ShapeDtypeStruct(vconv.shape, vconv.dtype),
                   jax.ShapeDtypeStruct(cache.shape, F32)],
        scratch_shapes=[pltpu.VMEM((width - 1 + t_s + SUBLANES, dc), F32)],
        input_output_aliases={7: 0},
        compiler_params=_params("arbitrary"),
        name="conv_sample",
    )(proj, proj, cache, w_dw, vec(b_dw), vec(ln_g), vec(ln_b), vconv)


def _expert_changed(be_ref, r):
    return jnp.logical_or(r == 0, be_ref[r] != be_ref[jnp.maximum(r - 1, 0)])


def _weight_copies(w_hbms, buf_refs, sem_ref, e, j, slot):
    copies = []
    for w_hbm, buf_ref in zip(w_hbms, buf_refs):
        tcol = buf_ref.shape[-1]
        cols = pl.ds(pl.multiple_of(j * tcol, tcol), tcol)
        copies.append(pltpu.make_async_copy(w_hbm.at[e, :, cols], buf_ref.at[slot], sem_ref.at[slot]))
    return copies


def _stage_expert_weights(be_ref, nr_ref, nx_ref, w_hbms, buf_refs, bf16_refs, sem_ref, cnt_ref):
    j, r = pl.program_id(0), pl.program_id(1)

    @pl.when(jnp.logical_and(j == 0, r == 0))
    def _():
        cnt_ref[0] = 0
        for c in _weight_copies(w_hbms, buf_refs, sem_ref, be_ref[0], 0, 0):
            c.start()

    @pl.when(jnp.logical_and(r < nr_ref[0], _expert_changed(be_ref, r)))
    def _():
        slot = cnt_ref[0] % 2
        for c in _weight_copies(w_hbms, buf_refs, sem_ref, be_ref[r], j, slot):
            c.wait()
        wraps = nx_ref[r] >= nr_ref[0]
        e_next = be_ref[jnp.where(wraps, 0, nx_ref[r])]
        j_next = jnp.where(wraps, j + 1, j)

        @pl.when(j_next < pl.num_programs(0))
        def _():
            for c in _weight_copies(w_hbms, buf_refs, sem_ref, e_next, j_next, 1 - slot):
                c.start()

        for buf_ref, dst_ref in zip(buf_refs, bf16_refs):
            dst_ref[...] = buf_ref[slot].astype(BF16)
        cnt_ref[0] = cnt_ref[0] + 1


def _moe_up_body(be_ref, nr_ref, nx_ref, x_ref, w1_hbm, w3_hbm, g_ref,
                 w1f_ref, w3f_ref, w1b_ref, w3b_ref, sem_ref, cnt_ref):
    _stage_expert_weights(be_ref, nr_ref, nx_ref, (w1_hbm, w3_hbm), (w1f_ref, w3f_ref),
                          (w1b_ref, w3b_ref), sem_ref, cnt_ref)

    @pl.when(pl.program_id(1) < nr_ref[0])
    def _():
        x = x_ref[...]
        a = jnp.dot(x, w1b_ref[...], preferred_element_type=F32)
        b = jnp.dot(x, w3b_ref[...], preferred_element_type=F32)
        g_ref[...] = (_silu(a) * b).astype(g_ref.dtype)


def _moe_down_body(be_ref, nr_ref, nx_ref, g_ref, w2_hbm, y_ref, w2f_ref, w2b_ref, sem_ref, cnt_ref):
    _stage_expert_weights(be_ref, nr_ref, nx_ref, (w2_hbm,), (w2f_ref,), (w2b_ref,), sem_ref, cnt_ref)

    @pl.when(pl.program_id(1) < nr_ref[0])
    def _():
        y_ref[...] = jnp.dot(g_ref[...], w2b_ref[...], preferred_element_type=F32)


def _dispatch_body(tok_ref, nr_ref, cnt_ref, first_ref, h2_hbm, o_ref, buf_ref, sem_ref):
    r = pl.program_id(0)
    tm = o_ref.shape[0]

    def start(step, slot):
        def copy_row(i, carry):
            pltpu.make_async_copy(h2_hbm.at[pl.ds(tok_ref[first_ref[step] + i], 1)],
                                  buf_ref.at[slot, pl.ds(i, 1)], sem_ref.at[slot]).start()
            return carry

        @pl.when(cnt_ref[step] == tm)
        def _():
            lax.fori_loop(0, tm, copy_row, 0, unroll=8)

        @pl.when(cnt_ref[step] < tm)
        def _():
            lax.fori_loop(0, cnt_ref[step], copy_row, 0)

    @pl.when(r == 0)
    def _():
        buf_ref[...] = jnp.zeros_like(buf_ref)
        start(0, 0)

    @pl.when(r + 1 < nr_ref[0])
    def _():
        start(r + 1, (r + 1) % 2)

    @pl.when(r < nr_ref[0])
    def _():
        slot = r % 2
        rows = pl.ds(0, pl.multiple_of(cnt_ref[r], SUBLANES))
        pltpu.make_async_copy(h2_hbm.at[rows], buf_ref.at[slot, rows], sem_ref.at[slot]).wait()
        o_ref[...] = buf_ref[slot].astype(o_ref.dtype)


def _dispatch_call(h2, sorted_tok, n_real, block_cnt, block_first, tm):
    n, d = h2.shape
    nblk = block_cnt.shape[0]
    return pl.pallas_call(
        _dispatch_body,
        grid_spec=pltpu.PrefetchScalarGridSpec(
            num_scalar_prefetch=4,
            grid=(nblk,),
            in_specs=[pl.BlockSpec(memory_space=pl.ANY)],
            out_specs=pl.BlockSpec((tm, d), lambda r, o, nr, cnt, first: (jnp.minimum(r, nr[0] - 1), 0)),
            scratch_shapes=[pltpu.VMEM((2, tm, d), F32), pltpu.SemaphoreType.DMA((2,))]),
        out_shape=jax.ShapeDtypeStruct((nblk * tm, d), BF16),
        compiler_params=_params("arbitrary"),
        name="moe_dispatch",
    )(sorted_tok, n_real, block_cnt, block_first, h2)


def _moe_experts(x_sorted, block_e, n_real, next_run, w1, w3, w2, tm):
    l, d = x_sorted.shape
    n_exp, _, dh = w1.shape
    nblk = l // tm
    th = _tile(dh, 512, LANES)
    tn = _tile(d, 2048, LANES)
    rr = lambda r, nr: jnp.minimum(r, nr[0] - 1)
    hbm = pl.BlockSpec(memory_space=pl.ANY)

    g = pl.pallas_call(
        _moe_up_body,
        grid_spec=pltpu.PrefetchScalarGridSpec(
            num_scalar_prefetch=3,
            grid=(dh // th, nblk),
            in_specs=[pl.BlockSpec((tm, d), lambda j, r, be, nr, nx: (rr(r, nr), 0)), hbm, hbm],
            out_specs=pl.BlockSpec((tm, th), lambda j, r, be, nr, nx: (rr(r, nr), j)),
            scratch_shapes=[pltpu.VMEM((2, d, th), F32), pltpu.VMEM((2, d, th), F32),
                            pltpu.VMEM((d, th), BF16), pltpu.VMEM((d, th), BF16),
                            pltpu.SemaphoreType.DMA((2,)), pltpu.SMEM((1,), jnp.int32)]),
        out_shape=jax.ShapeDtypeStruct((l, dh), BF16),
        compiler_params=_params("arbitrary", "arbitrary"),
        name="moe_up",
    )(block_e, n_real, next_run, x_sorted, w1, w3)

    return pl.pallas_call(
        _moe_down_body,
        grid_spec=pltpu.PrefetchScalarGridSpec(
            num_scalar_prefetch=3,
            grid=(d // tn, nblk),
            in_specs=[pl.BlockSpec((tm, dh), lambda j, r, be, nr, nx: (rr(r, nr), 0)), hbm],
            out_specs=pl.BlockSpec((tm, tn), lambda j, r, be, nr, nx: (rr(r, nr), j)),
            scratch_shapes=[pltpu.VMEM((2, dh, tn), F32), pltpu.VMEM((dh, tn), BF16),
                            pltpu.SemaphoreType.DMA((2,)), pltpu.SMEM((1,), jnp.int32)]),
        out_shape=jax.ShapeDtypeStruct((l, d), F32),
        compiler_params=_params("arbitrary", "arbitrary"),
        name="moe_down",
    )(block_e, n_real, next_run, g, w2)


def _route(logits, n_groups, n_experts, tm):
    n = logits.shape[0]
    eg = n_experts // n_groups
    l1 = logits[:, :n_groups]
    p1 = jax.nn.softmax(l1, axis=-1)
    grp = jnp.argmax(l1, axis=-1)
    p_grp = jnp.max(p1, axis=-1, keepdims=True)
    l2 = logits[:, n_groups:n_groups + n_experts].reshape(n, n_groups, eg)
    l2g = jnp.take_along_axis(l2, grp[:, None, None], axis=1)[:, 0]
    top_v, top_i = lax.top_k(l2g, TOP_K)
    weights = p_grp * jax.nn.softmax(top_v, axis=-1)
    expert = (grp[:, None] * eg + top_i).astype(jnp.int32)

    a = n * TOP_K
    i32 = lambda v: v.astype(jnp.int32)
    e = expert.reshape(-1)
    counts = i32(jnp.bincount(e, length=n_experts))
    padded = (counts + tm - 1) // tm * tm
    ends = jnp.cumsum(padded)
    starts = ends - padded
    seg_start = jnp.cumsum(counts) - counts
    order = jnp.argsort(e)
    rank = jnp.argsort(order)
    pos = i32(starts[e] + rank - seg_start[e]).reshape(n, TOP_K)
    nblk = -(-a // tm) + n_experts
    block_e = i32(jnp.minimum(jnp.searchsorted(ends, jnp.arange(nblk) * tm, side='right'),
                              n_experts - 1))
    n_real = i32(ends[-1:] // tm)
    block_cnt = jnp.clip(counts[block_e] - (jnp.arange(nblk) * tm - starts[block_e]), 0, tm)
    block_cnt = i32(-(-block_cnt // SUBLANES) * SUBLANES)
    next_run = i32(ends[block_e] // tm)
    block_first = i32(jnp.minimum(seg_start[block_e] + jnp.arange(nblk) * tm - starts[block_e], a))
    sorted_tok = jnp.pad(i32(order // TOP_K), (0, SUBLANES))
    return weights, sorted_tok, pos, block_e, n_real, block_cnt, block_first, next_run


def kernel(x_prompt, x_sample, state_hgrn, cache_conv, c_prompt, c_sample, g_mix, w_ada, b_ada, w_in, lb_param, g_onorm, w_dw, b_dw, ln_g, ln_b, w_pw2, b_pw2, w_out, g_ffn, w_r1, b_r1, w_r2, b_r2, moe_w1, moe_w3, moe_w2, g_final):
    bp, t_p, d = x_prompt.shape
    bs, t_s, _ = x_sample.shape
    depth, _, n_heads, head_k, head_v = state_hgrn.shape
    assert depth == 1 and head_k == LANES and head_v == LANES
    d_hgrn = n_heads * head_k
    d_conv = w_dw.shape[-1]
    n_groups, n_experts = w_r1.shape[-1], w_r2.shape[-1]
    np_, ns = bp * t_p, bs * t_s
    n = np_ + ns
    rows = _Rows(np_, t_p, ns, t_s, d)
    vec = lambda a: a.reshape(1, -1)

    assert bs % SUBLANES == 0
    c_all = jnp.concatenate([c_sample, c_prompt], axis=0)
    c_all = jnp.pad(c_all, ((0, -c_all.shape[0] % 16), (0, 0)))
    ada = _ada_call(c_all, w_ada[0], b_ada[0], bs, t_s, bp)
    ada_p = ada_s = ada
    SH_M, SC_M, GT_M, SH_F, SC_F, GT_F = range(6)
    mod = lambda which: (lambda sample: rows.mod(sample, which))
    ada_of = lambda sample: ada
    xs = (x_prompt.reshape(np_, d), x_sample.reshape(ns, d))
    x_spec = lambda sample: rows.rows(sample, d, offset=False)
    flat = lambda width: (lambda sample: rows.rows(sample, width))
    const = lambda shape: (lambda sample: rows.const(shape))

    (h,) = _rowwise(
        _norm1_body, rows, "norm1",
        [xs[0], vec(g_mix[0]), ada_p, ada_p], [xs[1], vec(g_mix[0]), ada_s, ada_s],
        [x_spec, const((1, d)), mod(SC_M), mod(SH_M)], [d], [BF16])
    proj = _mm_call(h, w_in[0], F32, 1088, 512, "in_proj")

    lb = jax.nn.softmax(lb_param.astype(F32), axis=0)[0:1]
    ma, state_p = _hgrn_prompt_call(proj, lb, vec(g_onorm[0]), bp, t_p, n_heads, d_conv, n)
    ma, state_s = _hgrn_sample_call(proj, lb, vec(g_onorm[0]), state_hgrn[0], ma, np_, t_s, d_conv)

    vconv, cache_p = _conv_prompt_call(proj, w_dw[0], b_dw[0], ln_g[0], ln_b[0], bp, t_p, d_hgrn, n)
    vconv, cache_s = _conv_sample_call(proj, cache_conv[0], w_dw[0], b_dw[0], ln_g[0], ln_b[0],
                                       vconv, np_, t_s, d_hgrn)

    merged = _mm_pw2_call(vconv, w_pw2[0], b_pw2[0], ma, proj, 4 * d_hgrn + 2 * d_conv + d)
    attn = _mm_call(merged, w_out[0], F32, 1088, 512, "out_proj")

    n_r = n_groups + n_experts
    n_r_pad = -(-n_r // LANES) * LANES
    w_r = jnp.concatenate([w_r1[0], w_r2[0], jnp.zeros((d, n_r_pad - n_r), F32)], axis=1)
    b_r = jnp.concatenate([b_r1[0], b_r2[0], jnp.zeros((n_r_pad - n_r,), F32)]).reshape(1, n_r_pad)
    x1, h2, logits = _rowwise(
        _norm2_body, rows, "norm2",
        [xs[0], attn, ada_p, vec(g_ffn[0]), ada_p, ada_p, w_r, b_r],
        [xs[1], attn, ada_s, vec(g_ffn[0]), ada_s, ada_s, w_r, b_r],
        [x_spec, flat(d), mod(GT_M), const((1, d)), mod(SC_F), mod(SH_F),
         const((d, n_r_pad)), const((1, n_r_pad))],
        [d, d, n_r_pad], [F32, F32, F32])

    tm = 256
    weights, sorted_tok, pos, block_e, n_real, block_cnt, block_first, next_run = _route(
        logits, n_groups, n_experts, tm)
    x_sorted = _dispatch_call(h2, sorted_tok, n_real, block_cnt, block_first, tm)
    y_sorted = _moe_experts(x_sorted, block_e, n_real, next_run, moe_w1[0], moe_w3[0], moe_w2[0], tm)
    pos_flat = pos.T.reshape(-1)

    outs = []
    for sample in (False, True):
        tt = rows.tt_s if sample else rows.tt_p
        wrap = lambda spec: pl.BlockSpec(spec.block_shape, lambda i, p, f=spec.index_map: f(i))
        outs.append(pl.pallas_call(
            functools.partial(_final_body, row0=np_ if sample else 0),
            grid_spec=pltpu.PrefetchScalarGridSpec(
                num_scalar_prefetch=1,
                grid=rows.grid(sample),
                in_specs=[wrap(rows.rows(sample, d)), wrap(rows.rows(sample, TOP_K)),
                          wrap(rows.mod(sample, GT_F)), wrap(rows.const((1, d))),
                          pl.BlockSpec(memory_space=pl.ANY)],
                out_specs=wrap(rows.rows(sample, d, offset=False)),
                scratch_shapes=[pltpu.VMEM((2, TOP_K, tt, d), F32), pltpu.SemaphoreType.DMA((2,))]),
            out_shape=jax.ShapeDtypeStruct((ns if sample else np_, d), F32),
            compiler_params=_params("arbitrary"),
            name="final_sample" if sample else "final_prompt",
        )(pos_flat, x1, weights, ada_of(sample), vec(g_final), y_sorted))
    y_prompt = outs[0].reshape(bp, t_p, d)
    y_sample = outs[1].reshape(bs, t_s, d)
    return (y_prompt, y_sample, state_p[None], cache_p[None], state_s[None], cache_s[None])
```

```python
import functools

import jax
import jax.numpy as jnp
from jax import lax
from jax.experimental import pallas as pl
from jax.experimental.pallas import tpu as pltpu

F32 = jnp.float32
BF16 = jnp.bfloat16
EPS = 1e-6
TOP_K = 2
LANES = 128
SUBLANES = 8
VMEM_LIMIT = 56 * 1024 * 1024
FACTORED_DECAY_MAX_SPREAD = 80.0


def _params(*sem):
    return pltpu.CompilerParams(dimension_semantics=sem, vmem_limit_bytes=VMEM_LIMIT)


def _tile(n, target, mult):
    best = None
    for t in range(mult, min(n, target) + 1, mult):
        if n % t == 0:
            best = t
    assert best is not None, (n, target, mult)
    return best


def _sigmoid(x):
    return 0.5 * jnp.tanh(0.5 * x) + 0.5


def _forget_gate(f_logit, lb):
    return lb + (1.0 - lb) / (1.0 + jnp.exp(-f_logit))


def _silu(x):
    return x * _sigmoid(x)


def _ada_body(c_ref, w_ref, b_ref, o_ref, *, bs, t_s, bp):
    s = _silu(c_ref[...]).astype(BF16)
    ada = jnp.dot(s, w_ref[...].astype(BF16), preferred_element_type=F32) + b_ref[...]
    ns = bs * t_s
    r_i = lax.broadcasted_iota(jnp.int32, (ns, bs), 0)
    b_i = lax.broadcasted_iota(jnp.int32, (ns, bs), 1)
    lo = b_i * t_s
    expand = jnp.where((r_i >= lo) & (r_i < lo + t_s), 1.0, 0.0).astype(BF16)
    o_ref[0:ns, :] = sum(jnp.dot(expand, term.astype(BF16), preferred_element_type=F32)
                         for term in _bf16_terms(ada[0:bs]))
    for b in range(bp):
        o_ref[ns + b * SUBLANES:ns + (b + 1) * SUBLANES, :] = jnp.broadcast_to(
            ada[bs + b:bs + b + 1], (SUBLANES, ada.shape[1]))


def _ada_call(c_all, w_ada, b_ada, bs, t_s, bp):
    rows, d = c_all.shape
    n6 = w_ada.shape[1]
    tn = _tile(n6, 512, LANES)
    out_rows = bs * t_s + bp * SUBLANES
    return pl.pallas_call(
        functools.partial(_ada_body, bs=bs, t_s=t_s, bp=bp),
        grid=(n6 // tn,),
        in_specs=[pl.BlockSpec((rows, d), lambda j: (0, 0)),
                  pl.BlockSpec((d, tn), lambda j: (0, j)),
                  pl.BlockSpec((1, tn), lambda j: (0, j))],
        out_specs=pl.BlockSpec((out_rows, tn), lambda j: (0, j)),
        out_shape=jax.ShapeDtypeStruct((out_rows, n6), F32),
        compiler_params=_params("arbitrary"),
        name="ada",
    )(c_all, w_ada, b_ada.reshape(1, n6))


def _rms(x, g):
    return x * lax.rsqrt(jnp.mean(x * x, axis=-1, keepdims=True) + EPS) * g


def _mod(ref, n_rows):
    return ref[...] if ref.shape[0] == n_rows else ref[0:1, :]


def _norm1_body(x_ref, g_ref, sc_ref, sh_ref, *rest):
    o_ref = rest[-1]
    x = x_ref[...]
    tt = x.shape[0]
    y = _rms(x, g_ref[...])
    o_ref[...] = (y * (1.0 + _mod(sc_ref, tt)) + _mod(sh_ref, tt)).astype(o_ref.dtype)


def _pack_bf16_halves(x):
    w = x.shape[1] // 2
    lo = lax.bitcast_convert_type(x[:, :w].astype(BF16).astype(F32), jnp.uint32)
    hi = lax.bitcast_convert_type(x[:, w:].astype(BF16).astype(F32), jnp.uint32)
    return (lo >> 16) | hi


def _unpack_bf16_halves(u):
    lo = lax.bitcast_convert_type(u << 16, F32)
    hi = lax.bitcast_convert_type(u & jnp.uint32(0xFFFF0000), F32)
    return lo, hi


def _norm2_body(x_ref, a_ref, gt_ref, g_ref, sc_ref, sh_ref, wr_ref, br_ref, *rest):
    x1_ref, h2p_ref, lg_ref = rest[-3:]
    tt = x_ref.shape[0]
    x1 = x_ref[...] + _mod(gt_ref, tt) * a_ref[...]
    x1_ref[...] = x1
    h2 = _rms(x1, g_ref[...]) * (1.0 + _mod(sc_ref, tt)) + _mod(sh_ref, tt)
    h2p_ref[...] = _pack_bf16_halves(h2)
    lg_ref[...] = jnp.dot(h2, wr_ref[...], preferred_element_type=F32,
                          precision=lax.Precision.HIGHEST) + br_ref[...]


def _row_gather_start(idx_ref, base, n_rows, src_hbm, dst, sem):
    def body(i, carry):
        pltpu.make_async_copy(src_hbm.at[pl.ds(idx_ref[base + i], 1)], dst.at[pl.ds(i, 1)], sem).start()
        return carry

    lax.fori_loop(0, n_rows, body, 0, unroll=8)


def _row_gather_wait(n_rows, src_hbm, dst, sem):
    pltpu.make_async_copy(src_hbm.at[pl.ds(0, n_rows)], dst, sem).wait()


def _final_body(pos_ref, x1_ref, w_ref, gt_ref, g_ref, y_hbm, o_ref, buf_ref, sem_ref, *, row0):
    i = pl.program_id(0)
    tt = x1_ref.shape[0]

    def start(step, slot):
        for k in range(TOP_K):
            _row_gather_start(pos_ref, k * pos_ref.shape[0] // TOP_K + row0 + step * tt, tt, y_hbm,
                              buf_ref.at[slot, k], sem_ref.at[slot])

    @pl.when(i == 0)
    def _():
        start(0, 0)

    @pl.when(i + 1 < pl.num_programs(0))
    def _():
        start(i + 1, (i + 1) % 2)

    slot = i % 2
    for k in range(TOP_K):
        _row_gather_wait(tt, y_hbm, buf_ref.at[slot, k], sem_ref.at[slot])
    per_row_gate = gt_ref.shape[0] == tt

    half = x1_ref.shape[1] // 2

    def row_group(c, carry):
        rs = pl.ds(pl.multiple_of(c * SUBLANES, SUBLANES), SUBLANES)
        moe_lo = jnp.zeros((SUBLANES, half), F32)
        moe_hi = jnp.zeros((SUBLANES, half), F32)
        for k in range(TOP_K):
            y_lo, y_hi = _unpack_bf16_halves(buf_ref[slot, k, rs, :])
            moe_lo = moe_lo + w_ref[rs, k:k + 1] * y_lo
            moe_hi = moe_hi + w_ref[rs, k:k + 1] * y_hi
        gate = gt_ref[rs, :] if per_row_gate else gt_ref[0:1, :]
        x2_lo = x1_ref[rs, 0:half] + gate[:, 0:half] * moe_lo
        x2_hi = x1_ref[rs, half:] + gate[:, half:] * moe_hi
        ms = (jnp.sum(x2_lo * x2_lo, axis=-1, keepdims=True)
              + jnp.sum(x2_hi * x2_hi, axis=-1, keepdims=True)) / x1_ref.shape[1]
        scale = lax.rsqrt(ms + EPS)
        o_ref[rs, 0:half] = x2_lo * scale * g_ref[:, 0:half]
        o_ref[rs, half:] = x2_hi * scale * g_ref[:, half:]
        return carry

    lax.fori_loop(0, tt // SUBLANES, row_group, 0, unroll=2)


class _Rows:
    def __init__(self, np_, t_p, ns, t_s, d):
        self.np_, self.t_p, self.ns, self.t_s, self.d = np_, t_p, ns, t_s, d
        self.tt_p = _tile(t_p, 256, 16)
        self.tt_s = _tile(ns, 128, 16)
        assert np_ % self.tt_s == 0
        self.n = np_ + ns

    def grid(self, sample):
        return (self.ns // self.tt_s,) if sample else (self.np_ // self.tt_p,)

    def rows(self, sample, width, offset=True):
        if sample:
            base = self.np_ // self.tt_s if offset else 0
            return pl.BlockSpec((self.tt_s, width), lambda i: (base + i, 0))
        return pl.BlockSpec((self.tt_p, width), lambda i: (i, 0))

    def mod(self, sample, which):
        if sample:
            return pl.BlockSpec((self.tt_s, self.d), lambda i: (i, which))
        per = self.t_p // self.tt_p
        base = self.ns // SUBLANES
        return pl.BlockSpec((SUBLANES, self.d), lambda i: (base + i // per, which))

    def const(self, shape):
        return pl.BlockSpec(shape, lambda i: tuple(0 for _ in shape))


def _rowwise(body, rows, name, prompt_in, sample_in, specs, out_widths, out_dtypes):
    outs = None
    for sample, ops in ((False, prompt_in), (True, sample_in)):
        in_specs = [s(sample) for s in specs]
        ops = list(ops)
        aliases = {}
        if outs is not None:
            for k, o in enumerate(outs):
                aliases[len(ops)] = k
                ops.append(o)
                in_specs.append(pl.BlockSpec(memory_space=pl.ANY))
        outs = pl.pallas_call(
            body,
            grid=rows.grid(sample),
            in_specs=in_specs,
            out_specs=[rows.rows(sample, w) for w in out_widths],
            out_shape=[jax.ShapeDtypeStruct((rows.n, w), dt) for w, dt in zip(out_widths, out_dtypes)],
            input_output_aliases=aliases,
            compiler_params=_params("arbitrary"),
            name=name + ("_sample" if sample else "_prompt"),
        )(*ops)
    return outs


def _mm_body(x_ref, w_ref, o_ref, wb_ref):
    @pl.when(pl.program_id(1) == 0)
    def _():
        wb_ref[...] = w_ref[...].astype(BF16)
    o_ref[...] = jnp.dot(x_ref[...], wb_ref[...], preferred_element_type=F32).astype(o_ref.dtype)


def _mm_pw2_body(x_ref, w_ref, b_ref, ma_ref, gb_ref, o_ref, wb_ref):
    @pl.when(pl.program_id(1) == 0)
    def _():
        wb_ref[...] = w_ref[...].astype(BF16)
    yb = jnp.dot(x_ref[...], wb_ref[...], preferred_element_type=F32) + b_ref[...]
    o_ref[...] = (ma_ref[...].astype(F32) + _sigmoid(gb_ref[...]) * yb).astype(o_ref.dtype)


def _mm_call(x, w, out_dtype, tm_target, tn_target, name):
    m, k = x.shape
    n = w.shape[1]
    tm = _tile(m, tm_target, 16)
    tn = _tile(n, tn_target, LANES)
    return pl.pallas_call(
        _mm_body,
        grid=(n // tn, m // tm),
        in_specs=[pl.BlockSpec((tm, k), lambda j, i: (i, 0)),
                  pl.BlockSpec((k, tn), lambda j, i: (0, j))],
        out_specs=pl.BlockSpec((tm, tn), lambda j, i: (i, j)),
        out_shape=jax.ShapeDtypeStruct((m, n), out_dtype),
        scratch_shapes=[pltpu.VMEM((k, tn), BF16)],
        compiler_params=_params("arbitrary", "arbitrary"),
        name=name,
    )(x, w)


def _mm_pw2_call(x, w, bias, ma, proj, gate_b_col):
    m, k = x.shape
    n = w.shape[1]
    tm = _tile(m, 1088, 16)
    tn = _tile(n, 512, LANES)
    assert gate_b_col % tn == 0
    gb0 = gate_b_col // tn
    return pl.pallas_call(
        _mm_pw2_body,
        grid=(n // tn, m // tm),
        in_specs=[pl.BlockSpec((tm, k), lambda j, i: (i, 0)),
                  pl.BlockSpec((k, tn), lambda j, i: (0, j)),
                  pl.BlockSpec((1, tn), lambda j, i: (0, j)),
                  pl.BlockSpec((tm, tn), lambda j, i: (i, j)),
                  pl.BlockSpec((tm, tn), lambda j, i: (i, gb0 + j))],
        out_specs=pl.BlockSpec((tm, tn), lambda j, i: (i, j)),
        out_shape=jax.ShapeDtypeStruct((m, n), BF16),
        scratch_shapes=[pltpu.VMEM((k, tn), BF16)],
        compiler_params=_params("arbitrary", "arbitrary"),
        name="pw2",
    )(x, w, bias.reshape(1, n), ma, proj)


def _bf16_terms(x):
    hi = x.astype(BF16).astype(F32)
    r = x - hi
    mid = r.astype(BF16).astype(F32)
    return hi, mid, r - mid


def _gate_epilogue(o, gon, og, ga):
    ya = _rms(o, gon) * _silu(og)
    return (_sigmoid(ga) * ya).astype(BF16)


def _hgrn_prompt_body(q_ref, f_ref, i_ref, og_ref, ga_ref, lb_ref, gon_ref, ma_ref, so_ref,
                      st_ref, b_scr, k_scr, *, chunk, n_sub, hb):
    c_idx = pl.program_id(2)
    C = chunk

    @pl.when(c_idx == 0)
    def _():
        st_ref[...] = jnp.zeros_like(st_ref)

    row_i = lax.broadcasted_iota(jnp.int32, (C, C), 0)
    col_i = lax.broadcasted_iota(jnp.int32, (C, C), 1)
    causal = row_i >= col_i
    tri = causal.astype(BF16)

    lb = lb_ref[...]
    f = _forget_gate(f_ref[...], lb)
    log_f = jnp.log(f)
    k_scr[...] = 1.0 - f
    spread = jnp.zeros((1, hb * LANES), F32)
    for c in range(n_sub):
        rows = slice(c * C, (c + 1) * C)
        b = sum(jnp.dot(tri, term.astype(BF16), preferred_element_type=F32)
                for term in _bf16_terms(log_f[rows]))
        b_scr[rows, :] = b
        b_mid = b[C // 2 - 1:C // 2, :]
        spread = jnp.maximum(spread, jnp.maximum(b[0:1, :] - b_mid, b_mid - b[C - 1:C, :]))
    factored_ok = jnp.max(spread) < FACTORED_DECAY_MAX_SPREAD

    def pairwise_scores(q, c, sl):
        rows = slice(c * C, (c + 1) * C)
        b = b_scr[rows, sl]
        t_i = lax.broadcasted_iota(jnp.int32, (C, 1), 0)
        s_i = lax.broadcasted_iota(jnp.int32, (1, C), 1)

        def column_group(g, acc):
            grp = pl.ds(pl.multiple_of(c * C + g * SUBLANES, SUBLANES), SUBLANES)
            b_g = b_scr[grp, sl]
            k_g = k_scr[grp, sl]
            for r in range(SUBLANES):
                s = g * SUBLANES + r
                decay = jnp.exp(jnp.where(t_i >= s, b - b_g[r:r + 1], -jnp.inf))
                col = jnp.sum(q * k_g[r:r + 1] * decay, axis=-1, keepdims=True)
                acc = jnp.where(s_i == s, col, acc)
            return acc

        return lax.fori_loop(0, C // SUBLANES, column_group, jnp.zeros((C, C), F32))

    nt = (((1,), (1,)), ((), ()))

    def run(factored):
        for hh in range(hb):
            sl = slice(hh * LANES, (hh + 1) * LANES)
            st = st_ref[hh]
            for c in range(n_sub):
                rows = slice(c * C, (c + 1) * C)
                q = q_ref[rows, sl]
                b = b_scr[rows, sl]
                k = k_scr[rows, sl]
                b_last = b[C - 1:C, :]
                if factored:
                    b_mid = b[C // 2 - 1:C // 2, :]
                    q_dec = q * jnp.exp(b - b_mid)
                    k_dec = k * jnp.exp(b_mid - b)
                    scores = lax.dot_general(q_dec.astype(BF16), k_dec.astype(BF16), nt,
                                             preferred_element_type=F32)
                    scores = jnp.where(causal, scores, 0.0)
                    qe = q_dec * jnp.exp(b_mid)
                    kl = k_dec * jnp.exp(b_last - b_mid)
                else:
                    scores = pairwise_scores(q, c, sl)
                    qe = q * jnp.exp(b)
                    kl = k * jnp.exp(b_last - b)
                v = i_ref[rows, sl]
                intra = jnp.dot(scores.astype(BF16), v.astype(BF16), preferred_element_type=F32)
                inter = lax.dot_general(qe.astype(BF16), st.astype(BF16), nt, preferred_element_type=F32)
                st = st * jnp.exp(b_last) + jnp.dot(v.T.astype(BF16), kl.astype(BF16),
                                                    preferred_element_type=F32)
                ma_ref[rows, sl] = _gate_epilogue(inter + intra, gon_ref[:, sl],
                                                  og_ref[rows, sl], ga_ref[rows, sl])
            st_ref[hh] = st

    @pl.when(factored_ok)
    def _():
        run(True)

    @pl.when(jnp.logical_not(factored_ok))
    def _():
        run(False)

    @pl.when(c_idx == pl.num_programs(2) - 1)
    def _():
        for hh in range(hb):
            so_ref[hh] = st_ref[hh].T


def _hgrn_prompt_call(proj, lb, g_onorm, bsz, t, n_heads, d_conv, n_rows):
    d = n_heads * LANES
    chunk = _tile(t, 128, LANES)
    n_sub = 2 if t % (2 * chunk) == 0 else 1
    tc = chunk * n_sub
    hb = _tile(n_heads, 16, 1)
    w = hb * LANES
    hcols = n_heads // hb
    ga0 = (4 * d + 2 * d_conv) // w
    per_b = t // tc

    def col(base):
        return pl.BlockSpec((tc, w), lambda b, h, c: (b * per_b + c, base + h))

    body = functools.partial(_hgrn_prompt_body, chunk=chunk, n_sub=n_sub, hb=hb)
    return pl.pallas_call(
        body,
        grid=(bsz, hcols, per_b),
        in_specs=[col(0), col(hcols), col(2 * hcols), col(3 * hcols), col(ga0),
                  pl.BlockSpec((1, w), lambda b, h, c: (0, h)),
                  pl.BlockSpec((1, w), lambda b, h, c: (0, h))],
        out_specs=[pl.BlockSpec((tc, w), lambda b, h, c: (b * per_b + c, h)),
                   pl.BlockSpec((None, hb, LANES, LANES), lambda b, h, c: (b, h, 0, 0))],
        out_shape=[jax.ShapeDtypeStruct((n_rows, d), BF16),
                   jax.ShapeDtypeStruct((bsz, n_heads, LANES, LANES), F32)],
        scratch_shapes=[pltpu.VMEM((hb, LANES, LANES), F32)] + [pltpu.VMEM((tc, w), F32)] * 2,
        compiler_params=_params("arbitrary", "arbitrary", "arbitrary"),
        name="hgrn_prompt",
    )(proj, proj, proj, proj, proj, lb, g_onorm)


def _hgrn_sample_body(q_ref, f_ref, i_ref, og_ref, ga_ref, lb_ref, gon_ref, s_ref, ma_in_ref,
                      ma_ref, so_ref, *, t_s, n_heads):
    del ma_in_ref
    nb = SUBLANES // t_s
    row = lax.broadcasted_iota(jnp.int32, (SUBLANES, 1), 0)
    tok = row % t_s
    contract0 = (((0,), (0,)), ((), ()))

    def head(h, carry):
        sl = pl.ds(pl.multiple_of(h * LANES, LANES), LANES)
        q = q_ref[:, sl]
        v = i_ref[:, sl]
        lb = lb_ref[:, sl]
        f = _forget_gate(f_ref[:, sl], lb)
        log_f = jnp.log(f)
        k = 1.0 - f
        b = log_f
        for d in range(1, t_s):
            b = b + jnp.where(tok >= d, pltpu.roll(log_f, d, 0), 0.0)
        o = jnp.zeros((SUBLANES, LANES), F32)
        for d in range(t_s):
            kd, bd, vd = (k, b, v) if d == 0 else (pltpu.roll(k, d, 0), pltpu.roll(b, d, 0),
                                                  pltpu.roll(v, d, 0))
            decay = jnp.exp(jnp.where(tok >= d, b - bd, -jnp.inf))
            score = jnp.sum(q * kd * decay, axis=-1, keepdims=True)
            o = o + score * jnp.where(tok >= d, vd, 0.0)
        qe = (q * jnp.exp(b)).astype(BF16)
        for bi in range(nb):
            in_b = (row >= bi * t_s) & (row < (bi + 1) * t_s)
            s0 = s_ref[bi, h]
            inter = jnp.dot(qe, s0.astype(BF16), preferred_element_type=F32)
            o = o + jnp.where(in_b, inter, 0.0)
            b_last = b[(bi + 1) * t_s - 1:(bi + 1) * t_s, :]
            kk = jnp.where(in_b, k * jnp.exp(jnp.where(in_b, b_last - b, 0.0)), 0.0)
            vv = jnp.where(in_b, v, 0.0)
            upd = lax.dot_general(kk.astype(BF16), vv.astype(BF16), contract0,
                                  preferred_element_type=F32)
            hi, mid, lo = _bf16_terms(jnp.exp(b_last))
            d_rows = jnp.where(row == 0, hi, jnp.where(row == 1, mid, jnp.where(row == 2, lo, 0.0)))
            one_rows = jnp.where(row < 3, jnp.ones((SUBLANES, LANES), F32), 0.0)
            decay_kv = lax.dot_general(d_rows.astype(BF16), one_rows.astype(BF16), contract0,
                                       preferred_element_type=F32)
            so_ref[bi, h] = s0 * decay_kv + upd
        ma_ref[:, sl] = _gate_epilogue(o, gon_ref[:, sl], og_ref[:, sl], ga_ref[:, sl])
        return carry

    lax.fori_loop(0, n_heads, head, 0, unroll=_tile(n_heads, 8, 1))


def _hgrn_sample_call(proj, lb, g_onorm, state, ma, row0, t_s, d_conv):
    bs, n_heads = state.shape[0], state.shape[1]
    d = n_heads * LANES
    assert SUBLANES % t_s == 0 and row0 % SUBLANES == 0
    nb = SUBLANES // t_s
    r0 = row0 // SUBLANES
    ga0 = (4 * d + 2 * d_conv) // d

    def col(base):
        return pl.BlockSpec((SUBLANES, d), lambda i: (r0 + i, base))

    body = functools.partial(_hgrn_sample_body, t_s=t_s, n_heads=n_heads)
    return pl.pallas_call(
        body,
        grid=(bs // nb,),
        in_specs=[col(0), col(1), col(2), col(3), col(ga0),
                  pl.BlockSpec((1, d), lambda i: (0, 0)),
                  pl.BlockSpec((1, d), lambda i: (0, 0)),
                  pl.BlockSpec((nb, n_heads, LANES, LANES), lambda i: (i, 0, 0, 0)),
                  pl.BlockSpec(memory_space=pl.ANY)],
        out_specs=[pl.BlockSpec((SUBLANES, d), lambda i: (r0 + i, 0)),
                   pl.BlockSpec((nb, n_heads, LANES, LANES), lambda i: (i, 0, 0, 0))],
        out_shape=[jax.ShapeDtypeStruct(ma.shape, ma.dtype),
                   jax.ShapeDtypeStruct(state.shape, F32)],
        input_output_aliases={8: 0},
        compiler_params=_params("arbitrary"),
        name="hgrn_sample",
    )(proj, proj, proj, proj, proj, lb, g_onorm, state, ma)


def _ln_swish(x, g, b):
    mu = jnp.mean(x, axis=-1, keepdims=True)
    xc = x - mu
    var = jnp.mean(xc * xc, axis=-1, keepdims=True)
    return _silu(xc * lax.rsqrt(var + EPS) * g + b)


def _conv_prompt_body(ga_ref, gb_ref, w_ref, bdw_ref, lg_ref, lbias_ref, o_ref, cache_ref,
                      ext_ref, conv_ref, *, width, hist, rc):
    t_idx = pl.program_id(1)
    tt = ga_ref.shape[0]
    dc = ga_ref.shape[1]

    @pl.when(t_idx == 0)
    def _():
        ext_ref[0:hist, :] = jnp.zeros((hist, dc), F32)

    @pl.when(t_idx > 0)
    def _():
        ext_ref[0:hist, :] = ext_ref[tt:tt + hist, :]

    ext_ref[hist:hist + tt, :] = ga_ref[...] * _sigmoid(gb_ref[...])
    off = hist - (width - 1)

    def strip(l, carry):
        lanes = pl.ds(pl.multiple_of(l * LANES, LANES), LANES)
        for r0 in range(0, tt, rc):
            win = ext_ref[r0:r0 + rc + hist, lanes]
            acc = jnp.zeros((rc, LANES), F32) + bdw_ref[:, lanes]
            for s in range(SUBLANES):
                taps = [j for j in range(width) if (off + j) % SUBLANES == s]
                if not taps:
                    continue
                shifted = win if s == 0 else pltpu.roll(win, rc + hist - s, 0)
                for j in taps:
                    a = (off + j) // SUBLANES * SUBLANES
                    acc = acc + w_ref[j:j + 1, lanes] * shifted[a:a + rc]
            conv_ref[r0:r0 + rc, lanes] = acc
        return carry

    lax.fori_loop(0, dc // LANES, strip, 0)

    def norm(i, carry):
        rows = pl.ds(pl.multiple_of(i * rc, rc), rc)
        o_ref[rows, :] = _ln_swish(conv_ref[rows, :], lg_ref[...], lbias_ref[...]).astype(o_ref.dtype)
        return carry

    lax.fori_loop(0, tt // rc, norm, 0)

    @pl.when(t_idx == pl.num_programs(1) - 1)
    def _():
        cache_ref[...] = ext_ref[hist + tt - (width - 1):hist + tt, :]


def _conv_prompt_call(proj, w_dw, b_dw, ln_g, ln_b, bsz, t, d_hgrn, n_rows):
    width, dc = w_dw.shape
    tt = _tile(t, 256, 32)
    hist = 32
    assert width - 1 <= hist <= tt and (4 * d_hgrn) % dc == 0
    c0 = 4 * d_hgrn // dc
    per_b = t // tt
    vec = lambda a: a.reshape(1, dc)
    body = functools.partial(_conv_prompt_body, width=width, hist=hist, rc=32)
    return pl.pallas_call(
        body,
        grid=(bsz, per_b),
        in_specs=[pl.BlockSpec((tt, dc), lambda b, i: (b * per_b + i, c0)),
                  pl.BlockSpec((tt, dc), lambda b, i: (b * per_b + i, c0 + 1)),
                  pl.BlockSpec((width, dc), lambda b, i: (0, 0)),
                  pl.BlockSpec((1, dc), lambda b, i: (0, 0)),
                  pl.BlockSpec((1, dc), lambda b, i: (0, 0)),
                  pl.BlockSpec((1, dc), lambda b, i: (0, 0))],
        out_specs=[pl.BlockSpec((tt, dc), lambda b, i: (b * per_b + i, 0)),
                   pl.BlockSpec((None, width - 1, dc), lambda b, i: (b, 0, 0))],
        out_shape=[jax.ShapeDtypeStruct((n_rows, dc), BF16),
                   jax.ShapeDtypeStruct((bsz, width - 1, dc), F32)],
        scratch_shapes=[pltpu.VMEM((hist + tt, dc), F32), pltpu.VMEM((tt, dc), F32)],
        compiler_params=_params("arbitrary", "arbitrary"),
        name="conv_prompt",
    )(proj, proj, w_dw, vec(b_dw), vec(ln_g), vec(ln_b))


def _conv_sample_body(ga_ref, gb_ref, cache_ref, w_ref, bdw_ref, lg_ref, lbias_ref, vc_in_ref,
                      o_ref, ncache_ref, ext_ref, *, width, t_s, bb):
    del vc_in_ref
    u = ga_ref[...] * _sigmoid(gb_ref[...])
    for bi in range(bb):
        ext_ref[0:width - 1, :] = cache_ref[bi]
        ext_ref[width - 1:width - 1 + t_s, :] = u[bi * t_s:(bi + 1) * t_s]
        acc = jnp.zeros((t_s, u.shape[1]), F32) + bdw_ref[...]
        for j in range(width):
            acc = acc + w_ref[j:j + 1, :] * ext_ref[j:j + t_s, :]
        ncache_ref[bi] = ext_ref[t_s:t_s + width - 1, :]
        o_ref[bi * t_s:(bi + 1) * t_s, :] = _ln_swish(acc, lg_ref[...], lbias_ref[...]).astype(o_ref.dtype)


def _conv_sample_call(proj, cache, w_dw, b_dw, ln_g, ln_b, vconv, row0, t_s, d_hgrn):
    width, dc = w_dw.shape
    bs = cache.shape[0]
    bb = _tile(bs, 8, 1)
    rows = bb * t_s
    assert rows % 16 == 0 and row0 % rows == 0
    r0 = row0 // rows
    c0 = 4 * d_hgrn // dc
    vec = lambda a: a.reshape(1, dc)
    body = functools.partial(_conv_sample_body, width=width, t_s=t_s, bb=bb)
    return pl.pallas_call(
        body,
        grid=(bs // bb,),
        in_specs=[pl.BlockSpec((rows, dc), lambda i: (r0 + i, c0)),
                  pl.BlockSpec((rows, dc), lambda i: (r0 + i, c0 + 1)),
                  pl.BlockSpec((bb, width - 1, dc), lambda i: (i, 0, 0)),
                  pl.BlockSpec((width, dc), lambda i: (0, 0)),
                  pl.BlockSpec((1, dc), lambda i: (0, 0)),
                  pl.BlockSpec((1, dc), lambda i: (0, 0)),
                  pl.BlockSpec((1, dc), lambda i: (0, 0)),
                  pl.BlockSpec(memory_space=pl.ANY)],
        out_specs=[pl.BlockSpec((rows, dc), lambda i: (r0 + i, 0)),
                   pl.BlockSpec((bb, width - 1, dc), lambda i: (i, 0, 0))],
        out_shape=[jax.ShapeDtypeStruct(vconv.shape, vconv.dtype),
                   jax.ShapeDtypeStruct(cache.shape, F32)],
        scratch_shapes=[pltpu.VMEM((width - 1 + t_s + SUBLANES, dc), F32)],
        input_output_aliases={7: 0},
        compiler_params=_params("arbitrary"),
        name="conv_sample",
    )(proj, proj, cache, w_dw, vec(b_dw), vec(ln_g), vec(ln_b), vconv)


def _expert_changed(be_ref, r):
    return jnp.logical_or(r == 0, be_ref[r] != be_ref[jnp.maximum(r - 1, 0)])


def _weight_copies(w_hbms, buf_refs, sem_ref, e, j, slot):
    copies = []
    for w_hbm, buf_ref in zip(w_hbms, buf_refs):
        tcol = buf_ref.shape[-1]
        cols = pl.ds(pl.multiple_of(j * tcol, tcol), tcol)
        copies.append(pltpu.make_async_copy(w_hbm.at[e, :, cols], buf_ref.at[slot], sem_ref.at[slot]))
    return copies


def _stage_expert_weights(be_ref, nr_ref, nx_ref, w_hbms, buf_refs, bf16_refs, sem_ref, cnt_ref):
    j, r = pl.program_id(0), pl.program_id(1)

    @pl.when(jnp.logical_and(j == 0, r == 0))
    def _():
        cnt_ref[0] = 0
        for c in _weight_copies(w_hbms, buf_refs, sem_ref, be_ref[0], 0, 0):
            c.start()

    @pl.when(jnp.logical_and(r < nr_ref[0], _expert_changed(be_ref, r)))
    def _():
        slot = cnt_ref[0] % 2
        for c in _weight_copies(w_hbms, buf_refs, sem_ref, be_ref[r], j, slot):
            c.wait()
        wraps = nx_ref[r] >= nr_ref[0]
        e_next = be_ref[jnp.where(wraps, 0, nx_ref[r])]
        j_next = jnp.where(wraps, j + 1, j)

        @pl.when(j_next < pl.num_programs(0))
        def _():
            for c in _weight_copies(w_hbms, buf_refs, sem_ref, e_next, j_next, 1 - slot):
                c.start()

        for buf_ref, dst_ref in zip(buf_refs, bf16_refs):
            dst_ref[...] = buf_ref[slot].astype(BF16)
        cnt_ref[0] = cnt_ref[0] + 1


def _moe_up_body(be_ref, nr_ref, nx_ref, x_ref, w1_hbm, w3_hbm, g_ref,
                 w1f_ref, w3f_ref, w1b_ref, w3b_ref, sem_ref, cnt_ref):
    _stage_expert_weights(be_ref, nr_ref, nx_ref, (w1_hbm, w3_hbm), (w1f_ref, w3f_ref),
                          (w1b_ref, w3b_ref), sem_ref, cnt_ref)

    @pl.when(pl.program_id(1) < nr_ref[0])
    def _():
        x = x_ref[...]
        a = jnp.dot(x, w1b_ref[...], preferred_element_type=F32)
        b = jnp.dot(x, w3b_ref[...], preferred_element_type=F32)
        g_ref[...] = (_silu(a) * b).astype(g_ref.dtype)


def _moe_down_body(be_ref, nr_ref, nx_ref, g_ref, w2_hbm, y_ref, w2f_ref, w2b_ref, sem_ref, cnt_ref):
    _stage_expert_weights(be_ref, nr_ref, nx_ref, (w2_hbm,), (w2f_ref,), (w2b_ref,), sem_ref, cnt_ref)

    @pl.when(pl.program_id(1) < nr_ref[0])
    def _():
        y_ref[...] = _pack_bf16_halves(jnp.dot(g_ref[...], w2b_ref[...], preferred_element_type=F32))


def _dispatch_body(tok_ref, nr_ref, cnt_ref, first_ref, h2_hbm, o_ref, buf_ref, sem_ref):
    r = pl.program_id(0)
    tm = o_ref.shape[0]

    def start(step, slot):
        def copy_row(i, carry):
            pltpu.make_async_copy(h2_hbm.at[pl.ds(tok_ref[first_ref[step] + i], 1)],
                                  buf_ref.at[slot, pl.ds(i, 1)], sem_ref.at[slot]).start()
            return carry

        @pl.when(cnt_ref[step] == tm)
        def _():
            lax.fori_loop(0, tm, copy_row, 0, unroll=8)

        @pl.when(cnt_ref[step] < tm)
        def _():
            lax.fori_loop(0, cnt_ref[step], copy_row, 0)

    @pl.when(r == 0)
    def _():
        buf_ref[...] = jnp.zeros_like(buf_ref)
        start(0, 0)

    @pl.when(r + 1 < nr_ref[0])
    def _():
        start(r + 1, (r + 1) % 2)

    @pl.when(r < nr_ref[0])
    def _():
        slot = r % 2
        rows = pl.ds(0, pl.multiple_of(cnt_ref[r], SUBLANES))
        pltpu.make_async_copy(h2_hbm.at[rows], buf_ref.at[slot, rows], sem_ref.at[slot]).wait()
        lo, hi = _unpack_bf16_halves(buf_ref[slot])
        half = lo.shape[1]
        o_ref[:, 0:half] = lo.astype(o_ref.dtype)
        o_ref[:, half:] = hi.astype(o_ref.dtype)


def _dispatch_call(h2p, sorted_tok, n_real, block_cnt, block_first, tm):
    half = h2p.shape[1]
    d = 2 * half
    nblk = block_cnt.shape[0]
    return pl.pallas_call(
        _dispatch_body,
        grid_spec=pltpu.PrefetchScalarGridSpec(
            num_scalar_prefetch=4,
            grid=(nblk,),
            in_specs=[pl.BlockSpec(memory_space=pl.ANY)],
            out_specs=pl.BlockSpec((tm, d), lambda r, o, nr, cnt, first: (jnp.minimum(r, nr[0] - 1), 0)),
            scratch_shapes=[pltpu.VMEM((2, tm, half), jnp.uint32), pltpu.SemaphoreType.DMA((2,))]),
        out_shape=jax.ShapeDtypeStruct((nblk * tm, d), BF16),
        compiler_params=_params("arbitrary"),
        name="moe_dispatch",
    )(sorted_tok, n_real, block_cnt, block_first, h2p)


def _moe_experts(x_sorted, block_e, n_real, next_run, w1, w3, w2, tm):
    l, d = x_sorted.shape
    n_exp, _, dh = w1.shape
    nblk = l // tm
    th = _tile(dh, 512, LANES)
    rr = lambda r, nr: jnp.minimum(r, nr[0] - 1)
    hbm = pl.BlockSpec(memory_space=pl.ANY)

    g = pl.pallas_call(
        _moe_up_body,
        grid_spec=pltpu.PrefetchScalarGridSpec(
            num_scalar_prefetch=3,
            grid=(dh // th, nblk),
            in_specs=[pl.BlockSpec((tm, d), lambda j, r, be, nr, nx: (rr(r, nr), 0)), hbm, hbm],
            out_specs=pl.BlockSpec((tm, th), lambda j, r, be, nr, nx: (rr(r, nr), j)),
            scratch_shapes=[pltpu.VMEM((2, d, th), F32), pltpu.VMEM((2, d, th), F32),
                            pltpu.VMEM((d, th), BF16), pltpu.VMEM((d, th), BF16),
                            pltpu.SemaphoreType.DMA((2,)), pltpu.SMEM((1,), jnp.int32)]),
        out_shape=jax.ShapeDtypeStruct((l, dh), BF16),
        compiler_params=_params("arbitrary", "arbitrary"),
        name="moe_up",
    )(block_e, n_real, next_run, x_sorted, w1, w3)

    return pl.pallas_call(
        _moe_down_body,
        grid_spec=pltpu.PrefetchScalarGridSpec(
            num_scalar_prefetch=3,
            grid=(1, nblk),
            in_specs=[pl.BlockSpec((tm, dh), lambda j, r, be, nr, nx: (rr(r, nr), 0)), hbm],
            out_specs=pl.BlockSpec((tm, d // 2), lambda j, r, be, nr, nx: (rr(r, nr), 0)),
            scratch_shapes=[pltpu.VMEM((2, dh, d), F32), pltpu.VMEM((dh, d), BF16),
                            pltpu.SemaphoreType.DMA((2,)), pltpu.SMEM((1,), jnp.int32)]),
        out_shape=jax.ShapeDtypeStruct((l, d // 2), jnp.uint32),
        compiler_params=_params("arbitrary", "arbitrary"),
        name="moe_down",
    )(block_e, n_real, next_run, g, w2)


def _route(logits, n_groups, n_experts, tm):
    n = logits.shape[0]
    eg = n_experts // n_groups
    l1 = logits[:, :n_groups]
    p1 = jax.nn.softmax(l1, axis=-1)
    grp = jnp.argmax(l1, axis=-1)
    p_grp = jnp.max(p1, axis=-1, keepdims=True)
    l2 = logits[:, n_groups:n_groups + n_experts].reshape(n, n_groups, eg)
    l2g = jnp.take_along_axis(l2, grp[:, None, None], axis=1)[:, 0]
    top_v, top_i = lax.top_k(l2g, TOP_K)
    weights = p_grp * jax.nn.softmax(top_v, axis=-1)
    expert = (grp[:, None] * eg + top_i).astype(jnp.int32)

    a = n * TOP_K
    i32 = lambda v: v.astype(jnp.int32)
    e = expert.reshape(-1)
    counts = i32(jnp.bincount(e, length=n_experts))
    padded = (counts + tm - 1) // tm * tm
    ends = jnp.cumsum(padded)
    starts = ends - padded
    seg_start = jnp.cumsum(counts) - counts
    order = jnp.argsort(e)
    rank = jnp.argsort(order)
    pos = i32(starts[e] + rank - seg_start[e]).reshape(n, TOP_K)
    nblk = -(-a // tm) + n_experts
    block_e = i32(jnp.minimum(jnp.searchsorted(ends, jnp.arange(nblk) * tm, side='right'),
                              n_experts - 1))
    n_real = i32(ends[-1:] // tm)
    block_cnt = jnp.clip(counts[block_e] - (jnp.arange(nblk) * tm - starts[block_e]), 0, tm)
    block_cnt = i32(-(-block_cnt // SUBLANES) * SUBLANES)
    next_run = i32(ends[block_e] // tm)
    block_first = i32(jnp.minimum(seg_start[block_e] + jnp.arange(nblk) * tm - starts[block_e], a))
    sorted_tok = jnp.pad(i32(order // TOP_K), (0, SUBLANES))
    return weights, sorted_tok, pos, block_e, n_real, block_cnt, block_first, next_run


def kernel(x_prompt, x_sample, state_hgrn, cache_conv, c_prompt, c_sample, g_mix, w_ada, b_ada, w_in, lb_param, g_onorm, w_dw, b_dw, ln_g, ln_b, w_pw2, b_pw2, w_out, g_ffn, w_r1, b_r1, w_r2, b_r2, moe_w1, moe_w3, moe_w2, g_final):
    bp, t_p, d = x_prompt.shape
    bs, t_s, _ = x_sample.shape
    depth, _, n_heads, head_k, head_v = state_hgrn.shape
    assert depth == 1 and head_k == LANES and head_v == LANES
    d_hgrn = n_heads * head_k
    d_conv = w_dw.shape[-1]
    n_groups, n_experts = w_r1.shape[-1], w_r2.shape[-1]
    np_, ns = bp * t_p, bs * t_s
    n = np_ + ns
    rows = _Rows(np_, t_p, ns, t_s, d)
    vec = lambda a: a.reshape(1, -1)

    assert bs % SUBLANES == 0
    c_all = jnp.concatenate([c_sample, c_prompt], axis=0)
    c_all = jnp.pad(c_all, ((0, -c_all.shape[0] % 16), (0, 0)))
    ada = _ada_call(c_all, w_ada[0], b_ada[0], bs, t_s, bp)
    ada_p = ada_s = ada
    SH_M, SC_M, GT_M, SH_F, SC_F, GT_F = range(6)
    mod = lambda which: (lambda sample: rows.mod(sample, which))
    ada_of = lambda sample: ada
    xs = (x_prompt.reshape(np_, d), x_sample.reshape(ns, d))
    x_spec = lambda sample: rows.rows(sample, d, offset=False)
    flat = lambda width: (lambda sample: rows.rows(sample, width))
    const = lambda shape: (lambda sample: rows.const(shape))

    (h,) = _rowwise(
        _norm1_body, rows, "norm1",
        [xs[0], vec(g_mix[0]), ada_p, ada_p], [xs[1], vec(g_mix[0]), ada_s, ada_s],
        [x_spec, const((1, d)), mod(SC_M), mod(SH_M)], [d], [BF16])
    proj = _mm_call(h, w_in[0], F32, 1088, 512, "in_proj")

    lb = jax.nn.softmax(lb_param.astype(F32), axis=0)[0:1]
    ma, state_p = _hgrn_prompt_call(proj, lb, vec(g_onorm[0]), bp, t_p, n_heads, d_conv, n)
    ma, state_s = _hgrn_sample_call(proj, lb, vec(g_onorm[0]), state_hgrn[0], ma, np_, t_s, d_conv)

    vconv, cache_p = _conv_prompt_call(proj, w_dw[0], b_dw[0], ln_g[0], ln_b[0], bp, t_p, d_hgrn, n)
    vconv, cache_s = _conv_sample_call(proj, cache_conv[0], w_dw[0], b_dw[0], ln_g[0], ln_b[0],
                                       vconv, np_, t_s, d_hgrn)

    merged = _mm_pw2_call(vconv, w_pw2[0], b_pw2[0], ma, proj, 4 * d_hgrn + 2 * d_conv + d)
    attn = _mm_call(merged, w_out[0], F32, 1088, 512, "out_proj")

    n_r = n_groups + n_experts
    n_r_pad = -(-n_r // LANES) * LANES
    w_r = jnp.concatenate([w_r1[0], w_r2[0], jnp.zeros((d, n_r_pad - n_r), F32)], axis=1)
    b_r = jnp.concatenate([b_r1[0], b_r2[0], jnp.zeros((n_r_pad - n_r,), F32)]).reshape(1, n_r_pad)
    x1, h2p, logits = _rowwise(
        _norm2_body, rows, "norm2",
        [xs[0], attn, ada_p, vec(g_ffn[0]), ada_p, ada_p, w_r, b_r],
        [xs[1], attn, ada_s, vec(g_ffn[0]), ada_s, ada_s, w_r, b_r],
        [x_spec, flat(d), mod(GT_M), const((1, d)), mod(SC_F), mod(SH_F),
         const((d, n_r_pad)), const((1, n_r_pad))],
        [d, d // 2, n_r_pad], [F32, jnp.uint32, F32])

    tm = 256
    weights, sorted_tok, pos, block_e, n_real, block_cnt, block_first, next_run = _route(
        logits, n_groups, n_experts, tm)
    x_sorted = _dispatch_call(h2p, sorted_tok, n_real, block_cnt, block_first, tm)
    y_sorted = _moe_experts(x_sorted, block_e, n_real, next_run, moe_w1[0], moe_w3[0], moe_w2[0], tm)
    pos_flat = pos.T.reshape(-1)

    outs = []
    for sample in (False, True):
        tt = rows.tt_s if sample else rows.tt_p
        wrap = lambda spec: pl.BlockSpec(spec.block_shape, lambda i, p, f=spec.index_map: f(i))
        outs.append(pl.pallas_call(
            functools.partial(_final_body, row0=np_ if sample else 0),
            grid_spec=pltpu.PrefetchScalarGridSpec(
                num_scalar_prefetch=1,
                grid=rows.grid(sample),
                in_specs=[wrap(rows.rows(sample, d)), wrap(rows.rows(sample, TOP_K)),
                          wrap(rows.mod(sample, GT_F)), wrap(rows.const((1, d))),
                          pl.BlockSpec(memory_space=pl.ANY)],
                out_specs=wrap(rows.rows(sample, d, offset=False)),
                scratch_shapes=[pltpu.VMEM((2, TOP_K, tt, d // 2), jnp.uint32),
                                pltpu.SemaphoreType.DMA((2,))]),
            out_shape=jax.ShapeDtypeStruct((ns if sample else np_, d), F32),
            compiler_params=_params("arbitrary"),
            name="final_sample" if sample else "final_prompt",
        )(pos_flat, x1, weights, ada_of(sample), vec(g_final), y_sorted))
    y_prompt = outs[0].reshape(bp, t_p, d)
    y_sample = outs[1].reshape(bs, t_s, d)
    return (y_prompt, y_sample, state_p[None], cache_p[None], state_s[None], cache_s[None])
```

```python
import functools

import jax
import jax.numpy as jnp
from jax import lax
from jax.experimental import pallas as pl
from jax.experimental.pallas import tpu as pltpu

F32 = jnp.float32
BF16 = jnp.bfloat16
EPS = 1e-6
TOP_K = 2
LANES = 128
SUBLANES = 8
ROW_GROUP = 16
VMEM_LIMIT = 60 * 1024 * 1024
FACTORED_DECAY_MAX_SPREAD = 80.0


def _params(*sem):
    return pltpu.CompilerParams(dimension_semantics=sem, vmem_limit_bytes=VMEM_LIMIT)


def _tile(n, target, mult):
    best = None
    for t in range(mult, min(n, target) + 1, mult):
        if n % t == 0:
            best = t
    assert best is not None, (n, target, mult)
    return best


def _sigmoid(x):
    return 0.5 * jnp.tanh(0.5 * x) + 0.5


def _forget_gate(f_logit, lb):
    return lb + (1.0 - lb) / (1.0 + jnp.exp(-f_logit))


def _silu(x):
    return x * _sigmoid(x)


def _ada_body(c_ref, w_ref, b_ref, o_ref, *, bs, t_s, bp):
    s = _silu(c_ref[...]).astype(BF16)
    ada = jnp.dot(s, w_ref[...].astype(BF16), preferred_element_type=F32) + b_ref[...]
    ns = bs * t_s
    r_i = lax.broadcasted_iota(jnp.int32, (ns, bs), 0)
    b_i = lax.broadcasted_iota(jnp.int32, (ns, bs), 1)
    lo = b_i * t_s
    expand = jnp.where((r_i >= lo) & (r_i < lo + t_s), 1.0, 0.0).astype(BF16)
    o_ref[0:ns, :] = sum(jnp.dot(expand, term.astype(BF16), preferred_element_type=F32)
                         for term in _bf16_terms(ada[0:bs]))
    for b in range(bp):
        o_ref[ns + b * SUBLANES:ns + (b + 1) * SUBLANES, :] = jnp.broadcast_to(
            ada[bs + b:bs + b + 1], (SUBLANES, ada.shape[1]))


def _ada_call(c_all, w_ada, b_ada, bs, t_s, bp):
    rows, d = c_all.shape
    n6 = w_ada.shape[1]
    tn = _tile(n6, 512, LANES)
    out_rows = bs * t_s + bp * SUBLANES
    return pl.pallas_call(
        functools.partial(_ada_body, bs=bs, t_s=t_s, bp=bp),
        grid=(n6 // tn,),
        in_specs=[pl.BlockSpec((rows, d), lambda j: (0, 0)),
                  pl.BlockSpec((d, tn), lambda j: (0, j)),
                  pl.BlockSpec((1, tn), lambda j: (0, j))],
        out_specs=pl.BlockSpec((out_rows, tn), lambda j: (0, j)),
        out_shape=jax.ShapeDtypeStruct((out_rows, n6), F32),
        compiler_params=_params("arbitrary"),
        name="ada",
    )(c_all, w_ada, b_ada.reshape(1, n6))


def _rms(x, g):
    return x * lax.rsqrt(jnp.mean(x * x, axis=-1, keepdims=True) + EPS) * g


def _mod(ref, n_rows):
    return ref[...] if ref.shape[0] == n_rows else ref[0:1, :]


def _norm1_body(x_ref, g_ref, sc_ref, sh_ref, *rest):
    o_ref = rest[-1]
    x = x_ref[...]
    tt = x.shape[0]
    y = _rms(x, g_ref[...])
    o_ref[...] = (y * (1.0 + _mod(sc_ref, tt)) + _mod(sh_ref, tt)).astype(o_ref.dtype)


def _pack_bf16_halves(x):
    w = x.shape[1] // 2
    lo = lax.bitcast_convert_type(x[:, :w].astype(BF16).astype(F32), jnp.uint32)
    hi = lax.bitcast_convert_type(x[:, w:].astype(BF16).astype(F32), jnp.uint32)
    return (lo >> 16) | hi


def _unpack_bf16_halves(u):
    lo = lax.bitcast_convert_type(u << 16, F32)
    hi = lax.bitcast_convert_type(u & jnp.uint32(0xFFFF0000), F32)
    return lo, hi


def _norm2_body(x_ref, a_ref, gt_ref, g_ref, sc_ref, sh_ref, wr_ref, br_ref, *rest):
    x1_ref, h2p_ref, lg_ref = rest[-3:]
    tt = x_ref.shape[0]
    x1 = x_ref[...] + _mod(gt_ref, tt) * a_ref[...]
    x1_ref[...] = x1
    h2 = _rms(x1, g_ref[...]) * (1.0 + _mod(sc_ref, tt)) + _mod(sh_ref, tt)
    h2p_ref[...] = _pack_bf16_halves(h2)
    lg_ref[...] = jnp.dot(h2, wr_ref[...], preferred_element_type=F32,
                          precision=lax.Precision.HIGHEST) + br_ref[...]


def _row_gather_wait(n_rows, src_hbm, dst, sem):
    pltpu.make_async_copy(src_hbm.at[pl.ds(0, n_rows)], dst, sem).wait()


def _final_body(pos_ref, x1_ref, w_ref, gt_ref, g_ref, y_hbm, o_ref, buf_ref, sem_ref, *, row0):
    i = pl.program_id(0)
    tt = x1_ref.shape[0]
    n_tok = pos_ref.shape[0] // TOP_K

    def start_rows(step, slot, first, count):
        for k in range(TOP_K):
            base = k * n_tok + row0 + step * tt
            for r in range(count):
                pltpu.make_async_copy(y_hbm.at[pl.ds(pos_ref[base + first + r], 1)],
                                      buf_ref.at[slot, k, pl.ds(first + r, 1)], sem_ref.at[slot]).start()

    @pl.when(i == 0)
    def _():
        def prime(c, carry):
            start_rows(0, 0, c * SUBLANES, SUBLANES)
            return carry

        lax.fori_loop(0, tt // SUBLANES, prime, 0)

    slot = i % 2
    for k in range(TOP_K):
        _row_gather_wait(tt, y_hbm, buf_ref.at[slot, k], sem_ref.at[slot])
    per_row_gate = gt_ref.shape[0] == tt

    half = x1_ref.shape[1] // 2

    def row_group(c, carry, prefetch):
        if prefetch:
            start_rows(i + 1, 1 - slot, c * SUBLANES, SUBLANES)
        rs = pl.ds(pl.multiple_of(c * SUBLANES, SUBLANES), SUBLANES)
        moe_lo = jnp.zeros((SUBLANES, half), F32)
        moe_hi = jnp.zeros((SUBLANES, half), F32)
        for k in range(TOP_K):
            y_lo, y_hi = _unpack_bf16_halves(buf_ref[slot, k, rs, :])
            moe_lo = moe_lo + w_ref[rs, k:k + 1] * y_lo
            moe_hi = moe_hi + w_ref[rs, k:k + 1] * y_hi
        gate = gt_ref[rs, :] if per_row_gate else gt_ref[0:1, :]
        x2_lo = x1_ref[rs, 0:half] + gate[:, 0:half] * moe_lo
        x2_hi = x1_ref[rs, half:] + gate[:, half:] * moe_hi
        ms = (jnp.sum(x2_lo * x2_lo, axis=-1, keepdims=True)
              + jnp.sum(x2_hi * x2_hi, axis=-1, keepdims=True)) / x1_ref.shape[1]
        scale = lax.rsqrt(ms + EPS)
        o_ref[rs, 0:half] = x2_lo * scale * g_ref[:, 0:half]
        o_ref[rs, half:] = x2_hi * scale * g_ref[:, half:]
        return carry

    has_next = i + 1 < pl.num_programs(0)
    unroll = _tile(tt // SUBLANES, 4, 1)

    @pl.when(has_next)
    def _():
        lax.fori_loop(0, tt // SUBLANES, functools.partial(row_group, prefetch=True), 0, unroll=unroll)

    @pl.when(jnp.logical_not(has_next))
    def _():
        lax.fori_loop(0, tt // SUBLANES, functools.partial(row_group, prefetch=False), 0, unroll=unroll)


class _Rows:
    def __init__(self, np_, t_p, ns, t_s, d):
        self.np_, self.t_p, self.ns, self.t_s, self.d = np_, t_p, ns, t_s, d
        self.tt_p = _tile(t_p, 256, 16)
        self.tt_s = _tile(ns, 128, 16)
        assert np_ % self.tt_s == 0
        self.n = np_ + ns

    def grid(self, sample):
        return (self.ns // self.tt_s,) if sample else (self.np_ // self.tt_p,)

    def rows(self, sample, width, offset=True):
        if sample:
            base = self.np_ // self.tt_s if offset else 0
            return pl.BlockSpec((self.tt_s, width), lambda i: (base + i, 0))
        return pl.BlockSpec((self.tt_p, width), lambda i: (i, 0))

    def mod(self, sample, which):
        if sample:
            return pl.BlockSpec((self.tt_s, self.d), lambda i: (i, which))
        per = self.t_p // self.tt_p
        base = self.ns // SUBLANES
        return pl.BlockSpec((SUBLANES, self.d), lambda i: (base + i // per, which))

    def const(self, shape):
        return pl.BlockSpec(shape, lambda i: tuple(0 for _ in shape))


def _rowwise(body, rows, name, prompt_in, sample_in, specs, out_widths, out_dtypes):
    outs = None
    for sample, ops in ((False, prompt_in), (True, sample_in)):
        in_specs = [s(sample) for s in specs]
        ops = list(ops)
        aliases = {}
        if outs is not None:
            for k, o in enumerate(outs):
                aliases[len(ops)] = k
                ops.append(o)
                in_specs.append(pl.BlockSpec(memory_space=pl.ANY))
        outs = pl.pallas_call(
            body,
            grid=rows.grid(sample),
            in_specs=in_specs,
            out_specs=[rows.rows(sample, w) for w in out_widths],
            out_shape=[jax.ShapeDtypeStruct((rows.n, w), dt) for w, dt in zip(out_widths, out_dtypes)],
            input_output_aliases=aliases,
            compiler_params=_params("arbitrary"),
            name=name + ("_sample" if sample else "_prompt"),
        )(*ops)
    return outs


def _mm_body(x_ref, w_ref, o_ref, wb_ref):
    @pl.when(pl.program_id(1) == 0)
    def _():
        wb_ref[...] = w_ref[...].astype(BF16)
    o_ref[...] = jnp.dot(x_ref[...], wb_ref[...], preferred_element_type=F32).astype(o_ref.dtype)


def _mm_pw2_body(x_ref, w_ref, b_ref, ma_ref, gb_ref, o_ref, wb_ref):
    @pl.when(pl.program_id(1) == 0)
    def _():
        wb_ref[...] = w_ref[...].astype(BF16)
    yb = jnp.dot(x_ref[...], wb_ref[...], preferred_element_type=F32) + b_ref[...]
    o_ref[...] = (ma_ref[...].astype(F32) + _sigmoid(gb_ref[...]) * yb).astype(o_ref.dtype)


def _mm_call(x, w, out_dtype, tm_target, tn_target, name):
    m, k = x.shape
    n = w.shape[1]
    tm = _tile(m, tm_target, 16)
    tn = _tile(n, tn_target, LANES)
    return pl.pallas_call(
        _mm_body,
        grid=(n // tn, m // tm),
        in_specs=[pl.BlockSpec((tm, k), lambda j, i: (i, 0)),
                  pl.BlockSpec((k, tn), lambda j, i: (0, j))],
        out_specs=pl.BlockSpec((tm, tn), lambda j, i: (i, j)),
        out_shape=jax.ShapeDtypeStruct((m, n), out_dtype),
        scratch_shapes=[pltpu.VMEM((k, tn), BF16)],
        compiler_params=_params("arbitrary", "arbitrary"),
        name=name,
    )(x, w)


def _mm_pw2_call(x, w, bias, ma, proj, gate_b_col):
    m, k = x.shape
    n = w.shape[1]
    tm = _tile(m, 1088, 16)
    tn = _tile(n, 512, LANES)
    assert gate_b_col % tn == 0
    gb0 = gate_b_col // tn
    return pl.pallas_call(
        _mm_pw2_body,
        grid=(n // tn, m // tm),
        in_specs=[pl.BlockSpec((tm, k), lambda j, i: (i, 0)),
                  pl.BlockSpec((k, tn), lambda j, i: (0, j)),
                  pl.BlockSpec((1, tn), lambda j, i: (0, j)),
                  pl.BlockSpec((tm, tn), lambda j, i: (i, j)),
                  pl.BlockSpec((tm, tn), lambda j, i: (i, gb0 + j))],
        out_specs=pl.BlockSpec((tm, tn), lambda j, i: (i, j)),
        out_shape=jax.ShapeDtypeStruct((m, n), BF16),
        scratch_shapes=[pltpu.VMEM((k, tn), BF16)],
        compiler_params=_params("arbitrary", "arbitrary"),
        name="pw2",
    )(x, w, bias.reshape(1, n), ma, proj)


def _bf16_terms(x):
    hi = x.astype(BF16).astype(F32)
    r = x - hi
    mid = r.astype(BF16).astype(F32)
    return hi, mid, r - mid


def _gate_epilogue(o, gon, og, ga):
    ya = _rms(o, gon) * _silu(og)
    return (_sigmoid(ga) * ya).astype(BF16)


def _hgrn_prompt_body(q_ref, f_ref, i_ref, og_ref, ga_ref, lb_ref, gon_ref, ma_ref, so_ref,
                      st_ref, b_scr, k_scr, *, chunk, n_sub, hb):
    c_idx = pl.program_id(2)
    C = chunk

    @pl.when(c_idx == 0)
    def _():
        st_ref[...] = jnp.zeros_like(st_ref)

    row_i = lax.broadcasted_iota(jnp.int32, (C, C), 0)
    col_i = lax.broadcasted_iota(jnp.int32, (C, C), 1)
    causal = row_i >= col_i
    tri = causal.astype(BF16)

    lb = lb_ref[...]
    f = _forget_gate(f_ref[...], lb)
    log_f = jnp.log(f)
    k_scr[...] = 1.0 - f
    spread = jnp.zeros((1, hb * LANES), F32)
    for c in range(n_sub):
        rows = slice(c * C, (c + 1) * C)
        b = sum(jnp.dot(tri, term.astype(BF16), preferred_element_type=F32)
                for term in _bf16_terms(log_f[rows]))
        b_scr[rows, :] = b
        b_mid = b[C // 2 - 1:C // 2, :]
        spread = jnp.maximum(spread, jnp.maximum(b[0:1, :] - b_mid, b_mid - b[C - 1:C, :]))
    factored_ok = jnp.max(spread) < FACTORED_DECAY_MAX_SPREAD

    def pairwise_scores(q, c, sl):
        rows = slice(c * C, (c + 1) * C)
        b = b_scr[rows, sl]
        t_i = lax.broadcasted_iota(jnp.int32, (C, 1), 0)
        s_i = lax.broadcasted_iota(jnp.int32, (1, C), 1)

        def column_group(g, acc):
            grp = pl.ds(pl.multiple_of(c * C + g * SUBLANES, SUBLANES), SUBLANES)
            b_g = b_scr[grp, sl]
            k_g = k_scr[grp, sl]
            for r in range(SUBLANES):
                s = g * SUBLANES + r
                decay = jnp.exp(jnp.where(t_i >= s, b - b_g[r:r + 1], -jnp.inf))
                col = jnp.sum(q * k_g[r:r + 1] * decay, axis=-1, keepdims=True)
                acc = jnp.where(s_i == s, col, acc)
            return acc

        return lax.fori_loop(0, C // SUBLANES, column_group, jnp.zeros((C, C), F32))

    nt = (((1,), (1,)), ((), ()))

    def run(factored):
        for hh in range(hb):
            sl = slice(hh * LANES, (hh + 1) * LANES)
            st = st_ref[hh]
            for c in range(n_sub):
                rows = slice(c * C, (c + 1) * C)
                q = q_ref[rows, sl]
                b = b_scr[rows, sl]
                k = k_scr[rows, sl]
                b_last = b[C - 1:C, :]
                if factored:
                    b_mid = b[C // 2 - 1:C // 2, :]
                    q_dec = q * jnp.exp(b - b_mid)
                    k_dec = k * jnp.exp(b_mid - b)
                    scores = lax.dot_general(q_dec.astype(BF16), k_dec.astype(BF16), nt,
                                             preferred_element_type=F32)
                    scores = jnp.where(causal, scores, 0.0)
                    qe = q_dec * jnp.exp(b_mid)
                    kl = k_dec * jnp.exp(b_last - b_mid)
                else:
                    scores = pairwise_scores(q, c, sl)
                    qe = q * jnp.exp(b)
                    kl = k * jnp.exp(b_last - b)
                v = i_ref[rows, sl]
                intra = jnp.dot(scores.astype(BF16), v.astype(BF16), preferred_element_type=F32)
                inter = lax.dot_general(qe.astype(BF16), st.astype(BF16), nt, preferred_element_type=F32)
                st = st * jnp.exp(b_last) + jnp.dot(v.T.astype(BF16), kl.astype(BF16),
                                                    preferred_element_type=F32)
                ma_ref[rows, sl] = _gate_epilogue(inter + intra, gon_ref[:, sl],
                                                  og_ref[rows, sl], ga_ref[rows, sl])
            st_ref[hh] = st

    @pl.when(factored_ok)
    def _():
        run(True)

    @pl.when(jnp.logical_not(factored_ok))
    def _():
        run(False)

    @pl.when(c_idx == pl.num_programs(2) - 1)
    def _():
        for hh in range(hb):
            so_ref[hh] = st_ref[hh].T


def _hgrn_prompt_call(proj, lb, g_onorm, bsz, t, n_heads, d_conv, n_rows):
    d = n_heads * LANES
    chunk = _tile(t, 128, LANES)
    n_sub = 2 if t % (2 * chunk) == 0 else 1
    tc = chunk * n_sub
    hb = _tile(n_heads, 16, 1)
    w = hb * LANES
    hcols = n_heads // hb
    ga0 = (4 * d + 2 * d_conv) // w
    per_b = t // tc

    def col(base):
        return pl.BlockSpec((tc, w), lambda b, h, c: (b * per_b + c, base + h))

    body = functools.partial(_hgrn_prompt_body, chunk=chunk, n_sub=n_sub, hb=hb)
    return pl.pallas_call(
        body,
        grid=(bsz, hcols, per_b),
        in_specs=[col(0), col(hcols), col(2 * hcols), col(3 * hcols), col(ga0),
                  pl.BlockSpec((1, w), lambda b, h, c: (0, h)),
                  pl.BlockSpec((1, w), lambda b, h, c: (0, h))],
        out_specs=[pl.BlockSpec((tc, w), lambda b, h, c: (b * per_b + c, h)),
                   pl.BlockSpec((None, hb, LANES, LANES), lambda b, h, c: (b, h, 0, 0))],
        out_shape=[jax.ShapeDtypeStruct((n_rows, d), BF16),
                   jax.ShapeDtypeStruct((bsz, n_heads, LANES, LANES), F32)],
        scratch_shapes=[pltpu.VMEM((hb, LANES, LANES), F32)] + [pltpu.VMEM((tc, w), F32)] * 2,
        compiler_params=_params("arbitrary", "arbitrary", "arbitrary"),
        name="hgrn_prompt",
    )(proj, proj, proj, proj, proj, lb, g_onorm)


def _hgrn_sample_body(q_ref, f_ref, i_ref, og_ref, ga_ref, lb_ref, gon_ref, s_ref, ma_in_ref,
                      ma_ref, so_ref, *, t_s, n_heads):
    del ma_in_ref
    nb = SUBLANES // t_s
    row = lax.broadcasted_iota(jnp.int32, (SUBLANES, 1), 0)
    tok = row % t_s
    contract0 = (((0,), (0,)), ((), ()))

    def head(h, carry):
        sl = pl.ds(pl.multiple_of(h * LANES, LANES), LANES)
        q = q_ref[:, sl]
        v = i_ref[:, sl]
        lb = lb_ref[:, sl]
        f = _forget_gate(f_ref[:, sl], lb)
        log_f = jnp.log(f)
        k = 1.0 - f
        b = log_f
        for d in range(1, t_s):
            b = b + jnp.where(tok >= d, pltpu.roll(log_f, d, 0), 0.0)
        o = jnp.zeros((SUBLANES, LANES), F32)
        for d in range(t_s):
            kd, bd, vd = (k, b, v) if d == 0 else (pltpu.roll(k, d, 0), pltpu.roll(b, d, 0),
                                                  pltpu.roll(v, d, 0))
            decay = jnp.exp(jnp.where(tok >= d, b - bd, -jnp.inf))
            score = jnp.sum(q * kd * decay, axis=-1, keepdims=True)
            o = o + score * jnp.where(tok >= d, vd, 0.0)
        qe = (q * jnp.exp(b)).astype(BF16)
        for bi in range(nb):
            in_b = (row >= bi * t_s) & (row < (bi + 1) * t_s)
            s0 = s_ref[bi, h]
            inter = jnp.dot(qe, s0.astype(BF16), preferred_element_type=F32)
            o = o + jnp.where(in_b, inter, 0.0)
            b_last = b[(bi + 1) * t_s - 1:(bi + 1) * t_s, :]
            kk = jnp.where(in_b, k * jnp.exp(jnp.where(in_b, b_last - b, 0.0)), 0.0)
            vv = jnp.where(in_b, v, 0.0)
            upd = lax.dot_general(kk.astype(BF16), vv.astype(BF16), contract0,
                                  preferred_element_type=F32)
            hi, mid, lo = _bf16_terms(jnp.exp(b_last))
            d_rows = jnp.where(row == 0, hi, jnp.where(row == 1, mid, jnp.where(row == 2, lo, 0.0)))
            one_rows = jnp.where(row < 3, jnp.ones((SUBLANES, LANES), F32), 0.0)
            decay_kv = lax.dot_general(d_rows.astype(BF16), one_rows.astype(BF16), contract0,
                                       preferred_element_type=F32)
            so_ref[bi, h] = s0 * decay_kv + upd
        ma_ref[:, sl] = _gate_epilogue(o, gon_ref[:, sl], og_ref[:, sl], ga_ref[:, sl])
        return carry

    lax.fori_loop(0, n_heads, head, 0, unroll=_tile(n_heads, 8, 1))


def _hgrn_sample_call(proj, lb, g_onorm, state, ma, row0, t_s, d_conv):
    bs, n_heads = state.shape[0], state.shape[1]
    d = n_heads * LANES
    assert SUBLANES % t_s == 0 and row0 % SUBLANES == 0
    nb = SUBLANES // t_s
    r0 = row0 // SUBLANES
    ga0 = (4 * d + 2 * d_conv) // d

    def col(base):
        return pl.BlockSpec((SUBLANES, d), lambda i: (r0 + i, base))

    body = functools.partial(_hgrn_sample_body, t_s=t_s, n_heads=n_heads)
    return pl.pallas_call(
        body,
        grid=(bs // nb,),
        in_specs=[col(0), col(1), col(2), col(3), col(ga0),
                  pl.BlockSpec((1, d), lambda i: (0, 0)),
                  pl.BlockSpec((1, d), lambda i: (0, 0)),
                  pl.BlockSpec((nb, n_heads, LANES, LANES), lambda i: (i, 0, 0, 0)),
                  pl.BlockSpec(memory_space=pl.ANY)],
        out_specs=[pl.BlockSpec((SUBLANES, d), lambda i: (r0 + i, 0)),
                   pl.BlockSpec((nb, n_heads, LANES, LANES), lambda i: (i, 0, 0, 0))],
        out_shape=[jax.ShapeDtypeStruct(ma.shape, ma.dtype),
                   jax.ShapeDtypeStruct(state.shape, F32)],
        input_output_aliases={8: 0},
        compiler_params=_params("arbitrary"),
        name="hgrn_sample",
    )(proj, proj, proj, proj, proj, lb, g_onorm, state, ma)


def _ln_swish(x, g, b):
    mu = jnp.mean(x, axis=-1, keepdims=True)
    xc = x - mu
    var = jnp.mean(xc * xc, axis=-1, keepdims=True)
    return _silu(xc * lax.rsqrt(var + EPS) * g + b)


def _conv_prompt_body(ga_ref, gb_ref, w_ref, bdw_ref, lg_ref, lbias_ref, o_ref, cache_ref,
                      ext_ref, conv_ref, *, width, hist, rc):
    t_idx = pl.program_id(1)
    tt = ga_ref.shape[0]
    dc = ga_ref.shape[1]

    @pl.when(t_idx == 0)
    def _():
        ext_ref[0:hist, :] = jnp.zeros((hist, dc), F32)

    @pl.when(t_idx > 0)
    def _():
        ext_ref[0:hist, :] = ext_ref[tt:tt + hist, :]

    ext_ref[hist:hist + tt, :] = ga_ref[...] * _sigmoid(gb_ref[...])
    off = hist - (width - 1)

    def strip(l, carry):
        lanes = pl.ds(pl.multiple_of(l * LANES, LANES), LANES)
        for r0 in range(0, tt, rc):
            win = ext_ref[r0:r0 + rc + hist, lanes]
            acc = jnp.zeros((rc, LANES), F32) + bdw_ref[:, lanes]
            for s in range(SUBLANES):
                taps = [j for j in range(width) if (off + j) % SUBLANES == s]
                if not taps:
                    continue
                shifted = win if s == 0 else pltpu.roll(win, rc + hist - s, 0)
                for j in taps:
                    a = (off + j) // SUBLANES * SUBLANES
                    acc = acc + w_ref[j:j + 1, lanes] * shifted[a:a + rc]
            conv_ref[r0:r0 + rc, lanes] = acc
        return carry

    lax.fori_loop(0, dc // LANES, strip, 0)

    def norm(i, carry):
        rows = pl.ds(pl.multiple_of(i * rc, rc), rc)
        o_ref[rows, :] = _ln_swish(conv_ref[rows, :], lg_ref[...], lbias_ref[...]).astype(o_ref.dtype)
        return carry

    lax.fori_loop(0, tt // rc, norm, 0)

    @pl.when(t_idx == pl.num_programs(1) - 1)
    def _():
        cache_ref[...] = ext_ref[hist + tt - (width - 1):hist + tt, :]


def _conv_prompt_call(proj, w_dw, b_dw, ln_g, ln_b, bsz, t, d_hgrn, n_rows):
    width, dc = w_dw.shape
    tt = _tile(t, 256, 32)
    hist = 32
    assert width - 1 <= hist <= tt and (4 * d_hgrn) % dc == 0
    c0 = 4 * d_hgrn // dc
    per_b = t // tt
    vec = lambda a: a.reshape(1, dc)
    body = functools.partial(_conv_prompt_body, width=width, hist=hist, rc=32)
    return pl.pallas_call(
        body,
        grid=(bsz, per_b),
        in_specs=[pl.BlockSpec((tt, dc), lambda b, i: (b * per_b + i, c0)),
                  pl.BlockSpec((tt, dc), lambda b, i: (b * per_b + i, c0 + 1)),
                  pl.BlockSpec((width, dc), lambda b, i: (0, 0)),
                  pl.BlockSpec((1, dc), lambda b, i: (0, 0)),
                  pl.BlockSpec((1, dc), lambda b, i: (0, 0)),
                  pl.BlockSpec((1, dc), lambda b, i: (0, 0))],
        out_specs=[pl.BlockSpec((tt, dc), lambda b, i: (b * per_b + i, 0)),
                   pl.BlockSpec((None, width - 1, dc), lambda b, i: (b, 0, 0))],
        out_shape=[jax.ShapeDtypeStruct((n_rows, dc), BF16),
                   jax.ShapeDtypeStruct((bsz, width - 1, dc), F32)],
        scratch_shapes=[pltpu.VMEM((hist + tt, dc), F32), pltpu.VMEM((tt, dc), F32)],
        compiler_params=_params("arbitrary", "arbitrary"),
        name="conv_prompt",
    )(proj, proj, w_dw, vec(b_dw), vec(ln_g), vec(ln_b))


def _conv_sample_body(ga_ref, gb_ref, cache_ref, w_ref, bdw_ref, lg_ref, lbias_ref, vc_in_ref,
                      o_ref, ncache_ref, ext_ref, *, width, t_s, bb):
    del vc_in_ref
    u = ga_ref[...] * _sigmoid(gb_ref[...])
    for bi in range(bb):
        ext_ref[0:width - 1, :] = cache_ref[bi]
        ext_ref[width - 1:width - 1 + t_s, :] = u[bi * t_s:(bi + 1) * t_s]
        acc = jnp.zeros((t_s, u.shape[1]), F32) + bdw_ref[...]
        for j in range(width):
            acc = acc + w_ref[j:j + 1, :] * ext_ref[j:j + t_s, :]
        ncache_ref[bi] = ext_ref[t_s:t_s + width - 1, :]
        o_ref[bi * t_s:(bi + 1) * t_s, :] = _ln_swish(acc, lg_ref[...], lbias_ref[...]).astype(o_ref.dtype)


def _conv_sample_call(proj, cache, w_dw, b_dw, ln_g, ln_b, vconv, row0, t_s, d_hgrn):
    width, dc = w_dw.shape
    bs = cache.shape[0]
    bb = _tile(bs, 8, 1)
    rows = bb * t_s
    assert rows % 16 == 0 and row0 % rows == 0
    r0 = row0 // rows
    c0 = 4 * d_hgrn // dc
    vec = lambda a: a.reshape(1, dc)
    body = functools.partial(_conv_sample_body, width=width, t_s=t_s, bb=bb)
    return pl.pallas_call(
        body,
        grid=(bs // bb,),
        in_specs=[pl.BlockSpec((rows, dc), lambda i: (r0 + i, c0)),
                  pl.BlockSpec((rows, dc), lambda i: (r0 + i, c0 + 1)),
                  pl.BlockSpec((bb, width - 1, dc), lambda i: (i, 0, 0)),
                  pl.BlockSpec((width, dc), lambda i: (0, 0)),
                  pl.BlockSpec((1, dc), lambda i: (0, 0)),
                  pl.BlockSpec((1, dc), lambda i: (0, 0)),
                  pl.BlockSpec((1, dc), lambda i: (0, 0)),
                  pl.BlockSpec(memory_space=pl.ANY)],
        out_specs=[pl.BlockSpec((rows, dc), lambda i: (r0 + i, 0)),
                   pl.BlockSpec((bb, width - 1, dc), lambda i: (i, 0, 0))],
        out_shape=[jax.ShapeDtypeStruct(vconv.shape, vconv.dtype),
                   jax.ShapeDtypeStruct(cache.shape, F32)],
        scratch_shapes=[pltpu.VMEM((width - 1 + t_s + SUBLANES, dc), F32)],
        input_output_aliases={7: 0},
        compiler_params=_params("arbitrary"),
        name="conv_sample",
    )(proj, proj, cache, w_dw, vec(b_dw), vec(ln_g), vec(ln_b), vconv)


def _expert_changed(be_ref, r):
    return jnp.logical_or(r == 0, be_ref[r] != be_ref[jnp.maximum(r - 1, 0)])


def _weight_copies(w_hbms, buf_refs, sem_ref, e, j, slot):
    copies = []
    for w_hbm, buf_ref in zip(w_hbms, buf_refs):
        tcol = buf_ref.shape[-1]
        cols = pl.ds(pl.multiple_of(j * tcol, tcol), tcol)
        copies.append(pltpu.make_async_copy(w_hbm.at[e, :, cols], buf_ref.at[slot], sem_ref.at[slot]))
    return copies


def _stage_expert_weights(be_ref, nr_ref, nx_ref, w_hbms, buf_refs, bf16_refs, sem_ref, cnt_ref):
    j, r = pl.program_id(0), pl.program_id(1)

    @pl.when(jnp.logical_and(j == 0, r == 0))
    def _():
        cnt_ref[0] = 0
        for c in _weight_copies(w_hbms, buf_refs, sem_ref, be_ref[0], 0, 0):
            c.start()

    @pl.when(jnp.logical_and(r < nr_ref[0], _expert_changed(be_ref, r)))
    def _():
        slot = cnt_ref[0] % 2
        for c in _weight_copies(w_hbms, buf_refs, sem_ref, be_ref[r], j, slot):
            c.wait()
        wraps = nx_ref[r] >= nr_ref[0]
        e_next = be_ref[jnp.where(wraps, 0, nx_ref[r])]
        j_next = jnp.where(wraps, j + 1, j)

        @pl.when(j_next < pl.num_programs(0))
        def _():
            for c in _weight_copies(w_hbms, buf_refs, sem_ref, e_next, j_next, 1 - slot):
                c.start()

        for buf_ref, dst_ref in zip(buf_refs, bf16_refs):
            dst_ref[...] = buf_ref[slot].astype(BF16)
        cnt_ref[0] = cnt_ref[0] + 1


def _moe_up_body(be_ref, nr_ref, nx_ref, x_ref, w1_hbm, w3_hbm, g_ref,
                 w1f_ref, w3f_ref, w1b_ref, w3b_ref, sem_ref, cnt_ref):
    _stage_expert_weights(be_ref, nr_ref, nx_ref, (w1_hbm, w3_hbm), (w1f_ref, w3f_ref),
                          (w1b_ref, w3b_ref), sem_ref, cnt_ref)

    @pl.when(pl.program_id(1) < nr_ref[0])
    def _():
        x = x_ref[...]
        a = jnp.dot(x, w1b_ref[...], preferred_element_type=F32)
        b = jnp.dot(x, w3b_ref[...], preferred_element_type=F32)
        g_ref[...] = (_silu(a) * b).astype(g_ref.dtype)


def _moe_down_body(be_ref, nr_ref, nx_ref, g_ref, w2_hbm, y_ref, w2f_ref, w2b_ref, sem_ref, cnt_ref):
    _stage_expert_weights(be_ref, nr_ref, nx_ref, (w2_hbm,), (w2f_ref,), (w2b_ref,), sem_ref, cnt_ref)

    @pl.when(pl.program_id(1) < nr_ref[0])
    def _():
        y_ref[...] = _pack_bf16_halves(jnp.dot(g_ref[...], w2b_ref[...], preferred_element_type=F32))


def _dispatch_body(tok_ref, nr_ref, cnt_ref, first_ref, h2_hbm, o_ref, buf_ref, sem_ref):
    r = pl.program_id(0)
    tm = o_ref.shape[0]
    half = buf_ref.shape[-1]
    n_groups = tm // ROW_GROUP

    def start_group(step, slot, c):
        for j in range(ROW_GROUP):
            row = c * ROW_GROUP + j
            pltpu.make_async_copy(h2_hbm.at[pl.ds(tok_ref[first_ref[step] + row], 1)],
                                  buf_ref.at[slot, pl.ds(row, 1)], sem_ref.at[slot]).start()

    @pl.when(r == 0)
    def _():
        buf_ref[...] = jnp.zeros_like(buf_ref)

        def prime(c, carry):
            start_group(0, 0, c)
            return carry

        lax.fori_loop(0, cnt_ref[0] // ROW_GROUP, prime, 0)

    @pl.when(r < nr_ref[0])
    def _():
        slot = r % 2
        rows = pl.ds(0, pl.multiple_of(cnt_ref[r], ROW_GROUP))
        pltpu.make_async_copy(h2_hbm.at[rows], buf_ref.at[slot, rows], sem_ref.at[slot]).wait()
        nxt = jnp.minimum(r + 1, pl.num_programs(0) - 1)
        n_next = jnp.where(r + 1 < nr_ref[0], cnt_ref[nxt] // ROW_GROUP, 0)

        def group(c, carry, prefetch):
            if prefetch:
                start_group(r + 1, 1 - slot, c)
            rs = pl.ds(pl.multiple_of(c * ROW_GROUP, ROW_GROUP), ROW_GROUP)
            lo, hi = _unpack_bf16_halves(buf_ref[slot, rs, :])
            o_ref[rs, 0:half] = lo.astype(o_ref.dtype)
            o_ref[rs, half:] = hi.astype(o_ref.dtype)
            return carry

        lax.fori_loop(0, n_next, functools.partial(group, prefetch=True), 0)
        lax.fori_loop(n_next, n_groups, functools.partial(group, prefetch=False), 0)


def _dispatch_call(h2p, sorted_tok, n_real, block_cnt, block_first, tm):
    half = h2p.shape[1]
    d = 2 * half
    nblk = block_cnt.shape[0]
    return pl.pallas_call(
        _dispatch_body,
        grid_spec=pltpu.PrefetchScalarGridSpec(
            num_scalar_prefetch=4,
            grid=(nblk,),
            in_specs=[pl.BlockSpec(memory_space=pl.ANY)],
            out_specs=pl.BlockSpec((tm, d), lambda r, o, nr, cnt, first: (jnp.minimum(r, nr[0] - 1), 0)),
            scratch_shapes=[pltpu.VMEM((2, tm, half), jnp.uint32), pltpu.SemaphoreType.DMA((2,))]),
        out_shape=jax.ShapeDtypeStruct((nblk * tm, d), BF16),
        compiler_params=_params("arbitrary"),
        name="moe_dispatch",
    )(sorted_tok, n_real, block_cnt, block_first, h2p)


def _moe_experts(x_sorted, block_e, n_real, next_run, w1, w3, w2, tm):
    l, d = x_sorted.shape
    n_exp, _, dh = w1.shape
    nblk = l // tm
    th = _tile(dh, 512, LANES)
    rr = lambda r, nr: jnp.minimum(r, nr[0] - 1)
    hbm = pl.BlockSpec(memory_space=pl.ANY)

    g = pl.pallas_call(
        _moe_up_body,
        grid_spec=pltpu.PrefetchScalarGridSpec(
            num_scalar_prefetch=3,
            grid=(dh // th, nblk),
            in_specs=[pl.BlockSpec((tm, d), lambda j, r, be, nr, nx: (rr(r, nr), 0)), hbm, hbm],
            out_specs=pl.BlockSpec((tm, th), lambda j, r, be, nr, nx: (rr(r, nr), j)),
            scratch_shapes=[pltpu.VMEM((2, d, th), F32), pltpu.VMEM((2, d, th), F32),
                            pltpu.VMEM((d, th), BF16), pltpu.VMEM((d, th), BF16),
                            pltpu.SemaphoreType.DMA((2,)), pltpu.SMEM((1,), jnp.int32)]),
        out_shape=jax.ShapeDtypeStruct((l, dh), BF16),
        compiler_params=_params("arbitrary", "arbitrary"),
        name="moe_up",
    )(block_e, n_real, next_run, x_sorted, w1, w3)

    return pl.pallas_call(
        _moe_down_body,
        grid_spec=pltpu.PrefetchScalarGridSpec(
            num_scalar_prefetch=3,
            grid=(1, nblk),
            in_specs=[pl.BlockSpec((tm, dh), lambda j, r, be, nr, nx: (rr(r, nr), 0)), hbm],
            out_specs=pl.BlockSpec((tm, d // 2), lambda j, r, be, nr, nx: (rr(r, nr), 0)),
            scratch_shapes=[pltpu.VMEM((2, dh, d), F32), pltpu.VMEM((dh, d), BF16),
                            pltpu.SemaphoreType.DMA((2,)), pltpu.SMEM((1,), jnp.int32)]),
        out_shape=jax.ShapeDtypeStruct((l, d // 2), jnp.uint32),
        compiler_params=_params("arbitrary", "arbitrary"),
        name="moe_down",
    )(block_e, n_real, next_run, g, w2)


def _route(logits, n_groups, n_experts, tm):
    n = logits.shape[0]
    eg = n_experts // n_groups
    l1 = logits[:, :n_groups]
    p1 = jax.nn.softmax(l1, axis=-1)
    grp = jnp.argmax(l1, axis=-1)
    p_grp = jnp.max(p1, axis=-1, keepdims=True)
    l2 = logits[:, n_groups:n_groups + n_experts].reshape(n, n_groups, eg)
    l2g = jnp.take_along_axis(l2, grp[:, None, None], axis=1)[:, 0]
    top_v, top_i = lax.top_k(l2g, TOP_K)
    weights = p_grp * jax.nn.softmax(top_v, axis=-1)
    expert = (grp[:, None] * eg + top_i).astype(jnp.int32)

    a = n * TOP_K
    i32 = lambda v: v.astype(jnp.int32)
    e = expert.reshape(-1)
    counts = i32(jnp.bincount(e, length=n_experts))
    padded = (counts + tm - 1) // tm * tm
    ends = jnp.cumsum(padded)
    starts = ends - padded
    seg_start = jnp.cumsum(counts) - counts
    order = jnp.argsort(e)
    rank = jnp.argsort(order)
    pos = i32(starts[e] + rank - seg_start[e]).reshape(n, TOP_K)
    nblk = -(-a // tm) + n_experts
    block_e = i32(jnp.minimum(jnp.searchsorted(ends, jnp.arange(nblk) * tm, side='right'),
                              n_experts - 1))
    n_real = i32(ends[-1:] // tm)
    block_cnt = jnp.clip(counts[block_e] - (jnp.arange(nblk) * tm - starts[block_e]), 0, tm)
    block_cnt = i32(-(-block_cnt // ROW_GROUP) * ROW_GROUP)
    next_run = i32(ends[block_e] // tm)
    block_first = i32(jnp.minimum(seg_start[block_e] + jnp.arange(nblk) * tm - starts[block_e], a))
    sorted_tok = jnp.pad(i32(order // TOP_K), (0, ROW_GROUP))
    return weights, sorted_tok, pos, block_e, n_real, block_cnt, block_first, next_run


def kernel(x_prompt, x_sample, state_hgrn, cache_conv, c_prompt, c_sample, g_mix, w_ada, b_ada, w_in, lb_param, g_onorm, w_dw, b_dw, ln_g, ln_b, w_pw2, b_pw2, w_out, g_ffn, w_r1, b_r1, w_r2, b_r2, moe_w1, moe_w3, moe_w2, g_final):
    bp, t_p, d = x_prompt.shape
    bs, t_s, _ = x_sample.shape
    depth, _, n_heads, head_k, head_v = state_hgrn.shape
    assert depth == 1 and head_k == LANES and head_v == LANES
    d_hgrn = n_heads * head_k
    d_conv = w_dw.shape[-1]
    n_groups, n_experts = w_r1.shape[-1], w_r2.shape[-1]
    np_, ns = bp * t_p, bs * t_s
    n = np_ + ns
    rows = _Rows(np_, t_p, ns, t_s, d)
    vec = lambda a: a.reshape(1, -1)

    assert bs % SUBLANES == 0
    c_all = jnp.concatenate([c_sample, c_prompt], axis=0)
    c_all = jnp.pad(c_all, ((0, -c_all.shape[0] % 16), (0, 0)))
    ada = _ada_call(c_all, w_ada[0], b_ada[0], bs, t_s, bp)
    ada_p = ada_s = ada
    SH_M, SC_M, GT_M, SH_F, SC_F, GT_F = range(6)
    mod = lambda which: (lambda sample: rows.mod(sample, which))
    ada_of = lambda sample: ada
    xs = (x_prompt.reshape(np_, d), x_sample.reshape(ns, d))
    x_spec = lambda sample: rows.rows(sample, d, offset=False)
    flat = lambda width: (lambda sample: rows.rows(sample, width))
    const = lambda shape: (lambda sample: rows.const(shape))

    (h,) = _rowwise(
        _norm1_body, rows, "norm1",
        [xs[0], vec(g_mix[0]), ada_p, ada_p], [xs[1], vec(g_mix[0]), ada_s, ada_s],
        [x_spec, const((1, d)), mod(SC_M), mod(SH_M)], [d], [BF16])
    proj = _mm_call(h, w_in[0], F32, 544, 1024, "in_proj")

    lb = jax.nn.softmax(lb_param.astype(F32), axis=0)[0:1]
    ma, state_p = _hgrn_prompt_call(proj, lb, vec(g_onorm[0]), bp, t_p, n_heads, d_conv, n)
    ma, state_s = _hgrn_sample_call(proj, lb, vec(g_onorm[0]), state_hgrn[0], ma, np_, t_s, d_conv)

    vconv, cache_p = _conv_prompt_call(proj, w_dw[0], b_dw[0], ln_g[0], ln_b[0], bp, t_p, d_hgrn, n)
    vconv, cache_s = _conv_sample_call(proj, cache_conv[0], w_dw[0], b_dw[0], ln_g[0], ln_b[0],
                                       vconv, np_, t_s, d_hgrn)

    merged = _mm_pw2_call(vconv, w_pw2[0], b_pw2[0], ma, proj, 4 * d_hgrn + 2 * d_conv + d)
    attn = _mm_call(merged, w_out[0], F32, 1088, 512, "out_proj")

    n_r = n_groups + n_experts
    n_r_pad = -(-n_r // LANES) * LANES
    w_r = jnp.concatenate([w_r1[0], w_r2[0], jnp.zeros((d, n_r_pad - n_r), F32)], axis=1)
    b_r = jnp.concatenate([b_r1[0], b_r2[0], jnp.zeros((n_r_pad - n_r,), F32)]).reshape(1, n_r_pad)
    x1, h2p, logits = _rowwise(
        _norm2_body, rows, "norm2",
        [xs[0], attn, ada_p, vec(g_ffn[0]), ada_p, ada_p, w_r, b_r],
        [xs[1], attn, ada_s, vec(g_ffn[0]), ada_s, ada_s, w_r, b_r],
        [x_spec, flat(d), mod(GT_M), const((1, d)), mod(SC_F), mod(SH_F),
         const((d, n_r_pad)), const((1, n_r_pad))],
        [d, d // 2, n_r_pad], [F32, jnp.uint32, F32])

    tm = 256
    weights, sorted_tok, pos, block_e, n_real, block_cnt, block_first, next_run = _route(
        logits, n_groups, n_experts, tm)
    x_sorted = _dispatch_call(h2p, sorted_tok, n_real, block_cnt, block_first, tm)
    y_sorted = _moe_experts(x_sorted, block_e, n_real, next_run, moe_w1[0], moe_w3[0], moe_w2[0], tm)
    pos_flat = pos.T.reshape(-1)

    outs = []
    for sample in (False, True):
        tt = rows.tt_s if sample else rows.tt_p
        wrap = lambda spec: pl.BlockSpec(spec.block_shape, lambda i, p, f=spec.index_map: f(i))
        outs.append(pl.pallas_call(
            functools.partial(_final_body, row0=np_ if sample else 0),
            grid_spec=pltpu.PrefetchScalarGridSpec(
                num_scalar_prefetch=1,
                grid=rows.grid(sample),
                in_specs=[wrap(rows.rows(sample, d)), wrap(rows.rows(sample, TOP_K)),
                          wrap(rows.mod(sample, GT_F)), wrap(rows.const((1, d))),
                          pl.BlockSpec(memory_space=pl.ANY)],
                out_specs=wrap(rows.rows(sample, d, offset=False)),
                scratch_shapes=[pltpu.VMEM((2, TOP_K, tt, d // 2), jnp.uint32),
                                pltpu.SemaphoreType.DMA((2,))]),
            out_shape=jax.ShapeDtypeStruct((ns if sample else np_, d), F32),
            compiler_params=_params("arbitrary"),
            name="final_sample" if sample else "final_prompt",
        )(pos_flat, x1, weights, ada_of(sample), vec(g_final), y_sorted))
    y_prompt = outs[0].reshape(bp, t_p, d)
    y_sample = outs[1].reshape(bs, t_s, d)
    return (y_prompt, y_sample, state_p[None], cache_p[None], state_s[None], cache_s[None])
```

```python
import functools

import jax
import jax.numpy as jnp
from jax import lax
from jax.experimental import pallas as pl
from jax.experimental.pallas import tpu as pltpu

F32 = jnp.float32
BF16 = jnp.bfloat16
EPS = 1e-6
TOP_K = 2
LANES = 128
SUBLANES = 8
ROW_GROUP = SUBLANES
VMEM_LIMIT = 60 * 1024 * 1024
FACTORED_DECAY_MAX_SPREAD = 80.0


def _params(*sem):
    return pltpu.CompilerParams(dimension_semantics=sem, vmem_limit_bytes=VMEM_LIMIT)


def _tile(n, target, mult):
    best = None
    for t in range(mult, min(n, target) + 1, mult):
        if n % t == 0:
            best = t
    assert best is not None, (n, target, mult)
    return best


def _sigmoid(x):
    return 0.5 * jnp.tanh(0.5 * x) + 0.5


def _forget_gate(f_logit, lb):
    return lb + (1.0 - lb) / (1.0 + jnp.exp(-f_logit))


def _silu(x):
    return x * _sigmoid(x)


def _ada_body(c_ref, w_ref, b_ref, o_ref, *, bs, t_s, bp):
    s = _silu(c_ref[...]).astype(BF16)
    ada = jnp.dot(s, w_ref[...].astype(BF16), preferred_element_type=F32) + b_ref[...]
    ns = bs * t_s
    r_i = lax.broadcasted_iota(jnp.int32, (ns, bs), 0)
    b_i = lax.broadcasted_iota(jnp.int32, (ns, bs), 1)
    lo = b_i * t_s
    expand = jnp.where((r_i >= lo) & (r_i < lo + t_s), 1.0, 0.0).astype(BF16)
    o_ref[0:ns, :] = sum(jnp.dot(expand, term.astype(BF16), preferred_element_type=F32)
                         for term in _bf16_terms(ada[0:bs]))
    for b in range(bp):
        o_ref[ns + b * SUBLANES:ns + (b + 1) * SUBLANES, :] = jnp.broadcast_to(
            ada[bs + b:bs + b + 1], (SUBLANES, ada.shape[1]))


def _ada_call(c_all, w_ada, b_ada, bs, t_s, bp):
    rows, d = c_all.shape
    n6 = w_ada.shape[1]
    tn = _tile(n6, 512, LANES)
    out_rows = bs * t_s + bp * SUBLANES
    return pl.pallas_call(
        functools.partial(_ada_body, bs=bs, t_s=t_s, bp=bp),
        grid=(n6 // tn,),
        in_specs=[pl.BlockSpec((rows, d), lambda j: (0, 0)),
                  pl.BlockSpec((d, tn), lambda j: (0, j)),
                  pl.BlockSpec((1, tn), lambda j: (0, j))],
        out_specs=pl.BlockSpec((out_rows, tn), lambda j: (0, j)),
        out_shape=jax.ShapeDtypeStruct((out_rows, n6), F32),
        compiler_params=_params("arbitrary"),
        name="ada",
    )(c_all, w_ada, b_ada.reshape(1, n6))


def _rms(x, g):
    return x * lax.rsqrt(jnp.mean(x * x, axis=-1, keepdims=True) + EPS) * g


def _mod(ref, n_rows):
    return ref[...] if ref.shape[0] == n_rows else ref[0:1, :]


def _norm1_body(x_ref, g_ref, sc_ref, sh_ref, *rest):
    o_ref = rest[-1]
    x = x_ref[...]
    tt = x.shape[0]
    y = _rms(x, g_ref[...])
    o_ref[...] = (y * (1.0 + _mod(sc_ref, tt)) + _mod(sh_ref, tt)).astype(o_ref.dtype)


def _pack_bf16_halves(x):
    w = x.shape[1] // 2
    lo = lax.bitcast_convert_type(x[:, :w].astype(BF16).astype(F32), jnp.uint32)
    hi = lax.bitcast_convert_type(x[:, w:].astype(BF16).astype(F32), jnp.uint32)
    return (lo >> 16) | hi


def _unpack_bf16_halves(u):
    lo = lax.bitcast_convert_type(u << 16, F32)
    hi = lax.bitcast_convert_type(u & jnp.uint32(0xFFFF0000), F32)
    return lo, hi


def _norm2_body(x_ref, a_ref, gt_ref, g_ref, sc_ref, sh_ref, wr_ref, br_ref, *rest):
    x1_ref, h2p_ref, lg_ref = rest[-3:]
    tt = x_ref.shape[0]
    x1 = x_ref[...] + _mod(gt_ref, tt) * a_ref[...]
    x1_ref[...] = x1
    h2 = _rms(x1, g_ref[...]) * (1.0 + _mod(sc_ref, tt)) + _mod(sh_ref, tt)
    h2p_ref[...] = _pack_bf16_halves(h2)
    lg_ref[...] = jnp.dot(h2, wr_ref[...], preferred_element_type=F32,
                          precision=lax.Precision.HIGHEST) + br_ref[...]


def _row_gather_wait(n_rows, src_hbm, dst, sem):
    pltpu.make_async_copy(src_hbm.at[pl.ds(0, n_rows)], dst, sem).wait()


def _final_body(pos_ref, x1_ref, w_ref, gt_ref, g_ref, y_hbm, o_ref, buf_ref, sem_ref, *, row0):
    i = pl.program_id(0)
    tt = x1_ref.shape[0]
    n_tok = pos_ref.shape[0] // TOP_K

    def start_rows(step, slot, first, count):
        for k in range(TOP_K):
            base = k * n_tok + row0 + step * tt
            for r in range(count):
                pltpu.make_async_copy(y_hbm.at[pl.ds(pos_ref[base + first + r], 1)],
                                      buf_ref.at[slot, k, pl.ds(first + r, 1)], sem_ref.at[slot]).start()

    @pl.when(i == 0)
    def _():
        def prime(c, carry):
            start_rows(0, 0, c * SUBLANES, SUBLANES)
            return carry

        lax.fori_loop(0, tt // SUBLANES, prime, 0)

    slot = i % 2
    for k in range(TOP_K):
        _row_gather_wait(tt, y_hbm, buf_ref.at[slot, k], sem_ref.at[slot])
    per_row_gate = gt_ref.shape[0] == tt

    half = x1_ref.shape[1] // 2

    def row_group(c, carry, prefetch):
        if prefetch:
            start_rows(i + 1, 1 - slot, c * SUBLANES, SUBLANES)
        rs = pl.ds(pl.multiple_of(c * SUBLANES, SUBLANES), SUBLANES)
        moe_lo = jnp.zeros((SUBLANES, half), F32)
        moe_hi = jnp.zeros((SUBLANES, half), F32)
        for k in range(TOP_K):
            y_lo, y_hi = _unpack_bf16_halves(buf_ref[slot, k, rs, :])
            moe_lo = moe_lo + w_ref[rs, k:k + 1] * y_lo
            moe_hi = moe_hi + w_ref[rs, k:k + 1] * y_hi
        gate = gt_ref[rs, :] if per_row_gate else gt_ref[0:1, :]
        x2_lo = x1_ref[rs, 0:half] + gate[:, 0:half] * moe_lo
        x2_hi = x1_ref[rs, half:] + gate[:, half:] * moe_hi
        ms = (jnp.sum(x2_lo * x2_lo, axis=-1, keepdims=True)
              + jnp.sum(x2_hi * x2_hi, axis=-1, keepdims=True)) / x1_ref.shape[1]
        scale = lax.rsqrt(ms + EPS)
        o_ref[rs, 0:half] = x2_lo * scale * g_ref[:, 0:half]
        o_ref[rs, half:] = x2_hi * scale * g_ref[:, half:]
        return carry

    has_next = i + 1 < pl.num_programs(0)
    unroll = _tile(tt // SUBLANES, 8, 1)

    @pl.when(has_next)
    def _():
        lax.fori_loop(0, tt // SUBLANES, functools.partial(row_group, prefetch=True), 0, unroll=unroll)

    @pl.when(jnp.logical_not(has_next))
    def _():
        lax.fori_loop(0, tt // SUBLANES, functools.partial(row_group, prefetch=False), 0, unroll=unroll)


class _Rows:
    def __init__(self, np_, t_p, ns, t_s, d):
        self.np_, self.t_p, self.ns, self.t_s, self.d = np_, t_p, ns, t_s, d
        self.tt_p = _tile(t_p, 256, 16)
        self.tt_s = _tile(ns, 128, 16)
        assert np_ % self.tt_s == 0
        self.n = np_ + ns

    def grid(self, sample):
        return (self.ns // self.tt_s,) if sample else (self.np_ // self.tt_p,)

    def rows(self, sample, width, offset=True):
        if sample:
            base = self.np_ // self.tt_s if offset else 0
            return pl.BlockSpec((self.tt_s, width), lambda i: (base + i, 0))
        return pl.BlockSpec((self.tt_p, width), lambda i: (i, 0))

    def mod(self, sample, which):
        if sample:
            return pl.BlockSpec((self.tt_s, self.d), lambda i: (i, which))
        per = self.t_p // self.tt_p
        base = self.ns // SUBLANES
        return pl.BlockSpec((SUBLANES, self.d), lambda i: (base + i // per, which))

    def const(self, shape):
        return pl.BlockSpec(shape, lambda i: tuple(0 for _ in shape))


def _rowwise(body, rows, name, prompt_in, sample_in, specs, out_widths, out_dtypes):
    outs = None
    for sample, ops in ((False, prompt_in), (True, sample_in)):
        in_specs = [s(sample) for s in specs]
        ops = list(ops)
        aliases = {}
        if outs is not None:
            for k, o in enumerate(outs):
                aliases[len(ops)] = k
                ops.append(o)
                in_specs.append(pl.BlockSpec(memory_space=pl.ANY))
        outs = pl.pallas_call(
            body,
            grid=rows.grid(sample),
            in_specs=in_specs,
            out_specs=[rows.rows(sample, w) for w in out_widths],
            out_shape=[jax.ShapeDtypeStruct((rows.n, w), dt) for w, dt in zip(out_widths, out_dtypes)],
            input_output_aliases=aliases,
            compiler_params=_params("arbitrary"),
            name=name + ("_sample" if sample else "_prompt"),
        )(*ops)
    return outs


def _mm_body(x_ref, w_ref, o_ref, wb_ref):
    @pl.when(pl.program_id(1) == 0)
    def _():
        wb_ref[...] = w_ref[...].astype(BF16)
    o_ref[...] = jnp.dot(x_ref[...], wb_ref[...], preferred_element_type=F32).astype(o_ref.dtype)


def _mm_pw2_body(x_ref, w_ref, b_ref, ma_ref, gb_ref, o_ref, wb_ref):
    @pl.when(pl.program_id(1) == 0)
    def _():
        wb_ref[...] = w_ref[...].astype(BF16)
    yb = jnp.dot(x_ref[...], wb_ref[...], preferred_element_type=F32) + b_ref[...]
    o_ref[...] = (ma_ref[...].astype(F32) + _sigmoid(gb_ref[...]) * yb).astype(o_ref.dtype)


def _mm_call(x, w, out_dtype, tm_target, tn_target, name):
    m, k = x.shape
    n = w.shape[1]
    tm = _tile(m, tm_target, 16)
    tn = _tile(n, tn_target, LANES)
    return pl.pallas_call(
        _mm_body,
        grid=(n // tn, m // tm),
        in_specs=[pl.BlockSpec((tm, k), lambda j, i: (i, 0)),
                  pl.BlockSpec((k, tn), lambda j, i: (0, j))],
        out_specs=pl.BlockSpec((tm, tn), lambda j, i: (i, j)),
        out_shape=jax.ShapeDtypeStruct((m, n), out_dtype),
        scratch_shapes=[pltpu.VMEM((k, tn), BF16)],
        compiler_params=_params("arbitrary", "arbitrary"),
        name=name,
    )(x, w)


def _mm_pw2_call(x, w, bias, ma, proj, gate_b_col):
    m, k = x.shape
    n = w.shape[1]
    tm = _tile(m, 544, 16)
    tn = _tile(n, 1024, LANES)
    assert gate_b_col % tn == 0
    gb0 = gate_b_col // tn
    return pl.pallas_call(
        _mm_pw2_body,
        grid=(n // tn, m // tm),
        in_specs=[pl.BlockSpec((tm, k), lambda j, i: (i, 0)),
                  pl.BlockSpec((k, tn), lambda j, i: (0, j)),
                  pl.BlockSpec((1, tn), lambda j, i: (0, j)),
                  pl.BlockSpec((tm, tn), lambda j, i: (i, j)),
                  pl.BlockSpec((tm, tn), lambda j, i: (i, gb0 + j))],
        out_specs=pl.BlockSpec((tm, tn), lambda j, i: (i, j)),
        out_shape=jax.ShapeDtypeStruct((m, n), BF16),
        scratch_shapes=[pltpu.VMEM((k, tn), BF16)],
        compiler_params=_params("arbitrary", "arbitrary"),
        name="pw2",
    )(x, w, bias.reshape(1, n), ma, proj)


def _bf16_terms(x):
    hi = x.astype(BF16).astype(F32)
    r = x - hi
    mid = r.astype(BF16).astype(F32)
    return hi, mid, r - mid


def _gate_epilogue(o, gon, og, ga):
    ya = _rms(o, gon) * _silu(og)
    return (_sigmoid(ga) * ya).astype(BF16)


def _hgrn_prompt_body(q_ref, f_ref, i_ref, og_ref, ga_ref, lb_ref, gon_ref, ma_ref, so_ref,
                      st_ref, b_scr, k_scr, *, chunk, n_sub, hb):
    c_idx = pl.program_id(2)
    C = chunk

    @pl.when(c_idx == 0)
    def _():
        st_ref[...] = jnp.zeros_like(st_ref)

    row_i = lax.broadcasted_iota(jnp.int32, (C, C), 0)
    col_i = lax.broadcasted_iota(jnp.int32, (C, C), 1)
    causal = row_i >= col_i
    tri = causal.astype(BF16)

    lb = lb_ref[...]
    f = _forget_gate(f_ref[...], lb)
    log_f = jnp.log(f)
    k_scr[...] = 1.0 - f
    spread = jnp.zeros((1, hb * LANES), F32)
    for c in range(n_sub):
        rows = slice(c * C, (c + 1) * C)
        b = sum(jnp.dot(tri, term.astype(BF16), preferred_element_type=F32)
                for term in _bf16_terms(log_f[rows]))
        b_scr[rows, :] = b
        b_mid = b[C // 2 - 1:C // 2, :]
        spread = jnp.maximum(spread, jnp.maximum(b[0:1, :] - b_mid, b_mid - b[C - 1:C, :]))
    factored_ok = jnp.max(spread) < FACTORED_DECAY_MAX_SPREAD

    def pairwise_scores(q, c, sl):
        rows = slice(c * C, (c + 1) * C)
        b = b_scr[rows, sl]
        t_i = lax.broadcasted_iota(jnp.int32, (C, 1), 0)
        s_i = lax.broadcasted_iota(jnp.int32, (1, C), 1)

        def column_group(g, acc):
            grp = pl.ds(pl.multiple_of(c * C + g * SUBLANES, SUBLANES), SUBLANES)
            b_g = b_scr[grp, sl]
            k_g = k_scr[grp, sl]
            for r in range(SUBLANES):
                s = g * SUBLANES + r
                decay = jnp.exp(jnp.where(t_i >= s, b - b_g[r:r + 1], -jnp.inf))
                col = jnp.sum(q * k_g[r:r + 1] * decay, axis=-1, keepdims=True)
                acc = jnp.where(s_i == s, col, acc)
            return acc

        return lax.fori_loop(0, C // SUBLANES, column_group, jnp.zeros((C, C), F32))

    nt = (((1,), (1,)), ((), ()))

    def run(factored):
        for hh in range(hb):
            sl = slice(hh * LANES, (hh + 1) * LANES)
            st = st_ref[hh]
            for c in range(n_sub):
                rows = slice(c * C, (c + 1) * C)
                q = q_ref[rows, sl]
                b = b_scr[rows, sl]
                k = k_scr[rows, sl]
                b_last = b[C - 1:C, :]
                if factored:
                    b_mid = b[C // 2 - 1:C // 2, :]
                    q_dec = q * jnp.exp(b - b_mid)
                    k_dec = k * jnp.exp(b_mid - b)
                    scores = lax.dot_general(q_dec.astype(BF16), k_dec.astype(BF16), nt,
                                             preferred_element_type=F32)
                    scores = jnp.where(causal, scores, 0.0)
                    qe = q_dec * jnp.exp(b_mid)
                    kl = k_dec * jnp.exp(b_last - b_mid)
                else:
                    scores = pairwise_scores(q, c, sl)
                    qe = q * jnp.exp(b)
                    kl = k * jnp.exp(b_last - b)
                v = i_ref[rows, sl]
                intra = jnp.dot(scores.astype(BF16), v.astype(BF16), preferred_element_type=F32)
                inter = lax.dot_general(qe.astype(BF16), st.astype(BF16), nt, preferred_element_type=F32)
                st = st * jnp.exp(b_last) + jnp.dot(v.T.astype(BF16), kl.astype(BF16),
                                                    preferred_element_type=F32)
                ma_ref[rows, sl] = _gate_epilogue(inter + intra, gon_ref[:, sl],
                                                  og_ref[rows, sl], ga_ref[rows, sl])
            st_ref[hh] = st

    @pl.when(factored_ok)
    def _():
        run(True)

    @pl.when(jnp.logical_not(factored_ok))
    def _():
        run(False)

    @pl.when(c_idx == pl.num_programs(2) - 1)
    def _():
        for hh in range(hb):
            so_ref[hh] = st_ref[hh].T


def _hgrn_prompt_call(proj, lb, g_onorm, bsz, t, n_heads, d_conv, n_rows):
    d = n_heads * LANES
    chunk = _tile(t, 128, LANES)
    n_sub = 2 if t % (2 * chunk) == 0 else 1
    tc = chunk * n_sub
    hb = _tile(n_heads, 16, 1)
    w = hb * LANES
    hcols = n_heads // hb
    ga0 = (4 * d + 2 * d_conv) // w
    per_b = t // tc

    def col(base):
        return pl.BlockSpec((tc, w), lambda b, h, c: (b * per_b + c, base + h))

    body = functools.partial(_hgrn_prompt_body, chunk=chunk, n_sub=n_sub, hb=hb)
    return pl.pallas_call(
        body,
        grid=(bsz, hcols, per_b),
        in_specs=[col(0), col(hcols), col(2 * hcols), col(3 * hcols), col(ga0),
                  pl.BlockSpec((1, w), lambda b, h, c: (0, h)),
                  pl.BlockSpec((1, w), lambda b, h, c: (0, h))],
        out_specs=[pl.BlockSpec((tc, w), lambda b, h, c: (b * per_b + c, h)),
                   pl.BlockSpec((None, hb, LANES, LANES), lambda b, h, c: (b, h, 0, 0))],
        out_shape=[jax.ShapeDtypeStruct((n_rows, d), BF16),
                   jax.ShapeDtypeStruct((bsz, n_heads, LANES, LANES), F32)],
        scratch_shapes=[pltpu.VMEM((hb, LANES, LANES), F32)] + [pltpu.VMEM((tc, w), F32)] * 2,
        compiler_params=_params("arbitrary", "arbitrary", "arbitrary"),
        name="hgrn_prompt",
    )(proj, proj, proj, proj, proj, lb, g_onorm)


def _hgrn_sample_body(q_ref, f_ref, i_ref, og_ref, ga_ref, lb_ref, gon_ref, s_ref, ma_in_ref,
                      ma_ref, so_ref, *, t_s, n_heads):
    del ma_in_ref
    nb = SUBLANES // t_s
    row = lax.broadcasted_iota(jnp.int32, (SUBLANES, 1), 0)
    tok = row % t_s
    contract0 = (((0,), (0,)), ((), ()))

    def head(h, carry):
        sl = pl.ds(pl.multiple_of(h * LANES, LANES), LANES)
        q = q_ref[:, sl]
        v = i_ref[:, sl]
        lb = lb_ref[:, sl]
        f = _forget_gate(f_ref[:, sl], lb)
        log_f = jnp.log(f)
        k = 1.0 - f
        b = log_f
        for d in range(1, t_s):
            b = b + jnp.where(tok >= d, pltpu.roll(log_f, d, 0), 0.0)
        o = jnp.zeros((SUBLANES, LANES), F32)
        for d in range(t_s):
            kd, bd, vd = (k, b, v) if d == 0 else (pltpu.roll(k, d, 0), pltpu.roll(b, d, 0),
                                                  pltpu.roll(v, d, 0))
            decay = jnp.exp(jnp.where(tok >= d, b - bd, -jnp.inf))
            score = jnp.sum(q * kd * decay, axis=-1, keepdims=True)
            o = o + score * jnp.where(tok >= d, vd, 0.0)
        qe = (q * jnp.exp(b)).astype(BF16)
        for bi in range(nb):
            in_b = (row >= bi * t_s) & (row < (bi + 1) * t_s)
            s0 = s_ref[bi, h]
            inter = jnp.dot(qe, s0.astype(BF16), preferred_element_type=F32)
            o = o + jnp.where(in_b, inter, 0.0)
            b_last = b[(bi + 1) * t_s - 1:(bi + 1) * t_s, :]
            kk = jnp.where(in_b, k * jnp.exp(jnp.where(in_b, b_last - b, 0.0)), 0.0)
            vv = jnp.where(in_b, v, 0.0)
            upd = lax.dot_general(kk.astype(BF16), vv.astype(BF16), contract0,
                                  preferred_element_type=F32)
            hi, mid, lo = _bf16_terms(jnp.exp(b_last))
            d_rows = jnp.where(row == 0, hi, jnp.where(row == 1, mid, jnp.where(row == 2, lo, 0.0)))
            one_rows = jnp.where(row < 3, jnp.ones((SUBLANES, LANES), F32), 0.0)
            decay_kv = lax.dot_general(d_rows.astype(BF16), one_rows.astype(BF16), contract0,
                                       preferred_element_type=F32)
            so_ref[bi, h] = s0 * decay_kv + upd
        ma_ref[:, sl] = _gate_epilogue(o, gon_ref[:, sl], og_ref[:, sl], ga_ref[:, sl])
        return carry

    lax.fori_loop(0, n_heads, head, 0, unroll=_tile(n_heads, 8, 1))


def _hgrn_sample_call(proj, lb, g_onorm, state, ma, row0, t_s, d_conv):
    bs, n_heads = state.shape[0], state.shape[1]
    d = n_heads * LANES
    assert SUBLANES % t_s == 0 and row0 % SUBLANES == 0
    nb = SUBLANES // t_s
    r0 = row0 // SUBLANES
    ga0 = (4 * d + 2 * d_conv) // d

    def col(base):
        return pl.BlockSpec((SUBLANES, d), lambda i: (r0 + i, base))

    body = functools.partial(_hgrn_sample_body, t_s=t_s, n_heads=n_heads)
    return pl.pallas_call(
        body,
        grid=(bs // nb,),
        in_specs=[col(0), col(1), col(2), col(3), col(ga0),
                  pl.BlockSpec((1, d), lambda i: (0, 0)),
                  pl.BlockSpec((1, d), lambda i: (0, 0)),
                  pl.BlockSpec((nb, n_heads, LANES, LANES), lambda i: (i, 0, 0, 0)),
                  pl.BlockSpec(memory_space=pl.ANY)],
        out_specs=[pl.BlockSpec((SUBLANES, d), lambda i: (r0 + i, 0)),
                   pl.BlockSpec((nb, n_heads, LANES, LANES), lambda i: (i, 0, 0, 0))],
        out_shape=[jax.ShapeDtypeStruct(ma.shape, ma.dtype),
                   jax.ShapeDtypeStruct(state.shape, F32)],
        input_output_aliases={8: 0},
        compiler_params=_params("arbitrary"),
        name="hgrn_sample",
    )(proj, proj, proj, proj, proj, lb, g_onorm, state, ma)


def _ln_swish(x, g, b):
    mu = jnp.mean(x, axis=-1, keepdims=True)
    xc = x - mu
    var = jnp.mean(xc * xc, axis=-1, keepdims=True)
    return _silu(xc * lax.rsqrt(var + EPS) * g + b)


def _conv_prompt_body(ga_ref, gb_ref, w_ref, bdw_ref, lg_ref, lbias_ref, o_ref, cache_ref,
                      ext_ref, conv_ref, *, width, hist, rc):
    t_idx = pl.program_id(1)
    tt = ga_ref.shape[0]
    dc = ga_ref.shape[1]

    @pl.when(t_idx == 0)
    def _():
        ext_ref[0:hist, :] = jnp.zeros((hist, dc), F32)

    @pl.when(t_idx > 0)
    def _():
        ext_ref[0:hist, :] = ext_ref[tt:tt + hist, :]

    ext_ref[hist:hist + tt, :] = ga_ref[...] * _sigmoid(gb_ref[...])
    off = hist - (width - 1)

    def strip(l, carry):
        lanes = pl.ds(pl.multiple_of(l * LANES, LANES), LANES)
        for r0 in range(0, tt, rc):
            win = ext_ref[r0:r0 + rc + hist, lanes]
            acc = jnp.zeros((rc, LANES), F32) + bdw_ref[:, lanes]
            for s in range(SUBLANES):
                taps = [j for j in range(width) if (off + j) % SUBLANES == s]
                if not taps:
                    continue
                shifted = win if s == 0 else pltpu.roll(win, rc + hist - s, 0)
                for j in taps:
                    a = (off + j) // SUBLANES * SUBLANES
                    acc = acc + w_ref[j:j + 1, lanes] * shifted[a:a + rc]
            conv_ref[r0:r0 + rc, lanes] = acc
        return carry

    lax.fori_loop(0, dc // LANES, strip, 0)

    def norm(i, carry):
        rows = pl.ds(pl.multiple_of(i * rc, rc), rc)
        o_ref[rows, :] = _ln_swish(conv_ref[rows, :], lg_ref[...], lbias_ref[...]).astype(o_ref.dtype)
        return carry

    lax.fori_loop(0, tt // rc, norm, 0, unroll=2)

    @pl.when(t_idx == pl.num_programs(1) - 1)
    def _():
        cache_ref[...] = ext_ref[hist + tt - (width - 1):hist + tt, :]


def _conv_prompt_call(proj, w_dw, b_dw, ln_g, ln_b, bsz, t, d_hgrn, n_rows):
    width, dc = w_dw.shape
    tt = _tile(t, 256, 32)
    hist = 32
    assert width - 1 <= hist <= tt and (4 * d_hgrn) % dc == 0
    c0 = 4 * d_hgrn // dc
    per_b = t // tt
    vec = lambda a: a.reshape(1, dc)
    body = functools.partial(_conv_prompt_body, width=width, hist=hist, rc=32)
    return pl.pallas_call(
        body,
        grid=(bsz, per_b),
        in_specs=[pl.BlockSpec((tt, dc), lambda b, i: (b * per_b + i, c0)),
                  pl.BlockSpec((tt, dc), lambda b, i: (b * per_b + i, c0 + 1)),
                  pl.BlockSpec((width, dc), lambda b, i: (0, 0)),
                  pl.BlockSpec((1, dc), lambda b, i: (0, 0)),
                  pl.BlockSpec((1, dc), lambda b, i: (0, 0)),
                  pl.BlockSpec((1, dc), lambda b, i: (0, 0))],
        out_specs=[pl.BlockSpec((tt, dc), lambda b, i: (b * per_b + i, 0)),
                   pl.BlockSpec((None, width - 1, dc), lambda b, i: (b, 0, 0))],
        out_shape=[jax.ShapeDtypeStruct((n_rows, dc), BF16),
                   jax.ShapeDtypeStruct((bsz, width - 1, dc), F32)],
        scratch_shapes=[pltpu.VMEM((hist + tt, dc), F32), pltpu.VMEM((tt, dc), F32)],
        compiler_params=_params("arbitrary", "arbitrary"),
        name="conv_prompt",
    )(proj, proj, w_dw, vec(b_dw), vec(ln_g), vec(ln_b))


def _conv_sample_body(ga_ref, gb_ref, cache_ref, w_ref, bdw_ref, lg_ref, lbias_ref, vc_in_ref,
                      o_ref, ncache_ref, ext_ref, *, width, t_s, bb):
    del vc_in_ref
    u = ga_ref[...] * _sigmoid(gb_ref[...])
    for bi in range(bb):
        ext_ref[0:width - 1, :] = cache_ref[bi]
        ext_ref[width - 1:width - 1 + t_s, :] = u[bi * t_s:(bi + 1) * t_s]
        acc = jnp.zeros((t_s, u.shape[1]), F32) + bdw_ref[...]
        for j in range(width):
            acc = acc + w_ref[j:j + 1, :] * ext_ref[j:j + t_s, :]
        ncache_ref[bi] = ext_ref[t_s:t_s + width - 1, :]
        o_ref[bi * t_s:(bi + 1) * t_s, :] = _ln_swish(acc, lg_ref[...], lbias_ref[...]).astype(o_ref.dtype)


def _conv_sample_call(proj, cache, w_dw, b_dw, ln_g, ln_b, vconv, row0, t_s, d_hgrn):
    width, dc = w_dw.shape
    bs = cache.shape[0]
    bb = _tile(bs, 8, 1)
    rows = bb * t_s
    assert rows % 16 == 0 and row0 % rows == 0
    r0 = row0 // rows
    c0 = 4 * d_hgrn // dc
    vec = lambda a: a.reshape(1, dc)
    body = functools.partial(_conv_sample_body, width=width, t_s=t_s, bb=bb)
    return pl.pallas_call(
        body,
        grid=(bs // bb,),
        in_specs=[pl.BlockSpec((rows, dc), lambda i: (r0 + i, c0)),
                  pl.BlockSpec((rows, dc), lambda i: (r0 + i, c0 + 1)),
                  pl.BlockSpec((bb, width - 1, dc), lambda i: (i, 0, 0)),
                  pl.BlockSpec((width, dc), lambda i: (0, 0)),
                  pl.BlockSpec((1, dc), lambda i: (0, 0)),
                  pl.BlockSpec((1, dc), lambda i: (0, 0)),
                  pl.BlockSpec((1, dc), lambda i: (0, 0)),
                  pl.BlockSpec(memory_space=pl.ANY)],
        out_specs=[pl.BlockSpec((rows, dc), lambda i: (r0 + i, 0)),
                   pl.BlockSpec((bb, width - 1, dc), lambda i: (i, 0, 0))],
        out_shape=[jax.ShapeDtypeStruct(vconv.shape, vconv.dtype),
                   jax.ShapeDtypeStruct(cache.shape, F32)],
        scratch_shapes=[pltpu.VMEM((width - 1 + t_s + SUBLANES, dc), F32)],
        input_output_aliases={7: 0},
        compiler_params=_params("arbitrary"),
        name="conv_sample",
    )(proj, proj, cache, w_dw, vec(b_dw), vec(ln_g), vec(ln_b), vconv)


def _expert_changed(be_ref, r):
    return jnp.logical_or(r == 0, be_ref[r] != be_ref[jnp.maximum(r - 1, 0)])


def _weight_copies(w_hbms, buf_refs, sem_ref, e, j, slot):
    copies = []
    for w_hbm, buf_ref in zip(w_hbms, buf_refs):
        tcol = buf_ref.shape[-1]
        cols = pl.ds(pl.multiple_of(j * tcol, tcol), tcol)
        copies.append(pltpu.make_async_copy(w_hbm.at[e, :, cols], buf_ref.at[slot], sem_ref.at[slot]))
    return copies


def _stage_expert_weights(be_ref, nr_ref, nx_ref, w_hbms, buf_refs, bf16_refs, sem_ref, cnt_ref):
    j, r = pl.program_id(0), pl.program_id(1)

    @pl.when(jnp.logical_and(j == 0, r == 0))
    def _():
        cnt_ref[0] = 0
        for c in _weight_copies(w_hbms, buf_refs, sem_ref, be_ref[0], 0, 0):
            c.start()

    @pl.when(jnp.logical_and(r < nr_ref[0], _expert_changed(be_ref, r)))
    def _():
        slot = cnt_ref[0] % 2
        for c in _weight_copies(w_hbms, buf_refs, sem_ref, be_ref[r], j, slot):
            c.wait()
        wraps = nx_ref[r] >= nr_ref[0]
        e_next = be_ref[jnp.where(wraps, 0, nx_ref[r])]
        j_next = jnp.where(wraps, j + 1, j)

        @pl.when(j_next < pl.num_programs(0))
        def _():
            for c in _weight_copies(w_hbms, buf_refs, sem_ref, e_next, j_next, 1 - slot):
                c.start()

        for buf_ref, dst_ref in zip(buf_refs, bf16_refs):
            dst_ref[...] = buf_ref[slot].astype(BF16)
        cnt_ref[0] = cnt_ref[0] + 1


def _moe_up_body(be_ref, nr_ref, nx_ref, x_ref, w1_hbm, w3_hbm, g_ref,
                 w1f_ref, w3f_ref, w1b_ref, w3b_ref, sem_ref, cnt_ref):
    _stage_expert_weights(be_ref, nr_ref, nx_ref, (w1_hbm, w3_hbm), (w1f_ref, w3f_ref),
                          (w1b_ref, w3b_ref), sem_ref, cnt_ref)

    @pl.when(pl.program_id(1) < nr_ref[0])
    def _():
        x = x_ref[...]
        a = jnp.dot(x, w1b_ref[...], preferred_element_type=F32)
        b = jnp.dot(x, w3b_ref[...], preferred_element_type=F32)
        g_ref[...] = (_silu(a) * b).astype(g_ref.dtype)


def _moe_down_body(be_ref, nr_ref, nx_ref, g_ref, w2_hbm, y_ref, w2f_ref, w2b_ref, sem_ref, cnt_ref):
    _stage_expert_weights(be_ref, nr_ref, nx_ref, (w2_hbm,), (w2f_ref,), (w2b_ref,), sem_ref, cnt_ref)

    @pl.when(pl.program_id(1) < nr_ref[0])
    def _():
        y_ref[...] = _pack_bf16_halves(jnp.dot(g_ref[...], w2b_ref[...], preferred_element_type=F32))


def _dispatch_body(tok_ref, nr_ref, cnt_ref, first_ref, h2_hbm, o_ref, buf_ref, sem_ref):
    r = pl.program_id(0)
    tm = o_ref.shape[0]

    def start(step, slot):
        def copy_row(i, carry):
            pltpu.make_async_copy(h2_hbm.at[pl.ds(tok_ref[first_ref[step] + i], 1)],
                                  buf_ref.at[slot, pl.ds(i, 1)], sem_ref.at[slot]).start()
            return carry

        @pl.when(cnt_ref[step] == tm)
        def _():
            lax.fori_loop(0, tm, copy_row, 0, unroll=8)

        @pl.when(cnt_ref[step] < tm)
        def _():
            lax.fori_loop(0, cnt_ref[step], copy_row, 0)

    @pl.when(r == 0)
    def _():
        buf_ref[...] = jnp.zeros_like(buf_ref)
        start(0, 0)

    @pl.when(r + 1 < nr_ref[0])
    def _():
        start(r + 1, (r + 1) % 2)

    @pl.when(r < nr_ref[0])
    def _():
        slot = r % 2
        rows = pl.ds(0, pl.multiple_of(cnt_ref[r], ROW_GROUP))
        pltpu.make_async_copy(h2_hbm.at[rows], buf_ref.at[slot, rows], sem_ref.at[slot]).wait()
        lo, hi = _unpack_bf16_halves(buf_ref[slot])
        half = lo.shape[1]
        o_ref[:, 0:half] = lo.astype(o_ref.dtype)
        o_ref[:, half:] = hi.astype(o_ref.dtype)


def _dispatch_call(h2p, sorted_tok, n_real, block_cnt, block_first, tm):
    half = h2p.shape[1]
    d = 2 * half
    nblk = block_cnt.shape[0]
    return pl.pallas_call(
        _dispatch_body,
        grid_spec=pltpu.PrefetchScalarGridSpec(
            num_scalar_prefetch=4,
            grid=(nblk,),
            in_specs=[pl.BlockSpec(memory_space=pl.ANY)],
            out_specs=pl.BlockSpec((tm, d), lambda r, o, nr, cnt, first: (jnp.minimum(r, nr[0] - 1), 0)),
            scratch_shapes=[pltpu.VMEM((2, tm, half), jnp.uint32), pltpu.SemaphoreType.DMA((2,))]),
        out_shape=jax.ShapeDtypeStruct((nblk * tm, d), BF16),
        compiler_params=_params("arbitrary"),
        name="moe_dispatch",
    )(sorted_tok, n_real, block_cnt, block_first, h2p)


def _moe_experts(x_sorted, block_e, n_real, next_run, w1, w3, w2, tm):
    l, d = x_sorted.shape
    n_exp, _, dh = w1.shape
    nblk = l // tm
    th = _tile(dh, 512, LANES)
    rr = lambda r, nr: jnp.minimum(r, nr[0] - 1)
    hbm = pl.BlockSpec(memory_space=pl.ANY)

    g = pl.pallas_call(
        _moe_up_body,
        grid_spec=pltpu.PrefetchScalarGridSpec(
            num_scalar_prefetch=3,
            grid=(dh // th, nblk),
            in_specs=[pl.BlockSpec((tm, d), lambda j, r, be, nr, nx: (rr(r, nr), 0)), hbm, hbm],
            out_specs=pl.BlockSpec((tm, th), lambda j, r, be, nr, nx: (rr(r, nr), j)),
            scratch_shapes=[pltpu.VMEM((2, d, th), F32), pltpu.VMEM((2, d, th), F32),
                            pltpu.VMEM((d, th), BF16), pltpu.VMEM((d, th), BF16),
                            pltpu.SemaphoreType.DMA((2,)), pltpu.SMEM((1,), jnp.int32)]),
        out_shape=jax.ShapeDtypeStruct((l, dh), BF16),
        compiler_params=_params("arbitrary", "arbitrary"),
        name="moe_up",
    )(block_e, n_real, next_run, x_sorted, w1, w3)

    return pl.pallas_call(
        _moe_down_body,
        grid_spec=pltpu.PrefetchScalarGridSpec(
            num_scalar_prefetch=3,
            grid=(1, nblk),
            in_specs=[pl.BlockSpec((tm, dh), lambda j, r, be, nr, nx: (rr(r, nr), 0)), hbm],
            out_specs=pl.BlockSpec((tm, d // 2), lambda j, r, be, nr, nx: (rr(r, nr), 0)),
            scratch_shapes=[pltpu.VMEM((2, dh, d), F32), pltpu.VMEM((dh, d), BF16),
                            pltpu.SemaphoreType.DMA((2,)), pltpu.SMEM((1,), jnp.int32)]),
        out_shape=jax.ShapeDtypeStruct((l, d // 2), jnp.uint32),
        compiler_params=_params("arbitrary", "arbitrary"),
        name="moe_down",
    )(block_e, n_real, next_run, g, w2)


def _route(logits, n_groups, n_experts, tm):
    n = logits.shape[0]
    eg = n_experts // n_groups
    l1 = logits[:, :n_groups]
    p1 = jax.nn.softmax(l1, axis=-1)
    grp = jnp.argmax(l1, axis=-1)
    p_grp = jnp.max(p1, axis=-1, keepdims=True)
    l2 = logits[:, n_groups:n_groups + n_experts].reshape(n, n_groups, eg)
    l2g = jnp.take_along_axis(l2, grp[:, None, None], axis=1)[:, 0]
    top_v, top_i = lax.top_k(l2g, TOP_K)
    weights = p_grp * jax.nn.softmax(top_v, axis=-1)
    expert = (grp[:, None] * eg + top_i).astype(jnp.int32)

    a = n * TOP_K
    i32 = lambda v: v.astype(jnp.int32)
    e = expert.reshape(-1)
    counts = i32(jnp.bincount(e, length=n_experts))
    padded = (counts + tm - 1) // tm * tm
    ends = jnp.cumsum(padded)
    starts = ends - padded
    seg_start = jnp.cumsum(counts) - counts
    order = jnp.argsort(e)
    rank = jnp.argsort(order)
    pos = i32(starts[e] + rank - seg_start[e]).reshape(n, TOP_K)
    nblk = -(-a // tm) + n_experts
    block_e = i32(jnp.minimum(jnp.searchsorted(ends, jnp.arange(nblk) * tm, side='right'),
                              n_experts - 1))
    n_real = i32(ends[-1:] // tm)
    block_cnt = jnp.clip(counts[block_e] - (jnp.arange(nblk) * tm - starts[block_e]), 0, tm)
    block_cnt = i32(-(-block_cnt // ROW_GROUP) * ROW_GROUP)
    next_run = i32(ends[block_e] // tm)
    block_first = i32(jnp.minimum(seg_start[block_e] + jnp.arange(nblk) * tm - starts[block_e], a))
    sorted_tok = jnp.pad(i32(order // TOP_K), (0, ROW_GROUP))
    return weights, sorted_tok, pos, block_e, n_real, block_cnt, block_first, next_run


def kernel(x_prompt, x_sample, state_hgrn, cache_conv, c_prompt, c_sample, g_mix, w_ada, b_ada, w_in, lb_param, g_onorm, w_dw, b_dw, ln_g, ln_b, w_pw2, b_pw2, w_out, g_ffn, w_r1, b_r1, w_r2, b_r2, moe_w1, moe_w3, moe_w2, g_final):
    bp, t_p, d = x_prompt.shape
    bs, t_s, _ = x_sample.shape
    depth, _, n_heads, head_k, head_v = state_hgrn.shape
    assert depth == 1 and head_k == LANES and head_v == LANES
    d_hgrn = n_heads * head_k
    d_conv = w_dw.shape[-1]
    n_groups, n_experts = w_r1.shape[-1], w_r2.shape[-1]
    np_, ns = bp * t_p, bs * t_s
    n = np_ + ns
    rows = _Rows(np_, t_p, ns, t_s, d)
    vec = lambda a: a.reshape(1, -1)

    assert bs % SUBLANES == 0
    c_all = jnp.concatenate([c_sample, c_prompt], axis=0)
    c_all = jnp.pad(c_all, ((0, -c_all.shape[0] % 16), (0, 0)))
    ada = _ada_call(c_all, w_ada[0], b_ada[0], bs, t_s, bp)
    ada_p = ada_s = ada
    SH_M, SC_M, GT_M, SH_F, SC_F, GT_F = range(6)
    mod = lambda which: (lambda sample: rows.mod(sample, which))
    ada_of = lambda sample: ada
    xs = (x_prompt.reshape(np_, d), x_sample.reshape(ns, d))
    x_spec = lambda sample: rows.rows(sample, d, offset=False)
    flat = lambda width: (lambda sample: rows.rows(sample, width))
    const = lambda shape: (lambda sample: rows.const(shape))

    (h,) = _rowwise(
        _norm1_body, rows, "norm1",
        [xs[0], vec(g_mix[0]), ada_p, ada_p], [xs[1], vec(g_mix[0]), ada_s, ada_s],
        [x_spec, const((1, d)), mod(SC_M), mod(SH_M)], [d], [BF16])
    proj = _mm_call(h, w_in[0], F32, 544, 1024, "in_proj")

    lb = jax.nn.softmax(lb_param.astype(F32), axis=0)[0:1]
    ma, state_p = _hgrn_prompt_call(proj, lb, vec(g_onorm[0]), bp, t_p, n_heads, d_conv, n)
    ma, state_s = _hgrn_sample_call(proj, lb, vec(g_onorm[0]), state_hgrn[0], ma, np_, t_s, d_conv)

    vconv, cache_p = _conv_prompt_call(proj, w_dw[0], b_dw[0], ln_g[0], ln_b[0], bp, t_p, d_hgrn, n)
    vconv, cache_s = _conv_sample_call(proj, cache_conv[0], w_dw[0], b_dw[0], ln_g[0], ln_b[0],
                                       vconv, np_, t_s, d_hgrn)

    merged = _mm_pw2_call(vconv, w_pw2[0], b_pw2[0], ma, proj, 4 * d_hgrn + 2 * d_conv + d)
    attn = _mm_call(merged, w_out[0], F32, 544, 1024, "out_proj")

    n_r = n_groups + n_experts
    n_r_pad = -(-n_r // LANES) * LANES
    w_r = jnp.concatenate([w_r1[0], w_r2[0], jnp.zeros((d, n_r_pad - n_r), F32)], axis=1)
    b_r = jnp.concatenate([b_r1[0], b_r2[0], jnp.zeros((n_r_pad - n_r,), F32)]).reshape(1, n_r_pad)
    x1, h2p, logits = _rowwise(
        _norm2_body, rows, "norm2",
        [xs[0], attn, ada_p, vec(g_ffn[0]), ada_p, ada_p, w_r, b_r],
        [xs[1], attn, ada_s, vec(g_ffn[0]), ada_s, ada_s, w_r, b_r],
        [x_spec, flat(d), mod(GT_M), const((1, d)), mod(SC_F), mod(SH_F),
         const((d, n_r_pad)), const((1, n_r_pad))],
        [d, d // 2, n_r_pad], [F32, jnp.uint32, F32])

    tm = 256
    weights, sorted_tok, pos, block_e, n_real, block_cnt, block_first, next_run = _route(
        logits, n_groups, n_experts, tm)
    x_sorted = _dispatch_call(h2p, sorted_tok, n_real, block_cnt, block_first, tm)
    y_sorted = _moe_experts(x_sorted, block_e, n_real, next_run, moe_w1[0], moe_w3[0], moe_w2[0], tm)
    pos_flat = pos.T.reshape(-1)

    outs = []
    for sample in (False, True):
        tt = rows.tt_s if sample else rows.tt_p
        wrap = lambda spec: pl.BlockSpec(spec.block_shape, lambda i, p, f=spec.index_map: f(i))
        outs.append(pl.pallas_call(
            functools.partial(_final_body, row0=np_ if sample else 0),
            grid_spec=pltpu.PrefetchScalarGridSpec(
                num_scalar_prefetch=1,
                grid=rows.grid(sample),
                in_specs=[wrap(rows.rows(sample, d)), wrap(rows.rows(sample, TOP_K)),
                          wrap(rows.mod(sample, GT_F)), wrap(rows.const((1, d))),
                          pl.BlockSpec(memory_space=pl.ANY)],
                out_specs=wrap(rows.rows(sample, d, offset=False)),
                scratch_shapes=[pltpu.VMEM((2, TOP_K, tt, d // 2), jnp.uint32),
                                pltpu.SemaphoreType.DMA((2,))]),
            out_shape=jax.ShapeDtypeStruct((ns if sample else np_, d), F32),
            compiler_params=_params("arbitrary"),
            name="final_sample" if sample else "final_prompt",
        )(pos_flat, x1, weights, ada_of(sample), vec(g_final), y_sorted))
    y_prompt = outs[0].reshape(bp, t_p, d)
    y_sample = outs[1].reshape(bs, t_s, d)
    return (y_prompt, y_sample, state_p[None], cache_p[None], state_s[None], cache_s[None])
```

```python
import functools

import jax
import jax.numpy as jnp
from jax import lax
from jax.experimental import pallas as pl
from jax.experimental.pallas import tpu as pltpu

F32 = jnp.float32
BF16 = jnp.bfloat16
EPS = 1e-6
TOP_K = 2
LANES = 128
SUBLANES = 8
ROW_GROUP = SUBLANES
VMEM_LIMIT = 60 * 1024 * 1024
FACTORED_DECAY_MAX_SPREAD = 80.0


def _params(*sem):
    return pltpu.CompilerParams(dimension_semantics=sem, vmem_limit_bytes=VMEM_LIMIT)


def _tile(n, target, mult):
    best = None
    for t in range(mult, min(n, target) + 1, mult):
        if n % t == 0:
            best = t
    assert best is not None, (n, target, mult)
    return best


def _sigmoid(x):
    return 0.5 * jnp.tanh(0.5 * x) + 0.5


def _forget_gate(f_logit, lb):
    return lb + (1.0 - lb) / (1.0 + jnp.exp(-f_logit))


def _silu(x):
    return x * _sigmoid(x)


def _ada_body(c_ref, w_ref, b_ref, o_ref, *, bs, t_s, bp):
    s = _silu(c_ref[...]).astype(BF16)
    ada = jnp.dot(s, w_ref[...].astype(BF16), preferred_element_type=F32) + b_ref[...]
    ns = bs * t_s
    r_i = lax.broadcasted_iota(jnp.int32, (ns, bs), 0)
    b_i = lax.broadcasted_iota(jnp.int32, (ns, bs), 1)
    lo = b_i * t_s
    expand = jnp.where((r_i >= lo) & (r_i < lo + t_s), 1.0, 0.0).astype(BF16)
    o_ref[0:ns, :] = sum(jnp.dot(expand, term.astype(BF16), preferred_element_type=F32)
                         for term in _bf16_terms(ada[0:bs]))
    for b in range(bp):
        o_ref[ns + b * SUBLANES:ns + (b + 1) * SUBLANES, :] = jnp.broadcast_to(
            ada[bs + b:bs + b + 1], (SUBLANES, ada.shape[1]))


def _ada_call(c_all, w_ada, b_ada, bs, t_s, bp):
    rows, d = c_all.shape
    n6 = w_ada.shape[1]
    tn = _tile(n6, 1024, LANES)
    out_rows = bs * t_s + bp * SUBLANES
    return pl.pallas_call(
        functools.partial(_ada_body, bs=bs, t_s=t_s, bp=bp),
        grid=(n6 // tn,),
        in_specs=[pl.BlockSpec((rows, d), lambda j: (0, 0)),
                  pl.BlockSpec((d, tn), lambda j: (0, j)),
                  pl.BlockSpec((1, tn), lambda j: (0, j))],
        out_specs=pl.BlockSpec((out_rows, tn), lambda j: (0, j)),
        out_shape=jax.ShapeDtypeStruct((out_rows, n6), F32),
        compiler_params=_params("arbitrary"),
        name="ada",
    )(c_all, w_ada, b_ada.reshape(1, n6))


def _rms(x, g):
    return x * lax.rsqrt(jnp.mean(x * x, axis=-1, keepdims=True) + EPS) * g


def _mod(ref, n_rows):
    return ref[...] if ref.shape[0] == n_rows else ref[0:1, :]


def _norm1_body(x_ref, g_ref, sc_ref, sh_ref, *rest):
    o_ref = rest[-1]
    x = x_ref[...]
    tt = x.shape[0]
    y = _rms(x, g_ref[...])
    o_ref[...] = (y * (1.0 + _mod(sc_ref, tt)) + _mod(sh_ref, tt)).astype(o_ref.dtype)


def _pack_bf16_halves(x):
    w = x.shape[1] // 2
    lo = lax.bitcast_convert_type(x[:, :w].astype(BF16).astype(F32), jnp.uint32)
    hi = lax.bitcast_convert_type(x[:, w:].astype(BF16).astype(F32), jnp.uint32)
    return (lo >> 16) | hi


def _unpack_bf16_halves(u):
    lo = lax.bitcast_convert_type(u << 16, F32)
    hi = lax.bitcast_convert_type(u & jnp.uint32(0xFFFF0000), F32)
    return lo, hi


def _norm2_body(x_ref, a_ref, gt_ref, g_ref, sc_ref, sh_ref, wr_ref, br_ref, *rest):
    x1_ref, h2p_ref, lg_ref = rest[-3:]
    tt = x_ref.shape[0]
    x1 = x_ref[...] + _mod(gt_ref, tt) * a_ref[...]
    x1_ref[...] = x1
    h2 = _rms(x1, g_ref[...]) * (1.0 + _mod(sc_ref, tt)) + _mod(sh_ref, tt)
    h2p_ref[...] = _pack_bf16_halves(h2)
    lg_ref[...] = jnp.dot(h2, wr_ref[...], preferred_element_type=F32,
                          precision=lax.Precision.HIGHEST) + br_ref[...]


def _row_gather_wait(n_rows, src_hbm, dst, sem):
    pltpu.make_async_copy(src_hbm.at[pl.ds(0, n_rows)], dst, sem).wait()


def _final_body(pos_ref, x1_ref, w_ref, gt_ref, g_ref, y_hbm, o_ref, buf_ref, sem_ref, *, row0):
    i = pl.program_id(0)
    tt = x1_ref.shape[0]
    n_tok = pos_ref.shape[0] // TOP_K

    def start_rows(step, slot, first, count):
        for k in range(TOP_K):
            base = k * n_tok + row0 + step * tt
            for r in range(count):
                pltpu.make_async_copy(y_hbm.at[pl.ds(pos_ref[base + first + r], 1)],
                                      buf_ref.at[slot, k, pl.ds(first + r, 1)], sem_ref.at[slot]).start()

    @pl.when(i == 0)
    def _():
        def prime(c, carry):
            start_rows(0, 0, c * SUBLANES, SUBLANES)
            return carry

        lax.fori_loop(0, tt // SUBLANES, prime, 0)

    slot = i % 2
    for k in range(TOP_K):
        _row_gather_wait(tt, y_hbm, buf_ref.at[slot, k], sem_ref.at[slot])
    per_row_gate = gt_ref.shape[0] == tt

    half = x1_ref.shape[1] // 2

    def row_group(c, carry, prefetch):
        if prefetch:
            start_rows(i + 1, 1 - slot, c * SUBLANES, SUBLANES)
        rs = pl.ds(pl.multiple_of(c * SUBLANES, SUBLANES), SUBLANES)
        moe_lo = jnp.zeros((SUBLANES, half), F32)
        moe_hi = jnp.zeros((SUBLANES, half), F32)
        for k in range(TOP_K):
            y_lo, y_hi = _unpack_bf16_halves(buf_ref[slot, k, rs, :])
            moe_lo = moe_lo + w_ref[rs, k:k + 1] * y_lo
            moe_hi = moe_hi + w_ref[rs, k:k + 1] * y_hi
        gate = gt_ref[rs, :] if per_row_gate else gt_ref[0:1, :]
        x2_lo = x1_ref[rs, 0:half] + gate[:, 0:half] * moe_lo
        x2_hi = x1_ref[rs, half:] + gate[:, half:] * moe_hi
        ms = (jnp.sum(x2_lo * x2_lo, axis=-1, keepdims=True)
              + jnp.sum(x2_hi * x2_hi, axis=-1, keepdims=True)) / x1_ref.shape[1]
        scale = lax.rsqrt(ms + EPS)
        o_ref[rs, 0:half] = x2_lo * scale * g_ref[:, 0:half]
        o_ref[rs, half:] = x2_hi * scale * g_ref[:, half:]
        return carry

    has_next = i + 1 < pl.num_programs(0)
    unroll = _tile(tt // SUBLANES, 8, 1)

    @pl.when(has_next)
    def _():
        lax.fori_loop(0, tt // SUBLANES, functools.partial(row_group, prefetch=True), 0, unroll=unroll)

    @pl.when(jnp.logical_not(has_next))
    def _():
        lax.fori_loop(0, tt // SUBLANES, functools.partial(row_group, prefetch=False), 0, unroll=unroll)


class _Rows:
    def __init__(self, np_, t_p, ns, t_s, d):
        self.np_, self.t_p, self.ns, self.t_s, self.d = np_, t_p, ns, t_s, d
        self.tt_p = _tile(t_p, 256, 16)
        self.tt_s = _tile(ns, 128, 16)
        assert np_ % self.tt_s == 0
        self.n = np_ + ns

    def grid(self, sample):
        return (self.ns // self.tt_s,) if sample else (self.np_ // self.tt_p,)

    def rows(self, sample, width, offset=True):
        if sample:
            base = self.np_ // self.tt_s if offset else 0
            return pl.BlockSpec((self.tt_s, width), lambda i: (base + i, 0))
        return pl.BlockSpec((self.tt_p, width), lambda i: (i, 0))

    def mod(self, sample, which):
        if sample:
            return pl.BlockSpec((self.tt_s, self.d), lambda i: (i, which))
        per = self.t_p // self.tt_p
        base = self.ns // SUBLANES
        return pl.BlockSpec((SUBLANES, self.d), lambda i: (base + i // per, which))

    def const(self, shape):
        return pl.BlockSpec(shape, lambda i: tuple(0 for _ in shape))


def _rowwise(body, rows, name, prompt_in, sample_in, specs, out_widths, out_dtypes):
    outs = None
    for sample, ops in ((False, prompt_in), (True, sample_in)):
        in_specs = [s(sample) for s in specs]
        ops = list(ops)
        aliases = {}
        if outs is not None:
            for k, o in enumerate(outs):
                aliases[len(ops)] = k
                ops.append(o)
                in_specs.append(pl.BlockSpec(memory_space=pl.ANY))
        outs = pl.pallas_call(
            body,
            grid=rows.grid(sample),
            in_specs=in_specs,
            out_specs=[rows.rows(sample, w) for w in out_widths],
            out_shape=[jax.ShapeDtypeStruct((rows.n, w), dt) for w, dt in zip(out_widths, out_dtypes)],
            input_output_aliases=aliases,
            compiler_params=_params("arbitrary"),
            name=name + ("_sample" if sample else "_prompt"),
        )(*ops)
    return outs


def _mm_body(x_ref, w_ref, o_ref, wb_ref):
    @pl.when(pl.program_id(1) == 0)
    def _():
        wb_ref[...] = w_ref[...].astype(BF16)
    o_ref[...] = jnp.dot(x_ref[...], wb_ref[...], preferred_element_type=F32).astype(o_ref.dtype)


def _mm_pw2_body(x_ref, w_ref, b_ref, ma_ref, gb_ref, o_ref, wb_ref):
    @pl.when(pl.program_id(1) == 0)
    def _():
        wb_ref[...] = w_ref[...].astype(BF16)
    yb = jnp.dot(x_ref[...], wb_ref[...], preferred_element_type=F32) + b_ref[...]
    o_ref[...] = (ma_ref[...].astype(F32) + _sigmoid(gb_ref[...]) * yb).astype(o_ref.dtype)


def _mm_call(x, w, out_dtype, tm_target, tn_target, name):
    m, k = x.shape
    n = w.shape[1]
    tm = _tile(m, tm_target, 16)
    tn = _tile(n, tn_target, LANES)
    return pl.pallas_call(
        _mm_body,
        grid=(n // tn, m // tm),
        in_specs=[pl.BlockSpec((tm, k), lambda j, i: (i, 0)),
                  pl.BlockSpec((k, tn), lambda j, i: (0, j))],
        out_specs=pl.BlockSpec((tm, tn), lambda j, i: (i, j)),
        out_shape=jax.ShapeDtypeStruct((m, n), out_dtype),
        scratch_shapes=[pltpu.VMEM((k, tn), BF16)],
        compiler_params=_params("arbitrary", "arbitrary"),
        name=name,
    )(x, w)


def _mm_pw2_call(x, w, bias, ma, proj, gate_b_col):
    m, k = x.shape
    n = w.shape[1]
    tm = _tile(m, 544, 16)
    tn = _tile(n, 1024, LANES)
    assert gate_b_col % tn == 0
    gb0 = gate_b_col // tn
    return pl.pallas_call(
        _mm_pw2_body,
        grid=(n // tn, m // tm),
        in_specs=[pl.BlockSpec((tm, k), lambda j, i: (i, 0)),
                  pl.BlockSpec((k, tn), lambda j, i: (0, j)),
                  pl.BlockSpec((1, tn), lambda j, i: (0, j)),
                  pl.BlockSpec((tm, tn), lambda j, i: (i, j)),
                  pl.BlockSpec((tm, tn), lambda j, i: (i, gb0 + j))],
        out_specs=pl.BlockSpec((tm, tn), lambda j, i: (i, j)),
        out_shape=jax.ShapeDtypeStruct((m, n), BF16),
        scratch_shapes=[pltpu.VMEM((k, tn), BF16)],
        compiler_params=_params("arbitrary", "arbitrary"),
        name="pw2",
    )(x, w, bias.reshape(1, n), ma, proj)


def _bf16_terms(x):
    hi = x.astype(BF16).astype(F32)
    r = x - hi
    mid = r.astype(BF16).astype(F32)
    return hi, mid, r - mid


def _gate_epilogue(o, gon, og, ga):
    ya = _rms(o, gon) * _silu(og)
    return (_sigmoid(ga) * ya).astype(BF16)


def _hgrn_prompt_body(q_ref, f_ref, i_ref, og_ref, ga_ref, lb_ref, gon_ref, ma_ref, so_ref,
                      st_ref, b_scr, k_scr, *, chunk, n_sub, hb):
    c_idx = pl.program_id(2)
    C = chunk

    @pl.when(c_idx == 0)
    def _():
        st_ref[...] = jnp.zeros_like(st_ref)

    row_i = lax.broadcasted_iota(jnp.int32, (C, C), 0)
    col_i = lax.broadcasted_iota(jnp.int32, (C, C), 1)
    causal = row_i >= col_i
    tri = causal.astype(BF16)

    lb = lb_ref[...]
    f = _forget_gate(f_ref[...], lb)
    log_f = jnp.log(f)
    k_scr[...] = 1.0 - f
    spread = jnp.zeros((1, hb * LANES), F32)
    for c in range(n_sub):
        rows = slice(c * C, (c + 1) * C)
        b = sum(jnp.dot(tri, term.astype(BF16), preferred_element_type=F32)
                for term in _bf16_terms(log_f[rows]))
        b_scr[rows, :] = b
        b_mid = b[C // 2 - 1:C // 2, :]
        spread = jnp.maximum(spread, jnp.maximum(b[0:1, :] - b_mid, b_mid - b[C - 1:C, :]))
    factored_ok = jnp.max(spread) < FACTORED_DECAY_MAX_SPREAD

    def pairwise_scores(q, c, sl):
        rows = slice(c * C, (c + 1) * C)
        b = b_scr[rows, sl]
        t_i = lax.broadcasted_iota(jnp.int32, (C, 1), 0)
        s_i = lax.broadcasted_iota(jnp.int32, (1, C), 1)

        def column_group(g, acc):
            grp = pl.ds(pl.multiple_of(c * C + g * SUBLANES, SUBLANES), SUBLANES)
            b_g = b_scr[grp, sl]
            k_g = k_scr[grp, sl]
            for r in range(SUBLANES):
                s = g * SUBLANES + r
                decay = jnp.exp(jnp.where(t_i >= s, b - b_g[r:r + 1], -jnp.inf))
                col = jnp.sum(q * k_g[r:r + 1] * decay, axis=-1, keepdims=True)
                acc = jnp.where(s_i == s, col, acc)
            return acc

        return lax.fori_loop(0, C // SUBLANES, column_group, jnp.zeros((C, C), F32))

    nt = (((1,), (1,)), ((), ()))

    def run(factored):
        for hh in range(hb):
            sl = slice(hh * LANES, (hh + 1) * LANES)
            st = st_ref[hh]
            for c in range(n_sub):
                rows = slice(c * C, (c + 1) * C)
                q = q_ref[rows, sl]
                b = b_scr[rows, sl]
                k = k_scr[rows, sl]
                b_last = b[C - 1:C, :]
                if factored:
                    b_mid = b[C // 2 - 1:C // 2, :]
                    q_dec = q * jnp.exp(b - b_mid)
                    k_dec = k * jnp.exp(b_mid - b)
                    scores = lax.dot_general(q_dec.astype(BF16), k_dec.astype(BF16), nt,
                                             preferred_element_type=F32)
                    scores = jnp.where(causal, scores, 0.0)
                    qe = q_dec * jnp.exp(b_mid)
                    kl = k_dec * jnp.exp(b_last - b_mid)
                else:
                    scores = pairwise_scores(q, c, sl)
                    qe = q * jnp.exp(b)
                    kl = k * jnp.exp(b_last - b)
                v = i_ref[rows, sl]
                intra = jnp.dot(scores.astype(BF16), v.astype(BF16), preferred_element_type=F32)
                inter = lax.dot_general(qe.astype(BF16), st.astype(BF16), nt, preferred_element_type=F32)
                st = st * jnp.exp(b_last) + jnp.dot(v.T.astype(BF16), kl.astype(BF16),
                                                    preferred_element_type=F32)
                ma_ref[rows, sl] = _gate_epilogue(inter + intra, gon_ref[:, sl],
                                                  og_ref[rows, sl], ga_ref[rows, sl])
            st_ref[hh] = st

    @pl.when(factored_ok)
    def _():
        run(True)

    @pl.when(jnp.logical_not(factored_ok))
    def _():
        run(False)

    @pl.when(c_idx == pl.num_programs(2) - 1)
    def _():
        for hh in range(hb):
            so_ref[hh] = st_ref[hh].T


def _hgrn_prompt_call(proj, lb, g_onorm, bsz, t, n_heads, d_conv, n_rows):
    d = n_heads * LANES
    chunk = _tile(t, 128, LANES)
    n_sub = 2 if t % (2 * chunk) == 0 else 1
    tc = chunk * n_sub
    hb = _tile(n_heads, 16, 1)
    w = hb * LANES
    hcols = n_heads // hb
    ga0 = (4 * d + 2 * d_conv) // w
    per_b = t // tc

    def col(base):
        return pl.BlockSpec((tc, w), lambda b, h, c: (b * per_b + c, base + h))

    body = functools.partial(_hgrn_prompt_body, chunk=chunk, n_sub=n_sub, hb=hb)
    return pl.pallas_call(
        body,
        grid=(bsz, hcols, per_b),
        in_specs=[col(0), col(hcols), col(2 * hcols), col(3 * hcols), col(ga0),
                  pl.BlockSpec((1, w), lambda b, h, c: (0, h)),
                  pl.BlockSpec((1, w), lambda b, h, c: (0, h))],
        out_specs=[pl.BlockSpec((tc, w), lambda b, h, c: (b * per_b + c, h)),
                   pl.BlockSpec((None, hb, LANES, LANES), lambda b, h, c: (b, h, 0, 0))],
        out_shape=[jax.ShapeDtypeStruct((n_rows, d), BF16),
                   jax.ShapeDtypeStruct((bsz, n_heads, LANES, LANES), F32)],
        scratch_shapes=[pltpu.VMEM((hb, LANES, LANES), F32)] + [pltpu.VMEM((tc, w), F32)] * 2,
        compiler_params=_params("arbitrary", "arbitrary", "arbitrary"),
        name="hgrn_prompt",
    )(proj, proj, proj, proj, proj, lb, g_onorm)


def _hgrn_sample_body(q_ref, f_ref, i_ref, og_ref, ga_ref, lb_ref, gon_ref, s_ref, ma_in_ref,
                      ma_ref, so_ref, *, t_s, n_heads):
    del ma_in_ref
    nb = SUBLANES // t_s
    row = lax.broadcasted_iota(jnp.int32, (SUBLANES, 1), 0)
    tok = row % t_s
    contract0 = (((0,), (0,)), ((), ()))

    def head(h, carry):
        for g in range(q_ref.shape[0] // SUBLANES):
            head_rows(h, g)
        return carry

    def head_rows(h, g):
        sl = pl.ds(pl.multiple_of(h * LANES, LANES), LANES)
        rs = slice(g * SUBLANES, (g + 1) * SUBLANES)
        q = q_ref[rs, sl]
        v = i_ref[rs, sl]
        lb = lb_ref[:, sl]
        f = _forget_gate(f_ref[rs, sl], lb)
        log_f = jnp.log(f)
        k = 1.0 - f
        b = log_f
        for d in range(1, t_s):
            b = b + jnp.where(tok >= d, pltpu.roll(log_f, d, 0), 0.0)
        o = jnp.zeros((SUBLANES, LANES), F32)
        for d in range(t_s):
            kd, bd, vd = (k, b, v) if d == 0 else (pltpu.roll(k, d, 0), pltpu.roll(b, d, 0),
                                                  pltpu.roll(v, d, 0))
            decay = jnp.exp(jnp.where(tok >= d, b - bd, -jnp.inf))
            score = jnp.sum(q * kd * decay, axis=-1, keepdims=True)
            o = o + score * jnp.where(tok >= d, vd, 0.0)
        qe = (q * jnp.exp(b)).astype(BF16)
        for bi in range(nb):
            in_b = (row >= bi * t_s) & (row < (bi + 1) * t_s)
            s0 = s_ref[g * nb + bi, h]
            inter = jnp.dot(qe, s0.astype(BF16), preferred_element_type=F32)
            o = o + jnp.where(in_b, inter, 0.0)
            b_last = b[(bi + 1) * t_s - 1:(bi + 1) * t_s, :]
            kk = jnp.where(in_b, k * jnp.exp(jnp.where(in_b, b_last - b, 0.0)), 0.0)
            vv = jnp.where(in_b, v, 0.0)
            upd = lax.dot_general(kk.astype(BF16), vv.astype(BF16), contract0,
                                  preferred_element_type=F32)
            hi, mid, lo = _bf16_terms(jnp.exp(b_last))
            d_rows = jnp.where(row == 0, hi, jnp.where(row == 1, mid, jnp.where(row == 2, lo, 0.0)))
            one_rows = jnp.where(row < 3, jnp.ones((SUBLANES, LANES), F32), 0.0)
            decay_kv = lax.dot_general(d_rows.astype(BF16), one_rows.astype(BF16), contract0,
                                       preferred_element_type=F32)
            so_ref[g * nb + bi, h] = s0 * decay_kv + upd
        ma_ref[rs, sl] = _gate_epilogue(o, gon_ref[:, sl], og_ref[rs, sl], ga_ref[rs, sl])

    lax.fori_loop(0, n_heads, head, 0, unroll=_tile(n_heads, 8, 1))


def _hgrn_sample_call(proj, lb, g_onorm, state, ma, row0, t_s, d_conv):
    bs, n_heads = state.shape[0], state.shape[1]
    d = n_heads * LANES
    assert SUBLANES % t_s == 0
    groups = 2 if bs * t_s % (2 * SUBLANES) == 0 else 1
    rows = groups * SUBLANES
    nb = rows // t_s
    assert row0 % rows == 0 and bs % nb == 0
    r0 = row0 // rows
    ga0 = (4 * d + 2 * d_conv) // d

    def col(base):
        return pl.BlockSpec((rows, d), lambda i: (r0 + i, base))

    body = functools.partial(_hgrn_sample_body, t_s=t_s, n_heads=n_heads)
    return pl.pallas_call(
        body,
        grid=(bs // nb,),
        in_specs=[col(0), col(1), col(2), col(3), col(ga0),
                  pl.BlockSpec((1, d), lambda i: (0, 0)),
                  pl.BlockSpec((1, d), lambda i: (0, 0)),
                  pl.BlockSpec((nb, n_heads, LANES, LANES), lambda i: (i, 0, 0, 0)),
                  pl.BlockSpec(memory_space=pl.ANY)],
        out_specs=[pl.BlockSpec((rows, d), lambda i: (r0 + i, 0)),
                   pl.BlockSpec((nb, n_heads, LANES, LANES), lambda i: (i, 0, 0, 0))],
        out_shape=[jax.ShapeDtypeStruct(ma.shape, ma.dtype),
                   jax.ShapeDtypeStruct(state.shape, F32)],
        input_output_aliases={8: 0},
        compiler_params=_params("arbitrary"),
        name="hgrn_sample",
    )(proj, proj, proj, proj, proj, lb, g_onorm, state, ma)


def _ln_swish(x, g, b):
    mu = jnp.mean(x, axis=-1, keepdims=True)
    xc = x - mu
    var = jnp.mean(xc * xc, axis=-1, keepdims=True)
    return _silu(xc * lax.rsqrt(var + EPS) * g + b)


def _conv_prompt_body(ga_ref, gb_ref, w_ref, bdw_ref, lg_ref, lbias_ref, o_ref, cache_ref,
                      ext_ref, conv_ref, *, width, hist, rc):
    t_idx = pl.program_id(1)
    tt = ga_ref.shape[0]
    dc = ga_ref.shape[1]

    @pl.when(t_idx == 0)
    def _():
        ext_ref[0:hist, :] = jnp.zeros((hist, dc), F32)

    @pl.when(t_idx > 0)
    def _():
        ext_ref[0:hist, :] = ext_ref[tt:tt + hist, :]

    ext_ref[hist:hist + tt, :] = ga_ref[...] * _sigmoid(gb_ref[...])
    off = hist - (width - 1)

    def strip(l, carry):
        lanes = pl.ds(pl.multiple_of(l * LANES, LANES), LANES)
        for r0 in range(0, tt, rc):
            win = ext_ref[r0:r0 + rc + hist, lanes]
            acc = jnp.zeros((rc, LANES), F32) + bdw_ref[:, lanes]
            for s in range(SUBLANES):
                taps = [j for j in range(width) if (off + j) % SUBLANES == s]
                if not taps:
                    continue
                shifted = win if s == 0 else pltpu.roll(win, rc + hist - s, 0)
                for j in taps:
                    a = (off + j) // SUBLANES * SUBLANES
                    acc = acc + w_ref[j:j + 1, lanes] * shifted[a:a + rc]
            conv_ref[r0:r0 + rc, lanes] = acc
        return carry

    lax.fori_loop(0, dc // LANES, strip, 0)

    def norm(i, carry):
        rows = pl.ds(pl.multiple_of(i * rc, rc), rc)
        o_ref[rows, :] = _ln_swish(conv_ref[rows, :], lg_ref[...], lbias_ref[...]).astype(o_ref.dtype)
        return carry

    lax.fori_loop(0, tt // rc, norm, 0, unroll=2)

    @pl.when(t_idx == pl.num_programs(1) - 1)
    def _():
        cache_ref[...] = ext_ref[hist + tt - (width - 1):hist + tt, :]


def _conv_prompt_call(proj, w_dw, b_dw, ln_g, ln_b, bsz, t, d_hgrn, n_rows):
    width, dc = w_dw.shape
    tt = _tile(t, 256, 32)
    hist = 32
    assert width - 1 <= hist <= tt and (4 * d_hgrn) % dc == 0
    c0 = 4 * d_hgrn // dc
    per_b = t // tt
    vec = lambda a: a.reshape(1, dc)
    body = functools.partial(_conv_prompt_body, width=width, hist=hist, rc=32)
    return pl.pallas_call(
        body,
        grid=(bsz, per_b),
        in_specs=[pl.BlockSpec((tt, dc), lambda b, i: (b * per_b + i, c0)),
                  pl.BlockSpec((tt, dc), lambda b, i: (b * per_b + i, c0 + 1)),
                  pl.BlockSpec((width, dc), lambda b, i: (0, 0)),
                  pl.BlockSpec((1, dc), lambda b, i: (0, 0)),
                  pl.BlockSpec((1, dc), lambda b, i: (0, 0)),
                  pl.BlockSpec((1, dc), lambda b, i: (0, 0))],
        out_specs=[pl.BlockSpec((tt, dc), lambda b, i: (b * per_b + i, 0)),
                   pl.BlockSpec((None, width - 1, dc), lambda b, i: (b, 0, 0))],
        out_shape=[jax.ShapeDtypeStruct((n_rows, dc), BF16),
                   jax.ShapeDtypeStruct((bsz, width - 1, dc), F32)],
        scratch_shapes=[pltpu.VMEM((hist + tt, dc), F32), pltpu.VMEM((tt, dc), F32)],
        compiler_params=_params("arbitrary", "arbitrary"),
        name="conv_prompt",
    )(proj, proj, w_dw, vec(b_dw), vec(ln_g), vec(ln_b))


def _conv_sample_body(ga_ref, gb_ref, cache_ref, w_ref, bdw_ref, lg_ref, lbias_ref, vc_in_ref,
                      o_ref, ncache_ref, ext_ref, *, width, t_s, bb):
    del vc_in_ref
    u = ga_ref[...] * _sigmoid(gb_ref[...])
    for bi in range(bb):
        ext_ref[0:width - 1, :] = cache_ref[bi]
        ext_ref[width - 1:width - 1 + t_s, :] = u[bi * t_s:(bi + 1) * t_s]
        acc = jnp.zeros((t_s, u.shape[1]), F32) + bdw_ref[...]
        for j in range(width):
            acc = acc + w_ref[j:j + 1, :] * ext_ref[j:j + t_s, :]
        ncache_ref[bi] = ext_ref[t_s:t_s + width - 1, :]
        o_ref[bi * t_s:(bi + 1) * t_s, :] = _ln_swish(acc, lg_ref[...], lbias_ref[...]).astype(o_ref.dtype)


def _conv_sample_call(proj, cache, w_dw, b_dw, ln_g, ln_b, vconv, row0, t_s, d_hgrn):
    width, dc = w_dw.shape
    bs = cache.shape[0]
    bb = _tile(bs, 8, 1)
    rows = bb * t_s
    assert rows % 16 == 0 and row0 % rows == 0
    r0 = row0 // rows
    c0 = 4 * d_hgrn // dc
    vec = lambda a: a.reshape(1, dc)
    body = functools.partial(_conv_sample_body, width=width, t_s=t_s, bb=bb)
    return pl.pallas_call(
        body,
        grid=(bs // bb,),
        in_specs=[pl.BlockSpec((rows, dc), lambda i: (r0 + i, c0)),
                  pl.BlockSpec((rows, dc), lambda i: (r0 + i, c0 + 1)),
                  pl.BlockSpec((bb, width - 1, dc), lambda i: (i, 0, 0)),
                  pl.BlockSpec((width, dc), lambda i: (0, 0)),
                  pl.BlockSpec((1, dc), lambda i: (0, 0)),
                  pl.BlockSpec((1, dc), lambda i: (0, 0)),
                  pl.BlockSpec((1, dc), lambda i: (0, 0)),
                  pl.BlockSpec(memory_space=pl.ANY)],
        out_specs=[pl.BlockSpec((rows, dc), lambda i: (r0 + i, 0)),
                   pl.BlockSpec((bb, width - 1, dc), lambda i: (i, 0, 0))],
        out_shape=[jax.ShapeDtypeStruct(vconv.shape, vconv.dtype),
                   jax.ShapeDtypeStruct(cache.shape, F32)],
        scratch_shapes=[pltpu.VMEM((width - 1 + t_s + SUBLANES, dc), F32)],
        input_output_aliases={7: 0},
        compiler_params=_params("arbitrary"),
        name="conv_sample",
    )(proj, proj, cache, w_dw, vec(b_dw), vec(ln_g), vec(ln_b), vconv)


def _expert_changed(be_ref, r):
    return jnp.logical_or(r == 0, be_ref[r] != be_ref[jnp.maximum(r - 1, 0)])


def _weight_copies(w_hbms, buf_refs, sem_ref, e, j, slot):
    copies = []
    for w_hbm, buf_ref in zip(w_hbms, buf_refs):
        tcol = buf_ref.shape[-1]
        cols = pl.ds(pl.multiple_of(j * tcol, tcol), tcol)
        copies.append(pltpu.make_async_copy(w_hbm.at[e, :, cols], buf_ref.at[slot], sem_ref.at[slot]))
    return copies


def _stage_expert_weights(be_ref, nr_ref, nx_ref, w_hbms, buf_refs, bf16_refs, sem_ref, cnt_ref):
    j, r = pl.program_id(0), pl.program_id(1)

    @pl.when(jnp.logical_and(j == 0, r == 0))
    def _():
        cnt_ref[0] = 0
        for c in _weight_copies(w_hbms, buf_refs, sem_ref, be_ref[0], 0, 0):
            c.start()

    @pl.when(jnp.logical_and(r < nr_ref[0], _expert_changed(be_ref, r)))
    def _():
        slot = cnt_ref[0] % 2
        for c in _weight_copies(w_hbms, buf_refs, sem_ref, be_ref[r], j, slot):
            c.wait()
        wraps = nx_ref[r] >= nr_ref[0]
        e_next = be_ref[jnp.where(wraps, 0, nx_ref[r])]
        j_next = jnp.where(wraps, j + 1, j)

        @pl.when(j_next < pl.num_programs(0))
        def _():
            for c in _weight_copies(w_hbms, buf_refs, sem_ref, e_next, j_next, 1 - slot):
                c.start()

        for buf_ref, dst_ref in zip(buf_refs, bf16_refs):
            dst_ref[...] = buf_ref[slot].astype(BF16)
        cnt_ref[0] = cnt_ref[0] + 1


def _moe_up_body(be_ref, nr_ref, nx_ref, x_ref, w1_hbm, w3_hbm, g_ref,
                 w1f_ref, w3f_ref, w1b_ref, w3b_ref, sem_ref, cnt_ref):
    _stage_expert_weights(be_ref, nr_ref, nx_ref, (w1_hbm, w3_hbm), (w1f_ref, w3f_ref),
                          (w1b_ref, w3b_ref), sem_ref, cnt_ref)

    @pl.when(pl.program_id(1) < nr_ref[0])
    def _():
        x = x_ref[...]
        a = jnp.dot(x, w1b_ref[...], preferred_element_type=F32)
        b = jnp.dot(x, w3b_ref[...], preferred_element_type=F32)
        g_ref[...] = (_silu(a) * b).astype(g_ref.dtype)


def _moe_down_body(be_ref, nr_ref, nx_ref, g_ref, w2_hbm, y_ref, w2f_ref, w2b_ref, sem_ref, cnt_ref):
    _stage_expert_weights(be_ref, nr_ref, nx_ref, (w2_hbm,), (w2f_ref,), (w2b_ref,), sem_ref, cnt_ref)

    @pl.when(pl.program_id(1) < nr_ref[0])
    def _():
        y_ref[...] = _pack_bf16_halves(jnp.dot(g_ref[...], w2b_ref[...], preferred_element_type=F32))


def _dispatch_body(tok_ref, nr_ref, cnt_ref, first_ref, h2_hbm, o_ref, buf_ref, sem_ref):
    r = pl.program_id(0)
    tm = o_ref.shape[0]

    def start(step, slot):
        def copy_row(i, carry):
            pltpu.make_async_copy(h2_hbm.at[pl.ds(tok_ref[first_ref[step] + i], 1)],
                                  buf_ref.at[slot, pl.ds(i, 1)], sem_ref.at[slot]).start()
            return carry

        @pl.when(cnt_ref[step] == tm)
        def _():
            lax.fori_loop(0, tm, copy_row, 0, unroll=8)

        @pl.when(cnt_ref[step] < tm)
        def _():
            lax.fori_loop(0, cnt_ref[step], copy_row, 0)

    @pl.when(r == 0)
    def _():
        buf_ref[...] = jnp.zeros_like(buf_ref)
        start(0, 0)

    @pl.when(r + 1 < nr_ref[0])
    def _():
        start(r + 1, (r + 1) % 2)

    @pl.when(r < nr_ref[0])
    def _():
        slot = r % 2
        rows = pl.ds(0, pl.multiple_of(cnt_ref[r], ROW_GROUP))
        pltpu.make_async_copy(h2_hbm.at[rows], buf_ref.at[slot, rows], sem_ref.at[slot]).wait()
        lo, hi = _unpack_bf16_halves(buf_ref[slot])
        half = lo.shape[1]
        o_ref[:, 0:half] = lo.astype(o_ref.dtype)
        o_ref[:, half:] = hi.astype(o_ref.dtype)


def _dispatch_call(h2p, sorted_tok, n_real, block_cnt, block_first, tm):
    half = h2p.shape[1]
    d = 2 * half
    nblk = block_cnt.shape[0]
    return pl.pallas_call(
        _dispatch_body,
        grid_spec=pltpu.PrefetchScalarGridSpec(
            num_scalar_prefetch=4,
            grid=(nblk,),
            in_specs=[pl.BlockSpec(memory_space=pl.ANY)],
            out_specs=pl.BlockSpec((tm, d), lambda r, o, nr, cnt, first: (jnp.minimum(r, nr[0] - 1), 0)),
            scratch_shapes=[pltpu.VMEM((2, tm, half), jnp.uint32), pltpu.SemaphoreType.DMA((2,))]),
        out_shape=jax.ShapeDtypeStruct((nblk * tm, d), BF16),
        compiler_params=_params("arbitrary"),
        name="moe_dispatch",
    )(sorted_tok, n_real, block_cnt, block_first, h2p)


def _moe_experts(x_sorted, block_e, n_real, next_run, w1, w3, w2, tm):
    l, d = x_sorted.shape
    n_exp, _, dh = w1.shape
    nblk = l // tm
    th = _tile(dh, 512, LANES)
    rr = lambda r, nr: jnp.minimum(r, nr[0] - 1)
    hbm = pl.BlockSpec(memory_space=pl.ANY)

    g = pl.pallas_call(
        _moe_up_body,
        grid_spec=pltpu.PrefetchScalarGridSpec(
            num_scalar_prefetch=3,
            grid=(dh // th, nblk),
            in_specs=[pl.BlockSpec((tm, d), lambda j, r, be, nr, nx: (rr(r, nr), 0)), hbm, hbm],
            out_specs=pl.BlockSpec((tm, th), lambda j, r, be, nr, nx: (rr(r, nr), j)),
            scratch_shapes=[pltpu.VMEM((2, d, th), F32), pltpu.VMEM((2, d, th), F32),
                            pltpu.VMEM((d, th), BF16), pltpu.VMEM((d, th), BF16),
                            pltpu.SemaphoreType.DMA((2,)), pltpu.SMEM((1,), jnp.int32)]),
        out_shape=jax.ShapeDtypeStruct((l, dh), BF16),
        compiler_params=_params("arbitrary", "arbitrary"),
        name="moe_up",
    )(block_e, n_real, next_run, x_sorted, w1, w3)

    return pl.pallas_call(
        _moe_down_body,
        grid_spec=pltpu.PrefetchScalarGridSpec(
            num_scalar_prefetch=3,
            grid=(1, nblk),
            in_specs=[pl.BlockSpec((tm, dh), lambda j, r, be, nr, nx: (rr(r, nr), 0)), hbm],
            out_specs=pl.BlockSpec((tm, d // 2), lambda j, r, be, nr, nx: (rr(r, nr), 0)),
            scratch_shapes=[pltpu.VMEM((2, dh, d), F32), pltpu.VMEM((dh, d), BF16),
                            pltpu.SemaphoreType.DMA((2,)), pltpu.SMEM((1,), jnp.int32)]),
        out_shape=jax.ShapeDtypeStruct((l, d // 2), jnp.uint32),
        compiler_params=_params("arbitrary", "arbitrary"),
        name="moe_down",
    )(block_e, n_real, next_run, g, w2)


def _route(logits, n_groups, n_experts, tm):
    n = logits.shape[0]
    eg = n_experts // n_groups
    l1 = logits[:, :n_groups]
    p1 = jax.nn.softmax(l1, axis=-1)
    grp = jnp.argmax(l1, axis=-1)
    p_grp = jnp.max(p1, axis=-1, keepdims=True)
    l2 = logits[:, n_groups:n_groups + n_experts].reshape(n, n_groups, eg)
    l2g = jnp.take_along_axis(l2, grp[:, None, None], axis=1)[:, 0]
    top_v, top_i = lax.top_k(l2g, TOP_K)
    weights = p_grp * jax.nn.softmax(top_v, axis=-1)
    expert = (grp[:, None] * eg + top_i).astype(jnp.int32)

    a = n * TOP_K
    i32 = lambda v: v.astype(jnp.int32)
    e = expert.reshape(-1)
    counts = i32(jnp.bincount(e, length=n_experts))
    padded = (counts + tm - 1) // tm * tm
    ends = jnp.cumsum(padded)
    starts = ends - padded
    seg_start = jnp.cumsum(counts) - counts
    order = jnp.argsort(e)
    rank = jnp.argsort(order)
    pos = i32(starts[e] + rank - seg_start[e]).reshape(n, TOP_K)
    nblk = -(-a // tm) + n_experts
    block_e = i32(jnp.minimum(jnp.searchsorted(ends, jnp.arange(nblk) * tm, side='right'),
                              n_experts - 1))
    n_real = i32(ends[-1:] // tm)
    block_cnt = jnp.clip(counts[block_e] - (jnp.arange(nblk) * tm - starts[block_e]), 0, tm)
    block_cnt = i32(-(-block_cnt // ROW_GROUP) * ROW_GROUP)
    next_run = i32(ends[block_e] // tm)
    block_first = i32(jnp.minimum(seg_start[block_e] + jnp.arange(nblk) * tm - starts[block_e], a))
    sorted_tok = jnp.pad(i32(order // TOP_K), (0, ROW_GROUP))
    return weights, sorted_tok, pos, block_e, n_real, block_cnt, block_first, next_run


def kernel(x_prompt, x_sample, state_hgrn, cache_conv, c_prompt, c_sample, g_mix, w_ada, b_ada, w_in, lb_param, g_onorm, w_dw, b_dw, ln_g, ln_b, w_pw2, b_pw2, w_out, g_ffn, w_r1, b_r1, w_r2, b_r2, moe_w1, moe_w3, moe_w2, g_final):
    bp, t_p, d = x_prompt.shape
    bs, t_s, _ = x_sample.shape
    depth, _, n_heads, head_k, head_v = state_hgrn.shape
    assert depth == 1 and head_k == LANES and head_v == LANES
    d_hgrn = n_heads * head_k
    d_conv = w_dw.shape[-1]
    n_groups, n_experts = w_r1.shape[-1], w_r2.shape[-1]
    np_, ns = bp * t_p, bs * t_s
    n = np_ + ns
    rows = _Rows(np_, t_p, ns, t_s, d)
    vec = lambda a: a.reshape(1, -1)

    assert bs % SUBLANES == 0
    c_all = jnp.concatenate([c_sample, c_prompt], axis=0)
    c_all = jnp.pad(c_all, ((0, -c_all.shape[0] % 16), (0, 0)))
    ada = _ada_call(c_all, w_ada[0], b_ada[0], bs, t_s, bp)
    ada_p = ada_s = ada
    SH_M, SC_M, GT_M, SH_F, SC_F, GT_F = range(6)
    mod = lambda which: (lambda sample: rows.mod(sample, which))
    ada_of = lambda sample: ada
    xs = (x_prompt.reshape(np_, d), x_sample.reshape(ns, d))
    x_spec = lambda sample: rows.rows(sample, d, offset=False)
    flat = lambda width: (lambda sample: rows.rows(sample, width))
    const = lambda shape: (lambda sample: rows.const(shape))

    (h,) = _rowwise(
        _norm1_body, rows, "norm1",
        [xs[0], vec(g_mix[0]), ada_p, ada_p], [xs[1], vec(g_mix[0]), ada_s, ada_s],
        [x_spec, const((1, d)), mod(SC_M), mod(SH_M)], [d], [BF16])
    proj = _mm_call(h, w_in[0], F32, 544, 1024, "in_proj")

    lb = jax.nn.softmax(lb_param.astype(F32), axis=0)[0:1]
    ma, state_p = _hgrn_prompt_call(proj, lb, vec(g_onorm[0]), bp, t_p, n_heads, d_conv, n)
    ma, state_s = _hgrn_sample_call(proj, lb, vec(g_onorm[0]), state_hgrn[0], ma, np_, t_s, d_conv)

    vconv, cache_p = _conv_prompt_call(proj, w_dw[0], b_dw[0], ln_g[0], ln_b[0], bp, t_p, d_hgrn, n)
    vconv, cache_s = _conv_sample_call(proj, cache_conv[0], w_dw[0], b_dw[0], ln_g[0], ln_b[0],
                                       vconv, np_, t_s, d_hgrn)

    merged = _mm_pw2_call(vconv, w_pw2[0], b_pw2[0], ma, proj, 4 * d_hgrn + 2 * d_conv + d)
    attn = _mm_call(merged, w_out[0], F32, 544, 1024, "out_proj")

    n_r = n_groups + n_experts
    n_r_pad = -(-n_r // LANES) * LANES
    w_r = jnp.concatenate([w_r1[0], w_r2[0], jnp.zeros((d, n_r_pad - n_r), F32)], axis=1)
    b_r = jnp.concatenate([b_r1[0], b_r2[0], jnp.zeros((n_r_pad - n_r,), F32)]).reshape(1, n_r_pad)
    x1, h2p, logits = _rowwise(
        _norm2_body, rows, "norm2",
        [xs[0], attn, ada_p, vec(g_ffn[0]), ada_p, ada_p, w_r, b_r],
        [xs[1], attn, ada_s, vec(g_ffn[0]), ada_s, ada_s, w_r, b_r],
        [x_spec, flat(d), mod(GT_M), const((1, d)), mod(SC_F), mod(SH_F),
         const((d, n_r_pad)), const((1, n_r_pad))],
        [d, d // 2, n_r_pad], [F32, jnp.uint32, F32])

    tm = 256
    weights, sorted_tok, pos, block_e, n_real, block_cnt, block_first, next_run = _route(
        logits, n_groups, n_experts, tm)
    x_sorted = _dispatch_call(h2p, sorted_tok, n_real, block_cnt, block_first, tm)
    y_sorted = _moe_experts(x_sorted, block_e, n_real, next_run, moe_w1[0], moe_w3[0], moe_w2[0], tm)
    pos_flat = pos.T.reshape(-1)

    outs = []
    for sample in (False, True):
        tt = rows.tt_s if sample else rows.tt_p
        wrap = lambda spec: pl.BlockSpec(spec.block_shape, lambda i, p, f=spec.index_map: f(i))
        outs.append(pl.pallas_call(
            functools.partial(_final_body, row0=np_ if sample else 0),
            grid_spec=pltpu.PrefetchScalarGridSpec(
                num_scalar_prefetch=1,
                grid=rows.grid(sample),
                in_specs=[wrap(rows.rows(sample, d)), wrap(rows.rows(sample, TOP_K)),
                          wrap(rows.mod(sample, GT_F)), wrap(rows.const((1, d))),
                          pl.BlockSpec(memory_space=pl.ANY)],
                out_specs=wrap(rows.rows(sample, d, offset=False)),
                scratch_shapes=[pltpu.VMEM((2, TOP_K, tt, d // 2), jnp.uint32),
                                pltpu.SemaphoreType.DMA((2,))]),
            out_shape=jax.ShapeDtypeStruct((ns if sample else np_, d), F32),
            compiler_params=_params("arbitrary"),
            name="final_sample" if sample else "final_prompt",
        )(pos_flat, x1, weights, ada_of(sample), vec(g_final), y_sorted))
    y_prompt = outs[0].reshape(bp, t_p, d)
    y_sample = outs[1].reshape(bs, t_s, d)
    return (y_prompt, y_sample, state_p[None], cache_p[None], state_s[None], cache_s[None])
```

```python
import functools

import jax
import jax.numpy as jnp
from jax import lax
from jax.experimental import pallas as pl
from jax.experimental.pallas import tpu as pltpu

F32 = jnp.float32
BF16 = jnp.bfloat16
EPS = 1e-6
TOP_K = 2
LANES = 128
SUBLANES = 8
ROW_GROUP = SUBLANES
VMEM_LIMIT = 60 * 1024 * 1024
FACTORED_DECAY_MAX_SPREAD = 80.0


def _params(*sem):
    return pltpu.CompilerParams(dimension_semantics=sem, vmem_limit_bytes=VMEM_LIMIT)


def _tile(n, target, mult):
    best = None
    for t in range(mult, min(n, target) + 1, mult):
        if n % t == 0:
            best = t
    assert best is not None, (n, target, mult)
    return best


def _sigmoid(x):
    return 0.5 * jnp.tanh(0.5 * x) + 0.5


def _forget_gate(f_logit, lb):
    return lb + (1.0 - lb) / (1.0 + jnp.exp(-f_logit))


def _silu(x):
    return x * _sigmoid(x)


def _ada_body(c_ref, w_ref, b_ref, o_ref, *, bs, t_s, bp):
    s = _silu(c_ref[...]).astype(BF16)
    ada = jnp.dot(s, w_ref[...].astype(BF16), preferred_element_type=F32) + b_ref[...]
    ns = bs * t_s
    r_i = lax.broadcasted_iota(jnp.int32, (ns, bs), 0)
    b_i = lax.broadcasted_iota(jnp.int32, (ns, bs), 1)
    lo = b_i * t_s
    expand = jnp.where((r_i >= lo) & (r_i < lo + t_s), 1.0, 0.0).astype(BF16)
    o_ref[0:ns, :] = sum(jnp.dot(expand, term.astype(BF16), preferred_element_type=F32)
                         for term in _bf16_terms(ada[0:bs]))
    for b in range(bp):
        o_ref[ns + b * SUBLANES:ns + (b + 1) * SUBLANES, :] = jnp.broadcast_to(
            ada[bs + b:bs + b + 1], (SUBLANES, ada.shape[1]))


def _ada_call(c_all, w_ada, b_ada, bs, t_s, bp):
    rows, d = c_all.shape
    n6 = w_ada.shape[1]
    tn = _tile(n6, 1024, LANES)
    out_rows = bs * t_s + bp * SUBLANES
    return pl.pallas_call(
        functools.partial(_ada_body, bs=bs, t_s=t_s, bp=bp),
        grid=(n6 // tn,),
        in_specs=[pl.BlockSpec((rows, d), lambda j: (0, 0)),
                  pl.BlockSpec((d, tn), lambda j: (0, j)),
                  pl.BlockSpec((1, tn), lambda j: (0, j))],
        out_specs=pl.BlockSpec((out_rows, tn), lambda j: (0, j)),
        out_shape=jax.ShapeDtypeStruct((out_rows, n6), F32),
        compiler_params=_params("arbitrary"),
        name="ada",
    )(c_all, w_ada, b_ada.reshape(1, n6))


def _rms(x, g):
    return x * lax.rsqrt(jnp.mean(x * x, axis=-1, keepdims=True) + EPS) * g


def _mod(ref, n_rows):
    return ref[...] if ref.shape[0] == n_rows else ref[0:1, :]


def _norm1_body(x_ref, g_ref, sc_ref, sh_ref, *rest):
    o_ref = rest[-1]
    x = x_ref[...]
    tt = x.shape[0]
    y = _rms(x, g_ref[...])
    o_ref[...] = (y * (1.0 + _mod(sc_ref, tt)) + _mod(sh_ref, tt)).astype(o_ref.dtype)


def _pack_bf16_halves(x):
    w = x.shape[1] // 2
    lo = lax.bitcast_convert_type(x[:, :w].astype(BF16).astype(F32), jnp.uint32)
    hi = lax.bitcast_convert_type(x[:, w:].astype(BF16).astype(F32), jnp.uint32)
    return (lo >> 16) | hi


def _unpack_bf16_halves(u):
    lo = lax.bitcast_convert_type(u << 16, F32)
    hi = lax.bitcast_convert_type(u & jnp.uint32(0xFFFF0000), F32)
    return lo, hi


def _norm2_body(x_ref, a_ref, gt_ref, g_ref, sc_ref, sh_ref, wr_ref, br_ref, *rest):
    x1_ref, h2p_ref, lg_ref = rest[-3:]
    tt = x_ref.shape[0]
    x1 = x_ref[...] + _mod(gt_ref, tt) * a_ref[...]
    x1_ref[...] = x1
    h2 = _rms(x1, g_ref[...]) * (1.0 + _mod(sc_ref, tt)) + _mod(sh_ref, tt)
    h2p_ref[...] = _pack_bf16_halves(h2)
    lg_ref[...] = jnp.dot(h2, wr_ref[...], preferred_element_type=F32,
                          precision=lax.Precision.HIGHEST) + br_ref[...]


def _row_gather_wait(n_rows, src_hbm, dst, sem):
    pltpu.make_async_copy(src_hbm.at[pl.ds(0, n_rows)], dst, sem).wait()


def _final_body(pos_ref, x1_ref, w_ref, gt_ref, g_ref, y_hbm, o_ref, buf_ref, sem_ref, *, row0):
    i = pl.program_id(0)
    tt = x1_ref.shape[0]
    n_tok = pos_ref.shape[0] // TOP_K

    def start_rows(step, slot, first, count):
        for k in range(TOP_K):
            base = k * n_tok + row0 + step * tt
            for r in range(count):
                pltpu.make_async_copy(y_hbm.at[pl.ds(pos_ref[base + first + r], 1)],
                                      buf_ref.at[slot, k, pl.ds(first + r, 1)],
                                      sem_ref.at[slot]).start(priority=r % 2)

    @pl.when(i == 0)
    def _():
        def prime(c, carry):
            start_rows(0, 0, c * SUBLANES, SUBLANES)
            return carry

        lax.fori_loop(0, tt // SUBLANES, prime, 0)

    slot = i % 2
    for k in range(TOP_K):
        _row_gather_wait(tt, y_hbm, buf_ref.at[slot, k], sem_ref.at[slot])
    per_row_gate = gt_ref.shape[0] == tt

    half = x1_ref.shape[1] // 2

    def row_group(c, carry, prefetch):
        if prefetch:
            start_rows(i + 1, 1 - slot, c * SUBLANES, SUBLANES)
        rs = pl.ds(pl.multiple_of(c * SUBLANES, SUBLANES), SUBLANES)
        moe_lo = jnp.zeros((SUBLANES, half), F32)
        moe_hi = jnp.zeros((SUBLANES, half), F32)
        for k in range(TOP_K):
            y_lo, y_hi = _unpack_bf16_halves(buf_ref[slot, k, rs, :])
            moe_lo = moe_lo + w_ref[rs, k:k + 1] * y_lo
            moe_hi = moe_hi + w_ref[rs, k:k + 1] * y_hi
        gate = gt_ref[rs, :] if per_row_gate else gt_ref[0:1, :]
        x2_lo = x1_ref[rs, 0:half] + gate[:, 0:half] * moe_lo
        x2_hi = x1_ref[rs, half:] + gate[:, half:] * moe_hi
        ms = (jnp.sum(x2_lo * x2_lo, axis=-1, keepdims=True)
              + jnp.sum(x2_hi * x2_hi, axis=-1, keepdims=True)) / x1_ref.shape[1]
        scale = lax.rsqrt(ms + EPS)
        o_ref[rs, 0:half] = x2_lo * scale * g_ref[:, 0:half]
        o_ref[rs, half:] = x2_hi * scale * g_ref[:, half:]
        return carry

    has_next = i + 1 < pl.num_programs(0)
    unroll = _tile(tt // SUBLANES, 8, 1)

    @pl.when(has_next)
    def _():
        lax.fori_loop(0, tt // SUBLANES, functools.partial(row_group, prefetch=True), 0, unroll=unroll)

    @pl.when(jnp.logical_not(has_next))
    def _():
        lax.fori_loop(0, tt // SUBLANES, functools.partial(row_group, prefetch=False), 0, unroll=unroll)


class _Rows:
    def __init__(self, np_, t_p, ns, t_s, d):
        self.np_, self.t_p, self.ns, self.t_s, self.d = np_, t_p, ns, t_s, d
        self.tt_p = _tile(t_p, 256, 16)
        self.tt_s = _tile(ns, 128, 16)
        assert np_ % self.tt_s == 0
        self.n = np_ + ns

    def grid(self, sample):
        return (self.ns // self.tt_s,) if sample else (self.np_ // self.tt_p,)

    def rows(self, sample, width, offset=True):
        if sample:
            base = self.np_ // self.tt_s if offset else 0
            return pl.BlockSpec((self.tt_s, width), lambda i: (base + i, 0))
        return pl.BlockSpec((self.tt_p, width), lambda i: (i, 0))

    def mod(self, sample, which):
        if sample:
            return pl.BlockSpec((self.tt_s, self.d), lambda i: (i, which))
        per = self.t_p // self.tt_p
        base = self.ns // SUBLANES
        return pl.BlockSpec((SUBLANES, self.d), lambda i: (base + i // per, which))

    def const(self, shape):
        return pl.BlockSpec(shape, lambda i: tuple(0 for _ in shape))


def _rowwise(body, rows, name, prompt_in, sample_in, specs, out_widths, out_dtypes):
    outs = None
    for sample, ops in ((False, prompt_in), (True, sample_in)):
        in_specs = [s(sample) for s in specs]
        ops = list(ops)
        aliases = {}
        if outs is not None:
            for k, o in enumerate(outs):
                aliases[len(ops)] = k
                ops.append(o)
                in_specs.append(pl.BlockSpec(memory_space=pl.ANY))
        outs = pl.pallas_call(
            body,
            grid=rows.grid(sample),
            in_specs=in_specs,
            out_specs=[rows.rows(sample, w) for w in out_widths],
            out_shape=[jax.ShapeDtypeStruct((rows.n, w), dt) for w, dt in zip(out_widths, out_dtypes)],
            input_output_aliases=aliases,
            compiler_params=_params("arbitrary"),
            name=name + ("_sample" if sample else "_prompt"),
        )(*ops)
    return outs


def _mm_body(x_ref, w_ref, o_ref, wb_ref):
    @pl.when(pl.program_id(1) == 0)
    def _():
        wb_ref[...] = w_ref[...].astype(BF16)
    o_ref[...] = jnp.dot(x_ref[...], wb_ref[...], preferred_element_type=F32).astype(o_ref.dtype)


def _mm_pw2_body(x_ref, w_ref, b_ref, ma_ref, gb_ref, o_ref, wb_ref):
    @pl.when(pl.program_id(1) == 0)
    def _():
        wb_ref[...] = w_ref[...].astype(BF16)
    yb = jnp.dot(x_ref[...], wb_ref[...], preferred_element_type=F32) + b_ref[...]
    o_ref[...] = (ma_ref[...].astype(F32) + _sigmoid(gb_ref[...]) * yb).astype(o_ref.dtype)


def _mm_call(x, w, out_dtype, tm_target, tn_target, name):
    m, k = x.shape
    n = w.shape[1]
    tm = _tile(m, tm_target, 16)
    tn = _tile(n, tn_target, LANES)
    return pl.pallas_call(
        _mm_body,
        grid=(n // tn, m // tm),
        in_specs=[pl.BlockSpec((tm, k), lambda j, i: (i, 0)),
                  pl.BlockSpec((k, tn), lambda j, i: (0, j))],
        out_specs=pl.BlockSpec((tm, tn), lambda j, i: (i, j)),
        out_shape=jax.ShapeDtypeStruct((m, n), out_dtype),
        scratch_shapes=[pltpu.VMEM((k, tn), BF16)],
        compiler_params=_params("arbitrary", "arbitrary"),
        name=name,
    )(x, w)


def _mm_pw2_call(x, w, bias, ma, proj, gate_b_col):
    m, k = x.shape
    n = w.shape[1]
    tm = _tile(m, 544, 16)
    tn = _tile(n, 1024, LANES)
    assert gate_b_col % tn == 0
    gb0 = gate_b_col // tn
    return pl.pallas_call(
        _mm_pw2_body,
        grid=(n // tn, m // tm),
        in_specs=[pl.BlockSpec((tm, k), lambda j, i: (i, 0)),
                  pl.BlockSpec((k, tn), lambda j, i: (0, j)),
                  pl.BlockSpec((1, tn), lambda j, i: (0, j)),
                  pl.BlockSpec((tm, tn), lambda j, i: (i, j)),
                  pl.BlockSpec((tm, tn), lambda j, i: (i, gb0 + j))],
        out_specs=pl.BlockSpec((tm, tn), lambda j, i: (i, j)),
        out_shape=jax.ShapeDtypeStruct((m, n), BF16),
        scratch_shapes=[pltpu.VMEM((k, tn), BF16)],
        compiler_params=_params("arbitrary", "arbitrary"),
        name="pw2",
    )(x, w, bias.reshape(1, n), ma, proj)


def _bf16_terms(x):
    hi = x.astype(BF16).astype(F32)
    r = x - hi
    mid = r.astype(BF16).astype(F32)
    return hi, mid, r - mid


def _gate_epilogue(o, gon, og, ga):
    ya = _rms(o, gon) * _silu(og)
    return (_sigmoid(ga) * ya).astype(BF16)


def _hgrn_prompt_body(q_ref, f_ref, i_ref, og_ref, ga_ref, lb_ref, gon_ref, ma_ref, so_ref,
                      st_ref, b_scr, k_scr, *, chunk, n_sub, hb):
    c_idx = pl.program_id(2)
    C = chunk

    @pl.when(c_idx == 0)
    def _():
        st_ref[...] = jnp.zeros_like(st_ref)

    row_i = lax.broadcasted_iota(jnp.int32, (C, C), 0)
    col_i = lax.broadcasted_iota(jnp.int32, (C, C), 1)
    causal = row_i >= col_i
    tri = causal.astype(BF16)

    lb = lb_ref[...]
    f = _forget_gate(f_ref[...], lb)
    log_f = jnp.log(f)
    k_scr[...] = 1.0 - f
    spread = jnp.zeros((1, hb * LANES), F32)
    for c in range(n_sub):
        rows = slice(c * C, (c + 1) * C)
        b = sum(jnp.dot(tri, term.astype(BF16), preferred_element_type=F32)
                for term in _bf16_terms(log_f[rows]))
        b_scr[rows, :] = b
        b_mid = b[C // 2 - 1:C // 2, :]
        spread = jnp.maximum(spread, jnp.maximum(b[0:1, :] - b_mid, b_mid - b[C - 1:C, :]))
    factored_ok = jnp.max(spread) < FACTORED_DECAY_MAX_SPREAD

    def pairwise_scores(q, c, sl):
        rows = slice(c * C, (c + 1) * C)
        b = b_scr[rows, sl]
        t_i = lax.broadcasted_iota(jnp.int32, (C, 1), 0)
        s_i = lax.broadcasted_iota(jnp.int32, (1, C), 1)

        def column_group(g, acc):
            grp = pl.ds(pl.multiple_of(c * C + g * SUBLANES, SUBLANES), SUBLANES)
            b_g = b_scr[grp, sl]
            k_g = k_scr[grp, sl]
            for r in range(SUBLANES):
                s = g * SUBLANES + r
                decay = jnp.exp(jnp.where(t_i >= s, b - b_g[r:r + 1], -jnp.inf))
                col = jnp.sum(q * k_g[r:r + 1] * decay, axis=-1, keepdims=True)
                acc = jnp.where(s_i == s, col, acc)
            return acc

        return lax.fori_loop(0, C // SUBLANES, column_group, jnp.zeros((C, C), F32))

    nt = (((1,), (1,)), ((), ()))

    def run(factored):
        for hh in range(hb):
            sl = slice(hh * LANES, (hh + 1) * LANES)
            st = st_ref[hh]
            for c in range(n_sub):
                rows = slice(c * C, (c + 1) * C)
                q = q_ref[rows, sl]
                b = b_scr[rows, sl]
                k = k_scr[rows, sl]
                b_last = b[C - 1:C, :]
                if factored:
                    b_mid = b[C // 2 - 1:C // 2, :]
                    q_dec = q * jnp.exp(b - b_mid)
                    k_dec = k * jnp.exp(b_mid - b)
                    scores = lax.dot_general(q_dec.astype(BF16), k_dec.astype(BF16), nt,
                                             preferred_element_type=F32)
                    scores = jnp.where(causal, scores, 0.0)
                    qe = q_dec * jnp.exp(b_mid)
                    kl = k_dec * jnp.exp(b_last - b_mid)
                else:
                    scores = pairwise_scores(q, c, sl)
                    qe = q * jnp.exp(b)
                    kl = k * jnp.exp(b_last - b)
                v = i_ref[rows, sl]
                intra = jnp.dot(scores.astype(BF16), v.astype(BF16), preferred_element_type=F32)
                inter = lax.dot_general(qe.astype(BF16), st.astype(BF16), nt, preferred_element_type=F32)
                st = st * jnp.exp(b_last) + jnp.dot(v.T.astype(BF16), kl.astype(BF16),
                                                    preferred_element_type=F32)
                ma_ref[rows, sl] = _gate_epilogue(inter + intra, gon_ref[:, sl],
                                                  og_ref[rows, sl], ga_ref[rows, sl])
            st_ref[hh] = st

    @pl.when(factored_ok)
    def _():
        run(True)

    @pl.when(jnp.logical_not(factored_ok))
    def _():
        run(False)

    @pl.when(c_idx == pl.num_programs(2) - 1)
    def _():
        for hh in range(hb):
            so_ref[hh] = st_ref[hh].T


def _hgrn_prompt_call(proj, lb, g_onorm, bsz, t, n_heads, d_conv, n_rows):
    d = n_heads * LANES
    chunk = _tile(t, 128, LANES)
    n_sub = 2 if t % (2 * chunk) == 0 else 1
    tc = chunk * n_sub
    hb = _tile(n_heads, 16, 1)
    w = hb * LANES
    hcols = n_heads // hb
    ga0 = (4 * d + 2 * d_conv) // w
    per_b = t // tc

    def col(base):
        return pl.BlockSpec((tc, w), lambda b, h, c: (b * per_b + c, base + h))

    body = functools.partial(_hgrn_prompt_body, chunk=chunk, n_sub=n_sub, hb=hb)
    return pl.pallas_call(
        body,
        grid=(bsz, hcols, per_b),
        in_specs=[col(0), col(hcols), col(2 * hcols), col(3 * hcols), col(ga0),
                  pl.BlockSpec((1, w), lambda b, h, c: (0, h)),
                  pl.BlockSpec((1, w), lambda b, h, c: (0, h))],
        out_specs=[pl.BlockSpec((tc, w), lambda b, h, c: (b * per_b + c, h)),
                   pl.BlockSpec((None, hb, LANES, LANES), lambda b, h, c: (b, h, 0, 0))],
        out_shape=[jax.ShapeDtypeStruct((n_rows, d), BF16),
                   jax.ShapeDtypeStruct((bsz, n_heads, LANES, LANES), F32)],
        scratch_shapes=[pltpu.VMEM((hb, LANES, LANES), F32)] + [pltpu.VMEM((tc, w), F32)] * 2,
        compiler_params=_params("arbitrary", "arbitrary", "arbitrary"),
        name="hgrn_prompt",
    )(proj, proj, proj, proj, proj, lb, g_onorm)


def _hgrn_sample_body(q_ref, f_ref, i_ref, og_ref, ga_ref, lb_ref, gon_ref, s_ref, ma_in_ref,
                      ma_ref, so_ref, *, t_s, n_heads):
    del ma_in_ref
    nb = SUBLANES // t_s
    row = lax.broadcasted_iota(jnp.int32, (SUBLANES, 1), 0)
    tok = row % t_s
    contract0 = (((0,), (0,)), ((), ()))

    def head(h, carry):
        for g in range(q_ref.shape[0] // SUBLANES):
            head_rows(h, g)
        return carry

    def head_rows(h, g):
        sl = pl.ds(pl.multiple_of(h * LANES, LANES), LANES)
        rs = slice(g * SUBLANES, (g + 1) * SUBLANES)
        q = q_ref[rs, sl]
        v = i_ref[rs, sl]
        lb = lb_ref[:, sl]
        f = _forget_gate(f_ref[rs, sl], lb)
        log_f = jnp.log(f)
        k = 1.0 - f
        b = log_f
        for d in range(1, t_s):
            b = b + jnp.where(tok >= d, pltpu.roll(log_f, d, 0), 0.0)
        o = jnp.zeros((SUBLANES, LANES), F32)
        for d in range(t_s):
            kd, bd, vd = (k, b, v) if d == 0 else (pltpu.roll(k, d, 0), pltpu.roll(b, d, 0),
                                                  pltpu.roll(v, d, 0))
            decay = jnp.exp(jnp.where(tok >= d, b - bd, -jnp.inf))
            score = jnp.sum(q * kd * decay, axis=-1, keepdims=True)
            o = o + score * jnp.where(tok >= d, vd, 0.0)
        qe = (q * jnp.exp(b)).astype(BF16)
        for bi in range(nb):
            in_b = (row >= bi * t_s) & (row < (bi + 1) * t_s)
            s0 = s_ref[g * nb + bi, h]
            inter = jnp.dot(qe, s0.astype(BF16), preferred_element_type=F32)
            o = o + jnp.where(in_b, inter, 0.0)
            b_last = b[(bi + 1) * t_s - 1:(bi + 1) * t_s, :]
            kk = jnp.where(in_b, k * jnp.exp(jnp.where(in_b, b_last - b, 0.0)), 0.0)
            vv = jnp.where(in_b, v, 0.0)
            upd = lax.dot_general(kk.astype(BF16), vv.astype(BF16), contract0,
                                  preferred_element_type=F32)
            hi, mid, lo = _bf16_terms(jnp.exp(b_last))
            d_rows = jnp.where(row == 0, hi, jnp.where(row == 1, mid, jnp.where(row == 2, lo, 0.0)))
            one_rows = jnp.where(row < 3, jnp.ones((SUBLANES, LANES), F32), 0.0)
            decay_kv = lax.dot_general(d_rows.astype(BF16), one_rows.astype(BF16), contract0,
                                       preferred_element_type=F32)
            so_ref[g * nb + bi, h] = s0 * decay_kv + upd
        ma_ref[rs, sl] = _gate_epilogue(o, gon_ref[:, sl], og_ref[rs, sl], ga_ref[rs, sl])

    lax.fori_loop(0, n_heads, head, 0, unroll=_tile(n_heads, 8, 1))


def _hgrn_sample_call(proj, lb, g_onorm, state, ma, row0, t_s, d_conv):
    bs, n_heads = state.shape[0], state.shape[1]
    d = n_heads * LANES
    assert SUBLANES % t_s == 0
    groups = 2 if bs * t_s % (2 * SUBLANES) == 0 else 1
    rows = groups * SUBLANES
    nb = rows // t_s
    assert row0 % rows == 0 and bs % nb == 0
    r0 = row0 // rows
    ga0 = (4 * d + 2 * d_conv) // d

    def col(base):
        return pl.BlockSpec((rows, d), lambda i: (r0 + i, base))

    body = functools.partial(_hgrn_sample_body, t_s=t_s, n_heads=n_heads)
    return pl.pallas_call(
        body,
        grid=(bs // nb,),
        in_specs=[col(0), col(1), col(2), col(3), col(ga0),
                  pl.BlockSpec((1, d), lambda i: (0, 0)),
                  pl.BlockSpec((1, d), lambda i: (0, 0)),
                  pl.BlockSpec((nb, n_heads, LANES, LANES), lambda i: (i, 0, 0, 0)),
                  pl.BlockSpec(memory_space=pl.ANY)],
        out_specs=[pl.BlockSpec((rows, d), lambda i: (r0 + i, 0)),
                   pl.BlockSpec((nb, n_heads, LANES, LANES), lambda i: (i, 0, 0, 0))],
        out_shape=[jax.ShapeDtypeStruct(ma.shape, ma.dtype),
                   jax.ShapeDtypeStruct(state.shape, F32)],
        input_output_aliases={8: 0},
        compiler_params=_params("arbitrary"),
        name="hgrn_sample",
    )(proj, proj, proj, proj, proj, lb, g_onorm, state, ma)


def _ln_swish(x, g, b):
    mu = jnp.mean(x, axis=-1, keepdims=True)
    xc = x - mu
    var = jnp.mean(xc * xc, axis=-1, keepdims=True)
    return _silu(xc * lax.rsqrt(var + EPS) * g + b)


def _conv_prompt_body(ga_ref, gb_ref, w_ref, bdw_ref, lg_ref, lbias_ref, o_ref, cache_ref,
                      ext_ref, conv_ref, *, width, hist, rc):
    t_idx = pl.program_id(1)
    tt = ga_ref.shape[0]
    dc = ga_ref.shape[1]

    @pl.when(t_idx == 0)
    def _():
        ext_ref[0:hist, :] = jnp.zeros((hist, dc), F32)

    @pl.when(t_idx > 0)
    def _():
        ext_ref[0:hist, :] = ext_ref[tt:tt + hist, :]

    ext_ref[hist:hist + tt, :] = ga_ref[...] * _sigmoid(gb_ref[...])
    off = hist - (width - 1)

    def strip(l, carry):
        lanes = pl.ds(pl.multiple_of(l * LANES, LANES), LANES)
        for r0 in range(0, tt, rc):
            win = ext_ref[r0:r0 + rc + hist, lanes]
            acc = jnp.zeros((rc, LANES), F32) + bdw_ref[:, lanes]
            for s in range(SUBLANES):
                taps = [j for j in range(width) if (off + j) % SUBLANES == s]
                if not taps:
                    continue
                shifted = win if s == 0 else pltpu.roll(win, rc + hist - s, 0)
                for j in taps:
                    a = (off + j) // SUBLANES * SUBLANES
                    acc = acc + w_ref[j:j + 1, lanes] * shifted[a:a + rc]
            conv_ref[r0:r0 + rc, lanes] = acc
        return carry

    lax.fori_loop(0, dc // LANES, strip, 0)

    def norm(i, carry):
        rows = pl.ds(pl.multiple_of(i * rc, rc), rc)
        o_ref[rows, :] = _ln_swish(conv_ref[rows, :], lg_ref[...], lbias_ref[...]).astype(o_ref.dtype)
        return carry

    lax.fori_loop(0, tt // rc, norm, 0, unroll=2)

    @pl.when(t_idx == pl.num_programs(1) - 1)
    def _():
        cache_ref[...] = ext_ref[hist + tt - (width - 1):hist + tt, :]


def _conv_prompt_call(proj, w_dw, b_dw, ln_g, ln_b, bsz, t, d_hgrn, n_rows):
    width, dc = w_dw.shape
    tt = _tile(t, 256, 32)
    hist = 32
    assert width - 1 <= hist <= tt and (4 * d_hgrn) % dc == 0
    c0 = 4 * d_hgrn // dc
    per_b = t // tt
    vec = lambda a: a.reshape(1, dc)
    body = functools.partial(_conv_prompt_body, width=width, hist=hist, rc=32)
    return pl.pallas_call(
        body,
        grid=(bsz, per_b),
        in_specs=[pl.BlockSpec((tt, dc), lambda b, i: (b * per_b + i, c0)),
                  pl.BlockSpec((tt, dc), lambda b, i: (b * per_b + i, c0 + 1)),
                  pl.BlockSpec((width, dc), lambda b, i: (0, 0)),
                  pl.BlockSpec((1, dc), lambda b, i: (0, 0)),
                  pl.BlockSpec((1, dc), lambda b, i: (0, 0)),
                  pl.BlockSpec((1, dc), lambda b, i: (0, 0))],
        out_specs=[pl.BlockSpec((tt, dc), lambda b, i: (b * per_b + i, 0)),
                   pl.BlockSpec((None, width - 1, dc), lambda b, i: (b, 0, 0))],
        out_shape=[jax.ShapeDtypeStruct((n_rows, dc), BF16),
                   jax.ShapeDtypeStruct((bsz, width - 1, dc), F32)],
        scratch_shapes=[pltpu.VMEM((hist + tt, dc), F32), pltpu.VMEM((tt, dc), F32)],
        compiler_params=_params("arbitrary", "arbitrary"),
        name="conv_prompt",
    )(proj, proj, w_dw, vec(b_dw), vec(ln_g), vec(ln_b))


def _conv_sample_body(ga_ref, gb_ref, cache_ref, w_ref, bdw_ref, lg_ref, lbias_ref, vc_in_ref,
                      o_ref, ncache_ref, ext_ref, *, width, t_s, bb):
    del vc_in_ref
    u = ga_ref[...] * _sigmoid(gb_ref[...])
    for bi in range(bb):
        ext_ref[0:width - 1, :] = cache_ref[bi]
        ext_ref[width - 1:width - 1 + t_s, :] = u[bi * t_s:(bi + 1) * t_s]
        acc = jnp.zeros((t_s, u.shape[1]), F32) + bdw_ref[...]
        for j in range(width):
            acc = acc + w_ref[j:j + 1, :] * ext_ref[j:j + t_s, :]
        ncache_ref[bi] = ext_ref[t_s:t_s + width - 1, :]
        o_ref[bi * t_s:(bi + 1) * t_s, :] = _ln_swish(acc, lg_ref[...], lbias_ref[...]).astype(o_ref.dtype)


def _conv_sample_call(proj, cache, w_dw, b_dw, ln_g, ln_b, vconv, row0, t_s, d_hgrn):
    width, dc = w_dw.shape
    bs = cache.shape[0]
    bb = _tile(bs, 8, 1)
    rows = bb * t_s
    assert rows % 16 == 0 and row0 % rows == 0
    r0 = row0 // rows
    c0 = 4 * d_hgrn // dc
    vec = lambda a: a.reshape(1, dc)
    body = functools.partial(_conv_sample_body, width=width, t_s=t_s, bb=bb)
    return pl.pallas_call(
        body,
        grid=(bs // bb,),
        in_specs=[pl.BlockSpec((rows, dc), lambda i: (r0 + i, c0)),
                  pl.BlockSpec((rows, dc), lambda i: (r0 + i, c0 + 1)),
                  pl.BlockSpec((bb, width - 1, dc), lambda i: (i, 0, 0)),
                  pl.BlockSpec((width, dc), lambda i: (0, 0)),
                  pl.BlockSpec((1, dc), lambda i: (0, 0)),
                  pl.BlockSpec((1, dc), lambda i: (0, 0)),
                  pl.BlockSpec((1, dc), lambda i: (0, 0)),
                  pl.BlockSpec(memory_space=pl.ANY)],
        out_specs=[pl.BlockSpec((rows, dc), lambda i: (r0 + i, 0)),
                   pl.BlockSpec((bb, width - 1, dc), lambda i: (i, 0, 0))],
        out_shape=[jax.ShapeDtypeStruct(vconv.shape, vconv.dtype),
                   jax.ShapeDtypeStruct(cache.shape, F32)],
        scratch_shapes=[pltpu.VMEM((width - 1 + t_s + SUBLANES, dc), F32)],
        input_output_aliases={7: 0},
        compiler_params=_params("arbitrary"),
        name="conv_sample",
    )(proj, proj, cache, w_dw, vec(b_dw), vec(ln_g), vec(ln_b), vconv)


def _expert_changed(be_ref, r):
    return jnp.logical_or(r == 0, be_ref[r] != be_ref[jnp.maximum(r - 1, 0)])


def _weight_copies(w_hbms, buf_refs, sem_ref, e, j, slot):
    copies = []
    for w_hbm, buf_ref in zip(w_hbms, buf_refs):
        tcol = buf_ref.shape[-1]
        cols = pl.ds(pl.multiple_of(j * tcol, tcol), tcol)
        copies.append(pltpu.make_async_copy(w_hbm.at[e, :, cols], buf_ref.at[slot], sem_ref.at[slot]))
    return copies


def _stage_expert_weights(be_ref, nr_ref, nx_ref, w_hbms, buf_refs, bf16_refs, sem_ref, cnt_ref):
    j, r = pl.program_id(0), pl.program_id(1)

    @pl.when(jnp.logical_and(j == 0, r == 0))
    def _():
        cnt_ref[0] = 0
        for c in _weight_copies(w_hbms, buf_refs, sem_ref, be_ref[0], 0, 0):
            c.start()

    @pl.when(jnp.logical_and(r < nr_ref[0], _expert_changed(be_ref, r)))
    def _():
        slot = cnt_ref[0] % 2
        for c in _weight_copies(w_hbms, buf_refs, sem_ref, be_ref[r], j, slot):
            c.wait()
        wraps = nx_ref[r] >= nr_ref[0]
        e_next = be_ref[jnp.where(wraps, 0, nx_ref[r])]
        j_next = jnp.where(wraps, j + 1, j)

        @pl.when(j_next < pl.num_programs(0))
        def _():
            for c in _weight_copies(w_hbms, buf_refs, sem_ref, e_next, j_next, 1 - slot):
                c.start()

        for buf_ref, dst_ref in zip(buf_refs, bf16_refs):
            dst_ref[...] = buf_ref[slot].astype(BF16)
        cnt_ref[0] = cnt_ref[0] + 1


def _moe_up_body(be_ref, nr_ref, nx_ref, x_ref, w1_hbm, w3_hbm, g_ref,
                 w1f_ref, w3f_ref, w1b_ref, w3b_ref, sem_ref, cnt_ref):
    _stage_expert_weights(be_ref, nr_ref, nx_ref, (w1_hbm, w3_hbm), (w1f_ref, w3f_ref),
                          (w1b_ref, w3b_ref), sem_ref, cnt_ref)

    @pl.when(pl.program_id(1) < nr_ref[0])
    def _():
        x = x_ref[...]
        a = jnp.dot(x, w1b_ref[...], preferred_element_type=F32)
        b = jnp.dot(x, w3b_ref[...], preferred_element_type=F32)
        g_ref[...] = (_silu(a) * b).astype(g_ref.dtype)


def _moe_down_body(be_ref, nr_ref, nx_ref, g_ref, w2_hbm, y_ref, w2f_ref, w2b_ref, sem_ref, cnt_ref):
    _stage_expert_weights(be_ref, nr_ref, nx_ref, (w2_hbm,), (w2f_ref,), (w2b_ref,), sem_ref, cnt_ref)

    @pl.when(pl.program_id(1) < nr_ref[0])
    def _():
        y_ref[...] = _pack_bf16_halves(jnp.dot(g_ref[...], w2b_ref[...], preferred_element_type=F32))


def _dispatch_body(tok_ref, nr_ref, cnt_ref, first_ref, h2_hbm, o_ref, buf_ref, sem_ref):
    r = pl.program_id(0)
    tm = o_ref.shape[0]

    def start(step, slot):
        def copy_group(c, carry):
            for j in range(ROW_GROUP):
                i = c * ROW_GROUP + j
                pltpu.make_async_copy(h2_hbm.at[pl.ds(tok_ref[first_ref[step] + i], 1)],
                                      buf_ref.at[slot, pl.ds(i, 1)], sem_ref.at[slot]).start(priority=j % 2)
            return carry

        lax.fori_loop(0, cnt_ref[step] // ROW_GROUP, copy_group, 0)

    @pl.when(r == 0)
    def _():
        buf_ref[...] = jnp.zeros_like(buf_ref)
        start(0, 0)

    @pl.when(r + 1 < nr_ref[0])
    def _():
        start(r + 1, (r + 1) % 2)

    @pl.when(r < nr_ref[0])
    def _():
        slot = r % 2
        rows = pl.ds(0, pl.multiple_of(cnt_ref[r], ROW_GROUP))
        pltpu.make_async_copy(h2_hbm.at[rows], buf_ref.at[slot, rows], sem_ref.at[slot]).wait()
        lo, hi = _unpack_bf16_halves(buf_ref[slot])
        half = lo.shape[1]
        o_ref[:, 0:half] = lo.astype(o_ref.dtype)
        o_ref[:, half:] = hi.astype(o_ref.dtype)


def _dispatch_call(h2p, sorted_tok, n_real, block_cnt, block_first, tm):
    half = h2p.shape[1]
    d = 2 * half
    nblk = block_cnt.shape[0]
    return pl.pallas_call(
        _dispatch_body,
        grid_spec=pltpu.PrefetchScalarGridSpec(
            num_scalar_prefetch=4,
            grid=(nblk,),
            in_specs=[pl.BlockSpec(memory_space=pl.ANY)],
            out_specs=pl.BlockSpec((tm, d), lambda r, o, nr, cnt, first: (jnp.minimum(r, nr[0] - 1), 0)),
            scratch_shapes=[pltpu.VMEM((2, tm, half), jnp.uint32), pltpu.SemaphoreType.DMA((2,))]),
        out_shape=jax.ShapeDtypeStruct((nblk * tm, d), BF16),
        compiler_params=_params("arbitrary"),
        name="moe_dispatch",
    )(sorted_tok, n_real, block_cnt, block_first, h2p)


def _moe_experts(x_sorted, block_e, n_real, next_run, w1, w3, w2, tm):
    l, d = x_sorted.shape
    n_exp, _, dh = w1.shape
    nblk = l // tm
    th = _tile(dh, 512, LANES)
    rr = lambda r, nr: jnp.minimum(r, nr[0] - 1)
    hbm = pl.BlockSpec(memory_space=pl.ANY)

    g = pl.pallas_call(
        _moe_up_body,
        grid_spec=pltpu.PrefetchScalarGridSpec(
            num_scalar_prefetch=3,
            grid=(dh // th, nblk),
            in_specs=[pl.BlockSpec((tm, d), lambda j, r, be, nr, nx: (rr(r, nr), 0)), hbm, hbm],
            out_specs=pl.BlockSpec((tm, th), lambda j, r, be, nr, nx: (rr(r, nr), j)),
            scratch_shapes=[pltpu.VMEM((2, d, th), F32), pltpu.VMEM((2, d, th), F32),
                            pltpu.VMEM((d, th), BF16), pltpu.VMEM((d, th), BF16),
                            pltpu.SemaphoreType.DMA((2,)), pltpu.SMEM((1,), jnp.int32)]),
        out_shape=jax.ShapeDtypeStruct((l, dh), BF16),
        compiler_params=_params("arbitrary", "arbitrary"),
        name="moe_up",
    )(block_e, n_real, next_run, x_sorted, w1, w3)

    return pl.pallas_call(
        _moe_down_body,
        grid_spec=pltpu.PrefetchScalarGridSpec(
            num_scalar_prefetch=3,
            grid=(1, nblk),
            in_specs=[pl.BlockSpec((tm, dh), lambda j, r, be, nr, nx: (rr(r, nr), 0)), hbm],
            out_specs=pl.BlockSpec((tm, d // 2), lambda j, r, be, nr, nx: (rr(r, nr), 0)),
            scratch_shapes=[pltpu.VMEM((2, dh, d), F32), pltpu.VMEM((dh, d), BF16),
                            pltpu.SemaphoreType.DMA((2,)), pltpu.SMEM((1,), jnp.int32)]),
        out_shape=jax.ShapeDtypeStruct((l, d // 2), jnp.uint32),
        compiler_params=_params("arbitrary", "arbitrary"),
        name="moe_down",
    )(block_e, n_real, next_run, g, w2)


def _route(logits, n_groups, n_experts, tm):
    n = logits.shape[0]
    eg = n_experts // n_groups
    l1 = logits[:, :n_groups]
    p1 = jax.nn.softmax(l1, axis=-1)
    grp = jnp.argmax(l1, axis=-1)
    p_grp = jnp.max(p1, axis=-1, keepdims=True)
    l2 = logits[:, n_groups:n_groups + n_experts].reshape(n, n_groups, eg)
    l2g = jnp.take_along_axis(l2, grp[:, None, None], axis=1)[:, 0]
    top_v, top_i = lax.top_k(l2g, TOP_K)
    weights = p_grp * jax.nn.softmax(top_v, axis=-1)
    expert = (grp[:, None] * eg + top_i).astype(jnp.int32)

    a = n * TOP_K
    i32 = lambda v: v.astype(jnp.int32)
    e = expert.reshape(-1)
    counts = i32(jnp.bincount(e, length=n_experts))
    padded = (counts + tm - 1) // tm * tm
    ends = jnp.cumsum(padded)
    starts = ends - padded
    seg_start = jnp.cumsum(counts) - counts
    order = jnp.argsort(e)
    rank = jnp.argsort(order)
    pos = i32(starts[e] + rank - seg_start[e]).reshape(n, TOP_K)
    nblk = -(-a // tm) + n_experts
    block_e = i32(jnp.minimum(jnp.searchsorted(ends, jnp.arange(nblk) * tm, side='right'),
                              n_experts - 1))
    n_real = i32(ends[-1:] // tm)
    block_cnt = jnp.clip(counts[block_e] - (jnp.arange(nblk) * tm - starts[block_e]), 0, tm)
    block_cnt = i32(-(-block_cnt // ROW_GROUP) * ROW_GROUP)
    next_run = i32(ends[block_e] // tm)
    block_first = i32(jnp.minimum(seg_start[block_e] + jnp.arange(nblk) * tm - starts[block_e], a))
    sorted_tok = jnp.pad(i32(order // TOP_K), (0, ROW_GROUP))
    return weights, sorted_tok, pos, block_e, n_real, block_cnt, block_first, next_run


def kernel(x_prompt, x_sample, state_hgrn, cache_conv, c_prompt, c_sample, g_mix, w_ada, b_ada, w_in, lb_param, g_onorm, w_dw, b_dw, ln_g, ln_b, w_pw2, b_pw2, w_out, g_ffn, w_r1, b_r1, w_r2, b_r2, moe_w1, moe_w3, moe_w2, g_final):
    bp, t_p, d = x_prompt.shape
    bs, t_s, _ = x_sample.shape
    depth, _, n_heads, head_k, head_v = state_hgrn.shape
    assert depth == 1 and head_k == LANES and head_v == LANES
    d_hgrn = n_heads * head_k
    d_conv = w_dw.shape[-1]
    n_groups, n_experts = w_r1.shape[-1], w_r2.shape[-1]
    np_, ns = bp * t_p, bs * t_s
    n = np_ + ns
    rows = _Rows(np_, t_p, ns, t_s, d)
    vec = lambda a: a.reshape(1, -1)

    assert bs % SUBLANES == 0
    c_all = jnp.concatenate([c_sample, c_prompt], axis=0)
    c_all = jnp.pad(c_all, ((0, -c_all.shape[0] % 16), (0, 0)))
    ada = _ada_call(c_all, w_ada[0], b_ada[0], bs, t_s, bp)
    ada_p = ada_s = ada
    SH_M, SC_M, GT_M, SH_F, SC_F, GT_F = range(6)
    mod = lambda which: (lambda sample: rows.mod(sample, which))
    ada_of = lambda sample: ada
    xs = (x_prompt.reshape(np_, d), x_sample.reshape(ns, d))
    x_spec = lambda sample: rows.rows(sample, d, offset=False)
    flat = lambda width: (lambda sample: rows.rows(sample, width))
    const = lambda shape: (lambda sample: rows.const(shape))

    (h,) = _rowwise(
        _norm1_body, rows, "norm1",
        [xs[0], vec(g_mix[0]), ada_p, ada_p], [xs[1], vec(g_mix[0]), ada_s, ada_s],
        [x_spec, const((1, d)), mod(SC_M), mod(SH_M)], [d], [BF16])
    proj = _mm_call(h, w_in[0], F32, 544, 1024, "in_proj")

    lb = jax.nn.softmax(lb_param.astype(F32), axis=0)[0:1]
    ma, state_p = _hgrn_prompt_call(proj, lb, vec(g_onorm[0]), bp, t_p, n_heads, d_conv, n)
    ma, state_s = _hgrn_sample_call(proj, lb, vec(g_onorm[0]), state_hgrn[0], ma, np_, t_s, d_conv)

    vconv, cache_p = _conv_prompt_call(proj, w_dw[0], b_dw[0], ln_g[0], ln_b[0], bp, t_p, d_hgrn, n)
    vconv, cache_s = _conv_sample_call(proj, cache_conv[0], w_dw[0], b_dw[0], ln_g[0], ln_b[0],
                                       vconv, np_, t_s, d_hgrn)

    merged = _mm_pw2_call(vconv, w_pw2[0], b_pw2[0], ma, proj, 4 * d_hgrn + 2 * d_conv + d)
    attn = _mm_call(merged, w_out[0], F32, 544, 1024, "out_proj")

    n_r = n_groups + n_experts
    n_r_pad = -(-n_r // LANES) * LANES
    w_r = jnp.concatenate([w_r1[0], w_r2[0], jnp.zeros((d, n_r_pad - n_r), F32)], axis=1)
    b_r = jnp.concatenate([b_r1[0], b_r2[0], jnp.zeros((n_r_pad - n_r,), F32)]).reshape(1, n_r_pad)
    x1, h2p, logits = _rowwise(
        _norm2_body, rows, "norm2",
        [xs[0], attn, ada_p, vec(g_ffn[0]), ada_p, ada_p, w_r, b_r],
        [xs[1], attn, ada_s, vec(g_ffn[0]), ada_s, ada_s, w_r, b_r],
        [x_spec, flat(d), mod(GT_M), const((1, d)), mod(SC_F), mod(SH_F),
         const((d, n_r_pad)), const((1, n_r_pad))],
        [d, d // 2, n_r_pad], [F32, jnp.uint32, F32])

    tm = 256
    weights, sorted_tok, pos, block_e, n_real, block_cnt, block_first, next_run = _route(
        logits, n_groups, n_experts, tm)
    x_sorted = _dispatch_call(h2p, sorted_tok, n_real, block_cnt, block_first, tm)
    y_sorted = _moe_experts(x_sorted, block_e, n_real, next_run, moe_w1[0], moe_w3[0], moe_w2[0], tm)
    pos_flat = pos.T.reshape(-1)

    outs = []
    for sample in (False, True):
        tt = rows.tt_s if sample else rows.tt_p
        wrap = lambda spec: pl.BlockSpec(spec.block_shape, lambda i, p, f=spec.index_map: f(i))
        outs.append(pl.pallas_call(
            functools.partial(_final_body, row0=np_ if sample else 0),
            grid_spec=pltpu.PrefetchScalarGridSpec(
                num_scalar_prefetch=1,
                grid=rows.grid(sample),
                in_specs=[wrap(rows.rows(sample, d)), wrap(rows.rows(sample, TOP_K)),
                          wrap(rows.mod(sample, GT_F)), wrap(rows.const((1, d))),
                          pl.BlockSpec(memory_space=pl.ANY)],
                out_specs=wrap(rows.rows(sample, d, offset=False)),
                scratch_shapes=[pltpu.VMEM((2, TOP_K, tt, d // 2), jnp.uint32),
                                pltpu.SemaphoreType.DMA((2,))]),
            out_shape=jax.ShapeDtypeStruct((ns if sample else np_, d), F32),
            compiler_params=_params("arbitrary"),
            name="final_sample" if sample else "final_prompt",
        )(pos_flat, x1, weights, ada_of(sample), vec(g_final), y_sorted))
    y_prompt = outs[0].reshape(bp, t_p, d)
    y_sample = outs[1].reshape(bs, t_s, d)
    return (y_prompt, y_sample, state_p[None], cache_p[None], state_s[None], cache_s[None])
```
